```python
import math
import jax
import jax.numpy as jnp
from jax import lax
import numpy as np

D_MODEL = 1024
BATCH = 8
SEQ = 4096
DEPTH = 2

GRID_W = 64
CTX_LEN = 256
EPS = 1e-6
F32 = jnp.float32

DA_HEADS = 4
DA_HEAD_DIM = D_MODEL // 16
DA_V_DIM = 2 * DA_HEAD_DIM
DA_WIDTH = DA_HEADS * DA_V_DIM
ROPE_THETA = 10000.0
Q_BLOCK = 128

S5_WIDTH = D_MODEL // 4
S5_GROUP = 16
S5_GROUPS = S5_WIDTH // S5_GROUP
S5_STATE = 64
S5_DT_MIN = 0.001
S5_DT_MAX = 0.1

GDN_HEADS = 4
GDN_DK = D_MODEL // 16
GDN_DV = D_MODEL // 16
GDN_WIDTH = GDN_HEADS * GDN_DV
GDN_CONV = 5
GDN_CHUNK = 64

D_MIX = DA_WIDTH + S5_WIDTH + GDN_WIDTH

DA_QK_W = 2 * DA_HEADS * DA_HEAD_DIM
GDN_QKV_W = 2 * GDN_HEADS * GDN_DK + GDN_HEADS * GDN_DV
IN_SIZES = (DA_QK_W, DA_QK_W, DA_WIDTH, S5_WIDTH, GDN_QKV_W, GDN_WIDTH, 2 * GDN_HEADS, 2 * GDN_HEADS)
IN_WIDTH = sum(IN_SIZES)
IN_OFFSETS = tuple(int(o) for o in np.cumsum(IN_SIZES)[:-1])

N_EXPERTS = 64
TOP_K = 8
N_GROUPS = 8
TOPK_GROUPS = 4
MOE_FFN = D_MODEL // 4
ROUTED_SCALE = 2.5

kernel_name = 'hybrid_diffusion_block_da_s5_gdn_moe'


def rmsnorm(x, w):
    xf = x.astype(F32)
    y = xf * lax.rsqrt(jnp.mean(xf * xf, axis=-1, keepdims=True) + EPS)
    return (y * w.astype(F32)).astype(x.dtype)


def modulate(x, w, shift, scale):
    return rmsnorm(x, w) * (1 + scale) + shift


def l2norm(x):
    return x * lax.rsqrt(jnp.sum(x * x, axis=-1, keepdims=True) + EPS)


def axial_rope_tables(row, col, dim, dtype):
    nf = dim // 4
    inv = ROPE_THETA ** (-jnp.arange(nf, dtype=F32) / nf)
    ang_r = row.astype(F32)[:, None] * inv
    ang_c = col.astype(F32)[:, None] * inv
    shp = (row.shape[0], 1, 1, nf)
    return tuple(t.reshape(shp).astype(dtype) for t in (jnp.cos(ang_r), jnp.sin(ang_r), jnp.cos(ang_c), jnp.sin(ang_c)))


def _rotate_half(x, cos, sin):
    x1, x2 = jnp.split(x, 2, axis=-1)
    return jnp.concatenate([x1 * cos - x2 * sin, x1 * sin + x2 * cos], axis=-1)


def apply_axial_rope(x, rope):
    cr, sr, cc, sc = rope
    x_row, x_col = jnp.split(x, 2, axis=-1)
    return jnp.concatenate([_rotate_half(x_row, cr, sr), _rotate_half(x_col, cc, sc)], axis=-1)


def diff_attention(pq, pk, pv, cq, ck, cv, rope, lam_vecs, subln_w, lam_init, ctx_out):
    b, n = pq.shape[:2]
    nc = cq.shape[1]
    h, dh = DA_HEADS, DA_HEAD_DIM
    scale = dh ** -0.5
    q = apply_axial_rope(pq.reshape(b, n, h, 2, dh), rope) * scale
    k = apply_axial_rope(pk.reshape(b, n, h, 2, dh), rope)
    v = pv.reshape(b, n, h, DA_V_DIM)
    qc = cq.reshape(b, nc, h, 2, dh) * scale
    kc = ck.reshape(b, nc, h, 2, dh)
    vc = cv.reshape(b, nc, h, DA_V_DIM)
    lq1, lk1, lq2, lk2 = lam_vecs.astype(F32)
    lam = jnp.exp(jnp.sum(lq1 * lk1)) - jnp.exp(jnp.sum(lq2 * lk2)) + lam_init

    def attend(qb, kk, vv):
        s = jnp.einsum('bqhmd,bkhmd->bhmqk', qb, kk).astype(F32)
        p = jax.nn.softmax(s, axis=-1)
        w = (p[:, :, 0] - lam * p[:, :, 1]).astype(vv.dtype)
        return jnp.einsum('bhqk,bkhe->bqhe', w, vv)

    k_all = jnp.concatenate([kc, k], axis=1)
    v_all = jnp.concatenate([vc, v], axis=1)
    nb = n // Q_BLOCK
    q_blocks = jnp.moveaxis(q.reshape(b, nb, Q_BLOCK, h, 2, dh), 1, 0)
    o = lax.map(lambda qb: attend(qb, k_all, v_all), q_blocks)
    o = jnp.moveaxis(o, 0, 1).reshape(b, n, h, DA_V_DIM)
    out = (rmsnorm(o, subln_w) * (1 - lam_init)).reshape(b, n, DA_WIDTH)
    out_c = None
    if ctx_out:
        oc = attend(qc, kc, vc)
        out_c = (rmsnorm(oc, subln_w) * (1 - lam_init)).reshape(b, nc, DA_WIDTH)
    return out, out_c


def s5_discretize(lam_re, lam_im, log_step, b_re, b_im):
    lr, li = lam_re.astype(F32), lam_im.astype(F32)
    step = jnp.exp(log_step.astype(F32))[:, None]
    mag = jnp.exp(lr * step)
    ab_re, ab_im = mag * jnp.cos(li * step), mag * jnp.sin(li * step)
    den = lr * lr + li * li
    nr, ni = ab_re - 1.0, ab_im
    f_re = (nr * lr + ni * li) / den
    f_im = (ni * lr - nr * li) / den
    br, bi = b_re.astype(F32), b_im.astype(F32)
    bb_re = f_re[..., None] * br - f_im[..., None] * bi
    bb_im = f_re[..., None] * bi + f_im[..., None] * br
    return ab_re, ab_im, bb_re, bb_im


def _complex_affine_combine(e1, e2):
    a1r, a1i, b1r, b1i = e1
    a2r, a2i, b2r, b2i = e2
    return (a2r * a1r - a2i * a1i, a2r * a1i + a2i * a1r,
            a2r * b1r - a2i * b1i + b2r, a2r * b1i + a2i * b1r + b2i)


def s5_scan(u, ab_re, ab_im, bb_re, bb_im, s0, reverse):
    bu_re = jnp.einsum('gph,blgh->blgp', bb_re, u)
    bu_im = jnp.einsum('gph,blgh->blgp', bb_im, u)
    if s0 is not None:
        s_re, s_im = s0
        first = -1 if reverse else 0
        bu_re = bu_re.at[:, first].add(ab_re * s_re - ab_im * s_im)
        bu_im = bu_im.at[:, first].add(ab_re * s_im + ab_im * s_re)
    a_re = jnp.broadcast_to(ab_re, bu_re.shape)
    a_im = jnp.broadcast_to(ab_im, bu_im.shape)
    _, _, x_re, x_im = lax.associative_scan(_complex_affine_combine, (a_re, a_im, bu_re, bu_im), reverse=reverse, axis=1)
    return x_re, x_im


def s5_readout(c_re, c_im, x_re, x_im):
    return (jnp.einsum('ghp,blgp->blgh', c_re.astype(F32), x_re)
            - jnp.einsum('ghp,blgp->blgh', c_im.astype(F32), x_im))


def s5_mixer(u, uc, lam_re, lam_im, log_step, b_re, b_im, c_re, c_im, d_skip, w_glu, b_glu, ctx_out):
    b, n, _ = u.shape
    nc = uc.shape[1]
    uf = u.astype(F32).reshape(b, n, S5_GROUPS, S5_GROUP)
    ucf = uc.astype(F32).reshape(b, nc, S5_GROUPS, S5_GROUP)
    dsk = d_skip.astype(F32)
    y = uf * dsk
    yc = ucf * dsk if ctx_out else None
    for dirn in range(2):
        rev = dirn == 1
        disc = s5_discretize(lam_re[dirn], lam_im[dirn], log_step[dirn], b_re[dirn], b_im[dirn])
        xc_re, xc_im = s5_scan(ucf, *disc, None, rev)
        last = 0 if rev else -1
        x_re, x_im = s5_scan(uf, *disc, (xc_re[:, last], xc_im[:, last]), rev)
        y = y + s5_readout(c_re[dirn], c_im[dirn], x_re, x_im)
        if ctx_out:
            yc = yc + s5_readout(c_re[dirn], c_im[dirn], xc_re, xc_im)

    def glu(yy, m):
        z = jax.nn.gelu(yy.reshape(b, m, S5_WIDTH)).astype(u.dtype)
        return z * jax.nn.sigmoid(z @ w_glu + b_glu)

    return glu(y, n), (glu(yc, nc) if ctx_out else None)


def short_conv(x, w):
    ch, kw = x.shape[-1], w.shape[0]
    return lax.conv_general_dilated(x, w.reshape(kw, 1, ch).astype(x.dtype), window_strides=(1,),
                                    padding=[(kw // 2, kw // 2)], dimension_numbers=('NWC', 'WIO', 'NWC'),
                                    feature_group_count=ch)


def _to_chunks(t, n_chunks):
    b, n, h = t.shape[:3]
    t = t.reshape((b, n_chunks, n // n_chunks, h) + t.shape[3:])
    return jnp.moveaxis(jnp.swapaxes(t, 2, 3), 1, 0)


def gated_delta_chunked(q, k, v, beta, g, s0):
    b, n, h, dk = q.shape
    dv = v.shape[-1]
    c = GDN_CHUNK
    nch = n // c
    qc, kc, vc = _to_chunks(q, nch), _to_chunks(k, nch), _to_chunks(v, nch)
    bc, gc = _to_chunks(beta, nch), _to_chunks(g, nch)
    gcum = jnp.cumsum(gc, axis=-1)
    tri = jnp.tril(jnp.ones((c, c), dtype=bool))
    strict = jnp.tril(jnp.ones((c, c), dtype=bool), -1)
    diff = gcum[..., :, None] - gcum[..., None, :]
    decay = jnp.where(tri, jnp.exp(jnp.where(tri, diff, 0.0)), 0.0)
    kb = kc * bc[..., None]
    a = jnp.where(strict, jnp.einsum('nbhid,nbhjd->nbhij', kb, kc) * decay, 0.0)
    rhs = jnp.concatenate([vc * bc[..., None], kb * jnp.exp(gcum)[..., None]], axis=-1)
    sol = lax.linalg.triangular_solve(a + jnp.eye(c, dtype=F32), rhs, left_side=True, lower=True, unit_diagonal=True)
    u, w = sol[..., :dv], sol[..., dv:]
    qk = jnp.einsum('nbhid,nbhjd->nbhij', qc, kc) * decay
    q_dec = qc * jnp.exp(gcum)[..., None]
    k_dec = kc * jnp.exp(gcum[..., -1:] - gcum)[..., None]
    g_last = jnp.exp(gcum[..., -1])

    def step(s, xs):
        u_i, w_i, qk_i, qd_i, kd_i, gl_i = xs
        v_new = u_i - jnp.einsum('bhcd,bhde->bhce', w_i, s)
        o_i = jnp.einsum('bhcd,bhde->bhce', qd_i, s) + jnp.einsum('bhij,bhje->bhie', qk_i, v_new)
        s = s * gl_i[..., None, None] + jnp.einsum('bhcd,bhce->bhde', kd_i, v_new)
        return s, o_i

    s_fin, o = lax.scan(step, s0, (u, w, qk, q_dec, k_dec, g_last))
    o = jnp.swapaxes(jnp.moveaxis(o, 0, 1), 2, 3).reshape(b, n, h, dv)
    return s_fin, o


def _gdn_prepare(qkv, b_logit, a_logit, conv_w, a_log, dt_bias):
    b, n = qkv.shape[:2]
    hq = jax.nn.silu(short_conv(qkv, conv_w)).astype(F32)
    q, k, v = jnp.split(hq, (GDN_HEADS * GDN_DK, 2 * GDN_HEADS * GDN_DK), axis=-1)
    q = l2norm(q.reshape(b, n, GDN_HEADS, GDN_DK)) * GDN_DK ** -0.5
    k = l2norm(k.reshape(b, n, GDN_HEADS, GDN_DK))
    v = v.reshape(b, n, GDN_HEADS, GDN_DV)
    beta = jax.nn.sigmoid(b_logit.astype(F32).reshape(b, n, 2, GDN_HEADS))
    g = -jnp.exp(a_log.astype(F32)) * jax.nn.softplus(a_logit.astype(F32).reshape(b, n, 2, GDN_HEADS) + dt_bias.astype(F32))
    return q, k, v, beta, g


def gdn_mixer(p_qkv, p_z, p_b, p_a, c_qkv, c_z, c_b, c_a, conv_w, a_log, dt_bias, norm_w, ctx_out):
    q, k, v, beta, g = _gdn_prepare(p_qkv, p_b, p_a, conv_w, a_log, dt_bias)
    qc, kc, vc, bc, gc = _gdn_prepare(c_qkv, c_b, c_a, conv_w, a_log, dt_bias)
    s_zero = jnp.zeros((q.shape[0], GDN_HEADS, GDN_DK, GDN_DV), F32)
    outs, outs_c = [], []
    for dirn in range(2):
        fl = (lambda t: jnp.flip(t, axis=1)) if dirn == 1 else (lambda t: t)
        s_ctx, o_ctx = gated_delta_chunked(fl(qc), fl(kc), fl(vc), fl(bc[:, :, dirn]), fl(gc[:, :, dirn]), s_zero)
        _, o_lat = gated_delta_chunked(fl(q), fl(k), fl(v), fl(beta[:, :, dirn]), fl(g[:, :, dirn]), s_ctx)
        outs.append(fl(o_lat))
        outs_c.append(fl(o_ctx))

    def finish(o, z):
        b, n = o.shape[:2]
        zz = jax.nn.silu(z.astype(F32).reshape(b, n, GDN_HEADS, GDN_DV))
        return (rmsnorm(o, norm_w) * zz).reshape(b, n, GDN_WIDTH).astype(p_qkv.dtype)

    out = finish(outs[0] + outs[1], p_z)
    out_c = finish(outs_c[0] + outs_c[1], c_z) if ctx_out else None
    return out, out_c


def moe_ffn(h, w_router, router_bias, w1, w3, w2, ws1, ws3, ws2):
    t = h.shape[0]
    scores = jax.nn.sigmoid((h @ w_router).astype(F32))
    biased = scores + router_bias.astype(F32)
    grp_score = lax.top_k(biased.reshape(t, N_GROUPS, N_EXPERTS // N_GROUPS), 2)[0].sum(-1)
    _, top_groups = lax.top_k(grp_score, TOPK_GROUPS)
    gmask = jnp.any(top_groups[:, :, None] == jnp.arange(N_GROUPS)[None, None, :], axis=1)
    emask = jnp.repeat(gmask, N_EXPERTS // N_GROUPS, axis=-1)
    _, top_idx = lax.top_k(jnp.where(emask, biased, -jnp.inf), TOP_K)
    top_w = jnp.take_along_axis(scores, top_idx, axis=-1)
    top_w = top_w / jnp.sum(top_w, axis=-1, keepdims=True) * ROUTED_SCALE
    gates = jnp.zeros((t, N_EXPERTS), F32).at[jnp.arange(t)[:, None], top_idx].set(top_w)
    shared = (jax.nn.silu(h @ ws1) * (h @ ws3)) @ ws2

    def add_expert(acc, p):
        e1, e3, e2, gate = p
        hid = jax.nn.silu(h @ e1) * (h @ e3) * gate[:, None]
        return acc + hid @ e2, None

    y, _ = lax.scan(add_expert, shared, (w1, w3, w2, gates.T.astype(h.dtype)))
    return y


def setup_inputs(seed: int = 0) -> dict:
    key = jax.random.key(seed)
    keys = iter(jax.random.split(key, 48))

    def nrm(shape, std):
        return jax.random.normal(next(keys), shape, F32) * std

    def unif(shape, lo, hi):
        return jax.random.uniform(next(keys), shape, F32, lo, hi)

    d, e, f = D_MODEL, N_EXPERTS, MOE_FFN
    g, p, hg = S5_GROUPS, S5_STATE, S5_GROUP
    x = nrm((BATCH, SEQ, d), 1.0)
    c = nrm((BATCH, d), 1.0)
    ctx = nrm((BATCH, CTX_LEN, d), 1.0)
    c_ctx = nrm((d,), 1.0)
    norm1_w = 1.0 + nrm((DEPTH, d), 0.02)
    norm2_w = 1.0 + nrm((DEPTH, d), 0.02)
    w_mod = nrm((DEPTH, d, 6 * d), 0.5 * d ** -0.5)
    b_mod = nrm((DEPTH, 6 * d), 0.02)
    w_in = nrm((DEPTH, d, IN_WIDTH), d ** -0.5)
    w_out = nrm((DEPTH, D_MIX, d), D_MIX ** -0.5)
    da_lambda = nrm((DEPTH, 4, DA_HEAD_DIM), 0.1)
    da_subln_w = 1.0 + nrm((DEPTH, DA_V_DIM), 0.02)
    s5_lam_re = -0.5 + nrm((DEPTH, 2, g, p), 0.02)
    s5_lam_im = math.pi * jnp.arange(p, dtype=F32) + nrm((DEPTH, 2, g, p), 0.02)
    s5_log_step = unif((DEPTH, 2, g), math.log(S5_DT_MIN), math.log(S5_DT_MAX))
    s5_b_re = nrm((DEPTH, 2, g, p, hg), (2 * hg) ** -0.5)
    s5_b_im = nrm((DEPTH, 2, g, p, hg), (2 * hg) ** -0.5)
    s5_c_re = nrm((DEPTH, 2, g, hg, p), 0.5)
    s5_c_im = nrm((DEPTH, 2, g, hg, p), 0.5)
    s5_d = nrm((DEPTH, g, hg), 1.0)
    s5_w_glu = nrm((DEPTH, S5_WIDTH, S5_WIDTH), S5_WIDTH ** -0.5)
    s5_b_glu = nrm((DEPTH, S5_WIDTH), 0.02)
    gdn_conv_w = nrm((DEPTH, GDN_CONV, GDN_QKV_W), GDN_CONV ** -0.5)
    gdn_a_log = jnp.log(unif((DEPTH, 2, GDN_HEADS), 1.0, 16.0))
    dt = jnp.exp(unif((DEPTH, 2, GDN_HEADS), math.log(0.001), math.log(0.1)))
    gdn_dt_bias = dt + jnp.log(-jnp.expm1(-dt))
    gdn_norm_w = 1.0 + nrm((DEPTH, GDN_DV), 0.02)
    moe_w_router = nrm((DEPTH, d, e), d ** -0.5)
    moe_router_bias = nrm((DEPTH, e), 0.01)
    moe_w1 = nrm((DEPTH, e, d, f), d ** -0.5)
    moe_w3 = nrm((DEPTH, e, d, f), d ** -0.5)
    moe_w2 = nrm((DEPTH, e, f, d), f ** -0.5)
    moe_ws1 = nrm((DEPTH, d, f), d ** -0.5)
    moe_ws3 = nrm((DEPTH, d, f), d ** -0.5)
    moe_ws2 = nrm((DEPTH, f, d), f ** -0.5)
    final_norm_w = 1.0 + nrm((d,), 0.02)
    return {'x': x, 'c': c, 'ctx': ctx, 'c_ctx': c_ctx,
            'norm1_w': norm1_w, 'norm2_w': norm2_w, 'w_mod': w_mod, 'b_mod': b_mod,
            'w_in': w_in, 'w_out': w_out, 'da_lambda': da_lambda, 'da_subln_w': da_subln_w,
            's5_lam_re': s5_lam_re, 's5_lam_im': s5_lam_im, 's5_log_step': s5_log_step,
            's5_b_re': s5_b_re, 's5_b_im': s5_b_im, 's5_c_re': s5_c_re, 's5_c_im': s5_c_im,
            's5_d': s5_d, 's5_w_glu': s5_w_glu, 's5_b_glu': s5_b_glu,
            'gdn_conv_w': gdn_conv_w, 'gdn_a_log': gdn_a_log, 'gdn_dt_bias': gdn_dt_bias, 'gdn_norm_w': gdn_norm_w,
            'moe_w_router': moe_w_router, 'moe_router_bias': moe_router_bias,
            'moe_w1': moe_w1, 'moe_w3': moe_w3, 'moe_w2': moe_w2,
            'moe_ws1': moe_ws1, 'moe_ws3': moe_ws3, 'moe_ws2': moe_ws2,
            'final_norm_w': final_norm_w}


def reference(x, c, ctx, c_ctx, norm1_w, norm2_w, w_mod, b_mod, w_in, w_out, da_lambda, da_subln_w,
              s5_lam_re, s5_lam_im, s5_log_step, s5_b_re, s5_b_im, s5_c_re, s5_c_im, s5_d, s5_w_glu, s5_b_glu,
              gdn_conv_w, gdn_a_log, gdn_dt_bias, gdn_norm_w,
              moe_w_router, moe_router_bias, moe_w1, moe_w3, moe_w2, moe_ws1, moe_ws3, moe_ws2,
              final_norm_w):
    b, n, d = x.shape
    nc = ctx.shape[1]
    rows = n // GRID_W
    row = jnp.repeat(jnp.arange(rows, dtype=jnp.int32), GRID_W)
    col = jnp.tile(jnp.arange(GRID_W, dtype=jnp.int32), rows)
    rope = axial_rope_tables(row, col, DA_HEAD_DIM, x.dtype)
    c_act = jax.nn.silu(c)
    cc_act = jax.nn.silu(c_ctx)
    xc = ctx
    for i in range(DEPTH):
        ctx_out = i < DEPTH - 1
        lam_init = 0.8 - 0.6 * math.exp(-0.3 * i)
        mod = (c_act @ w_mod[i] + b_mod[i]).reshape(b, 1, 6, d)
        modc = (cc_act @ w_mod[i] + b_mod[i]).reshape(6, d)
        sh1, sc1, g1, sh2, sc2, g2 = (mod[:, :, j] for j in range(6))
        csh1, csc1, cg1, csh2, csc2, cg2 = (modc[j] for j in range(6))

        h = modulate(x, norm1_w[i], sh1, sc1)
        hc = modulate(xc, norm1_w[i], csh1, csc1)
        pq, pk, pv, pu, pqkv, pz, pb, pa = jnp.split(h @ w_in[i], IN_OFFSETS, axis=-1)
        cq, ck, cv, cu, cqkv, cz, cb, ca = jnp.split(hc @ w_in[i], IN_OFFSETS, axis=-1)
        da_o, da_oc = diff_attention(pq, pk, pv, cq, ck, cv, rope, da_lambda[i], da_subln_w[i], lam_init, ctx_out)
        s5_o, s5_oc = s5_mixer(pu, cu, s5_lam_re[i], s5_lam_im[i], s5_log_step[i], s5_b_re[i], s5_b_im[i],
                               s5_c_re[i], s5_c_im[i], s5_d[i], s5_w_glu[i], s5_b_glu[i], ctx_out)
        gdn_o, gdn_oc = gdn_mixer(pqkv, pz, pb, pa, cqkv, cz, cb, ca, gdn_conv_w[i], gdn_a_log[i],
                                  gdn_dt_bias[i], gdn_norm_w[i], ctx_out)
        x = x + g1 * (jnp.concatenate([da_o, s5_o, gdn_o], axis=-1) @ w_out[i])
        if ctx_out:
            xc = xc + cg1 * (jnp.concatenate([da_oc, s5_oc, gdn_oc], axis=-1) @ w_out[i])

        h2 = modulate(x, norm2_w[i], sh2, sc2).reshape(b * n, d)
        moe_args = (moe_w_router[i], moe_router_bias[i], moe_w1[i], moe_w3[i], moe_w2[i],
                    moe_ws1[i], moe_ws3[i], moe_ws2[i])
        if ctx_out:
            h2c = modulate(xc, norm2_w[i], csh2, csc2).reshape(b * nc, d)
            f_all = moe_ffn(jnp.concatenate([h2, h2c], axis=0), *moe_args)
            xc = xc + cg2 * f_all[b * n:].reshape(b, nc, d)
            f_lat = f_all[:b * n]
        else:
            f_lat = moe_ffn(h2, *moe_args)
        x = x + g2 * f_lat.reshape(b, n, d)
    return rmsnorm(x, final_norm_w)
```

```python
import functools
import math

import jax
import jax.numpy as jnp
import numpy as np
from jax import lax
from jax.experimental import pallas as pl
from jax.experimental.pallas import tpu as pltpu

F32 = jnp.float32
BF16 = jnp.bfloat16

D_MODEL = 1024
DEPTH = 2
GRID_W = 64
EPS = 1e-6

DA_HEADS = 4
DA_HEAD_DIM = D_MODEL // 16
DA_V_DIM = 2 * DA_HEAD_DIM
DA_WIDTH = DA_HEADS * DA_V_DIM
ROPE_THETA = 10000.0
Q_BLOCK = 128

S5_WIDTH = D_MODEL // 4
S5_GROUP = 16
S5_GROUPS = S5_WIDTH // S5_GROUP
S5_STATE = 64

GDN_HEADS = 4
GDN_DK = D_MODEL // 16
GDN_DV = D_MODEL // 16
GDN_WIDTH = GDN_HEADS * GDN_DV
GDN_CONV = 5
GDN_CHUNK = 64

D_MIX = DA_WIDTH + S5_WIDTH + GDN_WIDTH
DA_QK_W = 2 * DA_HEADS * DA_HEAD_DIM
GDN_QKV_W = 2 * GDN_HEADS * GDN_DK + GDN_HEADS * GDN_DV
IN_SIZES = (DA_QK_W, DA_QK_W, DA_WIDTH, S5_WIDTH, GDN_QKV_W, GDN_WIDTH, 2 * GDN_HEADS, 2 * GDN_HEADS)
IN_WIDTH = sum(IN_SIZES)
IN_OFFSETS = tuple(int(o) for o in np.cumsum(IN_SIZES)[:-1])

N_EXPERTS = 64
TOP_K = 8
N_GROUPS = 8
GROUP_SIZE = N_EXPERTS // N_GROUPS
TOPK_GROUPS = 4
MOE_FFN = D_MODEL // 4
ROUTED_SCALE = 2.5

LANES = 128
VMEM_LIMIT_BYTES = 56 * 1024 * 1024


MOE_TM = 1024


def _pick_lowest(cur, idx, sentinel, axis):
    m = jnp.max(cur, axis=axis, keepdims=True)
    first = jnp.min(jnp.where(cur == m, idx, sentinel), axis=axis, keepdims=True)
    return idx == first


def _route(logits_t, bias):
    tm = logits_t.shape[1]
    neg = jnp.float32(-jnp.inf)
    scores = jax.nn.sigmoid(logits_t)
    biased = scores + bias
    b3 = biased.reshape(N_GROUPS, GROUP_SIZE, tm)
    eidx = lax.broadcasted_iota(jnp.int32, b3.shape, 1)
    m1 = jnp.max(b3, axis=1, keepdims=True)
    p1 = _pick_lowest(b3, eidx, GROUP_SIZE, 1)
    m2 = jnp.max(jnp.where(p1, neg, b3), axis=1, keepdims=True)
    gs = (m1 + m2).reshape(N_GROUPS, tm)
    gidx = lax.broadcasted_iota(jnp.int32, gs.shape, 0)
    gsel = jnp.zeros(gs.shape, jnp.bool_)
    cur = gs
    for _ in range(TOPK_GROUPS):
        pick = _pick_lowest(cur, gidx, N_GROUPS, 0)
        gsel = jnp.logical_or(gsel, pick)
        cur = jnp.where(pick, neg, cur)
    emask = jnp.broadcast_to(gsel.reshape(N_GROUPS, 1, tm), b3.shape)
    cur = jnp.where(emask, b3, neg).reshape(N_EXPERTS, tm)
    ridx = lax.broadcasted_iota(jnp.int32, cur.shape, 0)
    sel = jnp.zeros(cur.shape, jnp.bool_)
    for _ in range(TOP_K):
        pick = _pick_lowest(cur, ridx, N_EXPERTS, 0)
        sel = jnp.logical_or(sel, pick)
        cur = jnp.where(pick, neg, cur)
    w = jnp.where(sel, scores, 0.0)
    return w / jnp.sum(w, axis=0, keepdims=True) * ROUTED_SCALE


def _moe_kernel(x_ref, mod_ref, nw_ref, wr_ref, rb_ref, w13_ref, w2_ref, o_ref, h_sc, gate_sc, acc_sc):
    e = pl.program_id(1)
    n_e = pl.num_programs(1)

    @pl.when(e == 0)
    def _():
        x = x_ref[...]
        y = x * lax.rsqrt(jnp.mean(x * x, axis=-1, keepdims=True) + EPS) * nw_ref[...]
        h = y * (1.0 + mod_ref[0, 4:5, :]) + mod_ref[0, 3:4, :]
        h_sc[...] = h.astype(BF16)
        logits_t = lax.dot_general(wr_ref[...], h, (((1,), (1,)), ((), ())),
                                   precision=lax.Precision.HIGHEST, preferred_element_type=F32)
        w = _route(logits_t, rb_ref[...])
        tm = w.shape[1]
        row = lax.broadcasted_iota(jnp.int32, (LANES - N_EXPERTS, tm), 0)
        shared = jnp.where(row == 0, 1.0, 0.0).astype(F32)
        gate_sc[...] = jnp.concatenate([w, shared], axis=0).T
        acc_sc[...] = jnp.zeros_like(acc_sc)

    h = h_sc[...]
    ab = jnp.dot(h, w13_ref[0], preferred_element_type=F32)
    a, b = ab[:, :MOE_FFN], ab[:, MOE_FFN:]
    lane = lax.broadcasted_iota(jnp.int32, (1, LANES), 1)
    g = jnp.sum(jnp.where(lane == e, gate_sc[...], 0.0), axis=-1, keepdims=True)
    hid = a * (0.5 * (1.0 + jnp.tanh(0.5 * a))) * b * g
    acc_sc[...] += jnp.dot(hid.astype(BF16), w2_ref[0], preferred_element_type=F32)

    @pl.when(e == n_e - 1)
    def _():
        o_ref[...] = x_ref[...] + mod_ref[0, 5:6, :] * acc_sc[...]


def moe_sublayer(x, mod, rows_per_mod, norm_w, w_router_t, router_bias, w13, w2):
    t, d = x.shape
    tm = MOE_TM
    assert t % tm == 0 and rows_per_mod % tm == 0
    n_slots = w13.shape[0]
    tiles_per_mod = rows_per_mod // tm
    return pl.pallas_call(
        _moe_kernel,
        grid=(t // tm, n_slots),
        in_specs=[
            pl.BlockSpec((tm, d), lambda i, e: (i, 0)),
            pl.BlockSpec((1, 6, d), lambda i, e: (i // tiles_per_mod, 0, 0)),
            pl.BlockSpec((1, d), lambda i, e: (0, 0)),
            pl.BlockSpec((N_EXPERTS, d), lambda i, e: (0, 0)),
            pl.BlockSpec((N_EXPERTS, 1), lambda i, e: (0, 0)),
            pl.BlockSpec((1, d, 2 * MOE_FFN), lambda i, e: (e, 0, 0)),
            pl.BlockSpec((1, MOE_FFN, d), lambda i, e: (e, 0, 0)),
        ],
        out_specs=pl.BlockSpec((tm, d), lambda i, e: (i, 0)),
        out_shape=jax.ShapeDtypeStruct((t, d), F32),
        scratch_shapes=[
            pltpu.VMEM((tm, d), BF16),
            pltpu.VMEM((tm, LANES), F32),
            pltpu.VMEM((tm, d), F32),
        ],
        compiler_params=pltpu.CompilerParams(
            dimension_semantics=("parallel", "arbitrary"),
            vmem_limit_bytes=VMEM_LIMIT_BYTES),
        name="moe_sublayer",
    )(x, mod, norm_w.reshape(1, d), w_router_t, router_bias.reshape(N_EXPERTS, 1), w13, w2)


def _moe_weights(w1, w3, w2, ws1, ws3, ws2):
    w13 = jnp.concatenate([jnp.concatenate([w1, w3], axis=-1),
                           jnp.concatenate([ws1, ws3], axis=-1)[None]], axis=0).astype(BF16)
    w2a = jnp.concatenate([w2, ws2[None]], axis=0).astype(BF16)
    return w13, w2a


def rmsnorm(x, w):
    xf = x.astype(F32)
    y = xf * lax.rsqrt(jnp.mean(xf * xf, axis=-1, keepdims=True) + EPS)
    return (y * w.astype(F32)).astype(x.dtype)


def modulate(x, w, shift, scale):
    return rmsnorm(x, w) * (1 + scale) + shift


def l2norm(x):
    return x * lax.rsqrt(jnp.sum(x * x, axis=-1, keepdims=True) + EPS)


def axial_rope_tables(row, col, dim, dtype):
    nf = dim // 4
    inv = ROPE_THETA ** (-jnp.arange(nf, dtype=F32) / nf)
    ang_r = row.astype(F32)[:, None] * inv
    ang_c = col.astype(F32)[:, None] * inv
    shp = (row.shape[0], 1, 1, nf)
    return tuple(t.reshape(shp).astype(dtype) for t in (jnp.cos(ang_r), jnp.sin(ang_r), jnp.cos(ang_c), jnp.sin(ang_c)))


def _rotate_half(x, cos, sin):
    x1, x2 = jnp.split(x, 2, axis=-1)
    return jnp.concatenate([x1 * cos - x2 * sin, x1 * sin + x2 * cos], axis=-1)


def apply_axial_rope(x, rope):
    cr, sr, cc, sc = rope
    x_row, x_col = jnp.split(x, 2, axis=-1)
    return jnp.concatenate([_rotate_half(x_row, cr, sr), _rotate_half(x_col, cc, sc)], axis=-1)


def diff_attention(pq, pk, pv, cq, ck, cv, rope, lam_vecs, subln_w, lam_init, ctx_out):
    b, n = pq.shape[:2]
    nc = cq.shape[1]
    h, dh = DA_HEADS, DA_HEAD_DIM
    scale = dh ** -0.5
    q = apply_axial_rope(pq.reshape(b, n, h, 2, dh), rope) * scale
    k = apply_axial_rope(pk.reshape(b, n, h, 2, dh), rope)
    v = pv.reshape(b, n, h, DA_V_DIM)
    qc = cq.reshape(b, nc, h, 2, dh) * scale
    kc = ck.reshape(b, nc, h, 2, dh)
    vc = cv.reshape(b, nc, h, DA_V_DIM)
    lq1, lk1, lq2, lk2 = lam_vecs.astype(F32)
    lam = jnp.exp(jnp.sum(lq1 * lk1)) - jnp.exp(jnp.sum(lq2 * lk2)) + lam_init

    def attend(qb, kk, vv):
        s = jnp.einsum('bqhmd,bkhmd->bhmqk', qb, kk).astype(F32)
        p = jax.nn.softmax(s, axis=-1)
        w = (p[:, :, 0] - lam * p[:, :, 1]).astype(vv.dtype)
        return jnp.einsum('bhqk,bkhe->bqhe', w, vv)

    k_all = jnp.concatenate([kc, k], axis=1)
    v_all = jnp.concatenate([vc, v], axis=1)
    nb = n // Q_BLOCK
    q_blocks = jnp.moveaxis(q.reshape(b, nb, Q_BLOCK, h, 2, dh), 1, 0)
    o = lax.map(lambda qb: attend(qb, k_all, v_all), q_blocks)
    o = jnp.moveaxis(o, 0, 1).reshape(b, n, h, DA_V_DIM)
    out = (rmsnorm(o, subln_w) * (1 - lam_init)).reshape(b, n, DA_WIDTH)
    out_c = None
    if ctx_out:
        oc = attend(qc, kc, vc)
        out_c = (rmsnorm(oc, subln_w) * (1 - lam_init)).reshape(b, nc, DA_WIDTH)
    return out, out_c


def s5_discretize(lam_re, lam_im, log_step, b_re, b_im):
    lr, li = lam_re.astype(F32), lam_im.astype(F32)
    step = jnp.exp(log_step.astype(F32))[:, None]
    mag = jnp.exp(lr * step)
    ab_re, ab_im = mag * jnp.cos(li * step), mag * jnp.sin(li * step)
    den = lr * lr + li * li
    nr, ni = ab_re - 1.0, ab_im
    f_re = (nr * lr + ni * li) / den
    f_im = (ni * lr - nr * li) / den
    br, bi = b_re.astype(F32), b_im.astype(F32)
    bb_re = f_re[..., None] * br - f_im[..., None] * bi
    bb_im = f_re[..., None] * bi + f_im[..., None] * br
    return ab_re, ab_im, bb_re, bb_im


def _complex_affine_combine(e1, e2):
    a1r, a1i, b1r, b1i = e1
    a2r, a2i, b2r, b2i = e2
    return (a2r * a1r - a2i * a1i, a2r * a1i + a2i * a1r,
            a2r * b1r - a2i * b1i + b2r, a2r * b1i + a2i * b1r + b2i)


def s5_scan(u, ab_re, ab_im, bb_re, bb_im, s0, reverse):
    bu_re = jnp.einsum('gph,blgh->blgp', bb_re, u)
    bu_im = jnp.einsum('gph,blgh->blgp', bb_im, u)
    if s0 is not None:
        s_re, s_im = s0
        first = -1 if reverse else 0
        bu_re = bu_re.at[:, first].add(ab_re * s_re - ab_im * s_im)
        bu_im = bu_im.at[:, first].add(ab_re * s_im + ab_im * s_re)
    a_re = jnp.broadcast_to(ab_re, bu_re.shape)
    a_im = jnp.broadcast_to(ab_im, bu_im.shape)
    _, _, x_re, x_im = lax.associative_scan(_complex_affine_combine, (a_re, a_im, bu_re, bu_im), reverse=reverse, axis=1)
    return x_re, x_im


def s5_readout(c_re, c_im, x_re, x_im):
    return (jnp.einsum('ghp,blgp->blgh', c_re.astype(F32), x_re)
            - jnp.einsum('ghp,blgp->blgh', c_im.astype(F32), x_im))


def s5_mixer(u, uc, lam_re, lam_im, log_step, b_re, b_im, c_re, c_im, d_skip, w_glu, b_glu, ctx_out):
    b, n, _ = u.shape
    nc = uc.shape[1]
    uf = u.astype(F32).reshape(b, n, S5_GROUPS, S5_GROUP)
    ucf = uc.astype(F32).reshape(b, nc, S5_GROUPS, S5_GROUP)
    dsk = d_skip.astype(F32)
    y = uf * dsk
    yc = ucf * dsk if ctx_out else None
    for dirn in range(2):
        rev = dirn == 1
        disc = s5_discretize(lam_re[dirn], lam_im[dirn], log_step[dirn], b_re[dirn], b_im[dirn])
        xc_re, xc_im = s5_scan(ucf, *disc, None, rev)
        last = 0 if rev else -1
        x_re, x_im = s5_scan(uf, *disc, (xc_re[:, last], xc_im[:, last]), rev)
        y = y + s5_readout(c_re[dirn], c_im[dirn], x_re, x_im)
        if ctx_out:
            yc = yc + s5_readout(c_re[dirn], c_im[dirn], xc_re, xc_im)

    def glu(yy, m):
        z = jax.nn.gelu(yy.reshape(b, m, S5_WIDTH)).astype(u.dtype)
        return z * jax.nn.sigmoid(z @ w_glu + b_glu)

    return glu(y, n), (glu(yc, nc) if ctx_out else None)


def short_conv(x, w):
    ch, kw = x.shape[-1], w.shape[0]
    return lax.conv_general_dilated(x, w.reshape(kw, 1, ch).astype(x.dtype), window_strides=(1,),
                                    padding=[(kw // 2, kw // 2)], dimension_numbers=('NWC', 'WIO', 'NWC'),
                                    feature_group_count=ch)


def _to_chunks(t, n_chunks):
    b, n, h = t.shape[:3]
    t = t.reshape((b, n_chunks, n // n_chunks, h) + t.shape[3:])
    return jnp.moveaxis(jnp.swapaxes(t, 2, 3), 1, 0)


def gated_delta_chunked(q, k, v, beta, g, s0):
    b, n, h, dk = q.shape
    dv = v.shape[-1]
    c = GDN_CHUNK
    nch = n // c
    qc, kc, vc = _to_chunks(q, nch), _to_chunks(k, nch), _to_chunks(v, nch)
    bc, gc = _to_chunks(beta, nch), _to_chunks(g, nch)
    gcum = jnp.cumsum(gc, axis=-1)
    tri = jnp.tril(jnp.ones((c, c), dtype=bool))
    strict = jnp.tril(jnp.ones((c, c), dtype=bool), -1)
    diff = gcum[..., :, None] - gcum[..., None, :]
    decay = jnp.where(tri, jnp.exp(jnp.where(tri, diff, 0.0)), 0.0)
    kb = kc * bc[..., None]
    a = jnp.where(strict, jnp.einsum('nbhid,nbhjd->nbhij', kb, kc) * decay, 0.0)
    rhs = jnp.concatenate([vc * bc[..., None], kb * jnp.exp(gcum)[..., None]], axis=-1)
    sol = lax.linalg.triangular_solve(a + jnp.eye(c, dtype=F32), rhs, left_side=True, lower=True, unit_diagonal=True)
    u, w = sol[..., :dv], sol[..., dv:]
    qk = jnp.einsum('nbhid,nbhjd->nbhij', qc, kc) * decay
    q_dec = qc * jnp.exp(gcum)[..., None]
    k_dec = kc * jnp.exp(gcum[..., -1:] - gcum)[..., None]
    g_last = jnp.exp(gcum[..., -1])

    def step(s, xs):
        u_i, w_i, qk_i, qd_i, kd_i, gl_i = xs
        v_new = u_i - jnp.einsum('bhcd,bhde->bhce', w_i, s)
        o_i = jnp.einsum('bhcd,bhde->bhce', qd_i, s) + jnp.einsum('bhij,bhje->bhie', qk_i, v_new)
        s = s * gl_i[..., None, None] + jnp.einsum('bhcd,bhce->bhde', kd_i, v_new)
        return s, o_i

    s_fin, o = lax.scan(step, s0, (u, w, qk, q_dec, k_dec, g_last))
    o = jnp.swapaxes(jnp.moveaxis(o, 0, 1), 2, 3).reshape(b, n, h, dv)
    return s_fin, o


def _gdn_prepare(qkv, b_logit, a_logit, conv_w, a_log, dt_bias):
    b, n = qkv.shape[:2]
    hq = jax.nn.silu(short_conv(qkv, conv_w)).astype(F32)
    q, k, v = jnp.split(hq, (GDN_HEADS * GDN_DK, 2 * GDN_HEADS * GDN_DK), axis=-1)
    q = l2norm(q.reshape(b, n, GDN_HEADS, GDN_DK)) * GDN_DK ** -0.5
    k = l2norm(k.reshape(b, n, GDN_HEADS, GDN_DK))
    v = v.reshape(b, n, GDN_HEADS, GDN_DV)
    beta = jax.nn.sigmoid(b_logit.astype(F32).reshape(b, n, 2, GDN_HEADS))
    g = -jnp.exp(a_log.astype(F32)) * jax.nn.softplus(a_logit.astype(F32).reshape(b, n, 2, GDN_HEADS) + dt_bias.astype(F32))
    return q, k, v, beta, g


def gdn_mixer(p_qkv, p_z, p_b, p_a, c_qkv, c_z, c_b, c_a, conv_w, a_log, dt_bias, norm_w, ctx_out):
    q, k, v, beta, g = _gdn_prepare(p_qkv, p_b, p_a, conv_w, a_log, dt_bias)
    qc, kc, vc, bc, gc = _gdn_prepare(c_qkv, c_b, c_a, conv_w, a_log, dt_bias)
    s_zero = jnp.zeros((q.shape[0], GDN_HEADS, GDN_DK, GDN_DV), F32)
    outs, outs_c = [], []
    for dirn in range(2):
        fl = (lambda t: jnp.flip(t, axis=1)) if dirn == 1 else (lambda t: t)
        s_ctx, o_ctx = gated_delta_chunked(fl(qc), fl(kc), fl(vc), fl(bc[:, :, dirn]), fl(gc[:, :, dirn]), s_zero)
        _, o_lat = gated_delta_chunked(fl(q), fl(k), fl(v), fl(beta[:, :, dirn]), fl(g[:, :, dirn]), s_ctx)
        outs.append(fl(o_lat))
        outs_c.append(fl(o_ctx))

    def finish(o, z):
        b, n = o.shape[:2]
        zz = jax.nn.silu(z.astype(F32).reshape(b, n, GDN_HEADS, GDN_DV))
        return (rmsnorm(o, norm_w) * zz).reshape(b, n, GDN_WIDTH).astype(p_qkv.dtype)

    out = finish(outs[0] + outs[1], p_z)
    out_c = finish(outs_c[0] + outs_c[1], c_z) if ctx_out else None
    return out, out_c


def kernel(x, c, ctx, c_ctx, norm1_w, norm2_w, w_mod, b_mod, w_in, w_out, da_lambda, da_subln_w,
           s5_lam_re, s5_lam_im, s5_log_step, s5_b_re, s5_b_im, s5_c_re, s5_c_im, s5_d, s5_w_glu, s5_b_glu,
           gdn_conv_w, gdn_a_log, gdn_dt_bias, gdn_norm_w,
           moe_w_router, moe_router_bias, moe_w1, moe_w3, moe_w2, moe_ws1, moe_ws3, moe_ws2,
           final_norm_w):
    b, n, d = x.shape
    nc = ctx.shape[1]
    rows = n // GRID_W
    row = jnp.repeat(jnp.arange(rows, dtype=jnp.int32), GRID_W)
    col = jnp.tile(jnp.arange(GRID_W, dtype=jnp.int32), rows)
    rope = axial_rope_tables(row, col, DA_HEAD_DIM, x.dtype)
    c_act = jax.nn.silu(c)
    cc_act = jax.nn.silu(c_ctx)
    xc = ctx
    for i in range(DEPTH):
        ctx_out = i < DEPTH - 1
        lam_init = 0.8 - 0.6 * math.exp(-0.3 * i)
        mod6 = (c_act @ w_mod[i] + b_mod[i]).reshape(b, 6, d)
        modc6 = (cc_act @ w_mod[i] + b_mod[i]).reshape(1, 6, d)
        mod = mod6.reshape(b, 1, 6, d)
        sh1, sc1, g1, sh2, sc2, g2 = (mod[:, :, j] for j in range(6))
        csh1, csc1, cg1, csh2, csc2, cg2 = (modc6[0, j] for j in range(6))

        h = modulate(x, norm1_w[i], sh1, sc1)
        hc = modulate(xc, norm1_w[i], csh1, csc1)
        pq, pk, pv, pu, pqkv, pz, pb, pa = jnp.split(h @ w_in[i], IN_OFFSETS, axis=-1)
        cq, ck, cv, cu, cqkv, cz, cb, ca = jnp.split(hc @ w_in[i], IN_OFFSETS, axis=-1)
        da_o, da_oc = diff_attention(pq, pk, pv, cq, ck, cv, rope, da_lambda[i], da_subln_w[i], lam_init, ctx_out)
        s5_o, s5_oc = s5_mixer(pu, cu, s5_lam_re[i], s5_lam_im[i], s5_log_step[i], s5_b_re[i], s5_b_im[i],
                               s5_c_re[i], s5_c_im[i], s5_d[i], s5_w_glu[i], s5_b_glu[i], ctx_out)
        gdn_o, gdn_oc = gdn_mixer(pqkv, pz, pb, pa, cqkv, cz, cb, ca, gdn_conv_w[i], gdn_a_log[i],
                                  gdn_dt_bias[i], gdn_norm_w[i], ctx_out)
        x = x + g1 * (jnp.concatenate([da_o, s5_o, gdn_o], axis=-1) @ w_out[i])
        if ctx_out:
            xc = xc + cg1 * (jnp.concatenate([da_oc, s5_oc, gdn_oc], axis=-1) @ w_out[i])

        w13, w2a = _moe_weights(moe_w1[i], moe_w3[i], moe_w2[i], moe_ws1[i], moe_ws3[i], moe_ws2[i])
        wr_t = moe_w_router[i].T
        x = moe_sublayer(x.reshape(b * n, d), mod6, n, norm2_w[i], wr_t, moe_router_bias[i], w13, w2a).reshape(b, n, d)
        if ctx_out:
            xc = moe_sublayer(xc.reshape(b * nc, d), modc6, b * nc, norm2_w[i], wr_t, moe_router_bias[i],
                              w13, w2a).reshape(b, nc, d)
    return rmsnorm(x, final_norm_w)
```

```python
import functools
import math

import jax
import jax.numpy as jnp
import numpy as np
from jax import lax
from jax.experimental import pallas as pl
from jax.experimental.pallas import tpu as pltpu

F32 = jnp.float32
BF16 = jnp.bfloat16

D_MODEL = 1024
DEPTH = 2
GRID_W = 64
EPS = 1e-6

DA_HEADS = 4
DA_HEAD_DIM = D_MODEL // 16
DA_V_DIM = 2 * DA_HEAD_DIM
DA_WIDTH = DA_HEADS * DA_V_DIM
ROPE_THETA = 10000.0
Q_BLOCK = 128

S5_WIDTH = D_MODEL // 4
S5_GROUP = 16
S5_GROUPS = S5_WIDTH // S5_GROUP
S5_STATE = 64

GDN_HEADS = 4
GDN_DK = D_MODEL // 16
GDN_DV = D_MODEL // 16
GDN_WIDTH = GDN_HEADS * GDN_DV
GDN_CONV = 5
GDN_CHUNK = 64

D_MIX = DA_WIDTH + S5_WIDTH + GDN_WIDTH
DA_QK_W = 2 * DA_HEADS * DA_HEAD_DIM
GDN_QKV_W = 2 * GDN_HEADS * GDN_DK + GDN_HEADS * GDN_DV
IN_SIZES = (DA_QK_W, DA_QK_W, DA_WIDTH, S5_WIDTH, GDN_QKV_W, GDN_WIDTH, 2 * GDN_HEADS, 2 * GDN_HEADS)
IN_WIDTH = sum(IN_SIZES)
IN_OFFSETS = tuple(int(o) for o in np.cumsum(IN_SIZES)[:-1])

N_EXPERTS = 64
TOP_K = 8
N_GROUPS = 8
GROUP_SIZE = N_EXPERTS // N_GROUPS
TOPK_GROUPS = 4
MOE_FFN = D_MODEL // 4
ROUTED_SCALE = 2.5

LANES = 128
VMEM_LIMIT_BYTES = 56 * 1024 * 1024


MOE_TM = 1024


def _pick_lowest(cur, idx, sentinel, axis):
    m = jnp.max(cur, axis=axis, keepdims=True)
    first = jnp.min(jnp.where(cur == m, idx, sentinel), axis=axis, keepdims=True)
    return idx == first


def _route(logits_t, bias):
    tm = logits_t.shape[1]
    neg = jnp.float32(-jnp.inf)
    scores = jax.nn.sigmoid(logits_t)
    biased = scores + bias
    b3 = biased.reshape(N_GROUPS, GROUP_SIZE, tm)
    eidx = lax.broadcasted_iota(jnp.int32, b3.shape, 1)
    m1 = jnp.max(b3, axis=1, keepdims=True)
    p1 = _pick_lowest(b3, eidx, GROUP_SIZE, 1)
    m2 = jnp.max(jnp.where(p1, neg, b3), axis=1, keepdims=True)
    gs = (m1 + m2).reshape(N_GROUPS, tm)
    gidx = lax.broadcasted_iota(jnp.int32, gs.shape, 0)
    gsel = jnp.zeros(gs.shape, jnp.bool_)
    cur = gs
    for _ in range(TOPK_GROUPS):
        pick = _pick_lowest(cur, gidx, N_GROUPS, 0)
        gsel = jnp.logical_or(gsel, pick)
        cur = jnp.where(pick, neg, cur)
    emask = jnp.broadcast_to(gsel.reshape(N_GROUPS, 1, tm), b3.shape)
    cur = jnp.where(emask, b3, neg).reshape(N_EXPERTS, tm)
    ridx = lax.broadcasted_iota(jnp.int32, cur.shape, 0)
    sel = jnp.zeros(cur.shape, jnp.bool_)
    for _ in range(TOP_K):
        pick = _pick_lowest(cur, ridx, N_EXPERTS, 0)
        sel = jnp.logical_or(sel, pick)
        cur = jnp.where(pick, neg, cur)
    w = jnp.where(sel, scores, 0.0)
    return w / jnp.sum(w, axis=0, keepdims=True) * ROUTED_SCALE


def _moe_kernel(x_ref, mod_ref, nw_ref, wr_ref, rb_ref, w13_ref, w2_ref, o_ref, h_sc, gate_sc, acc_sc):
    e = pl.program_id(1)
    n_e = pl.num_programs(1)

    @pl.when(e == 0)
    def _():
        x = x_ref[...]
        y = x * lax.rsqrt(jnp.mean(x * x, axis=-1, keepdims=True) + EPS) * nw_ref[...]
        h = y * (1.0 + mod_ref[0, 4:5, :]) + mod_ref[0, 3:4, :]
        h_sc[...] = h.astype(BF16)
        logits_t = lax.dot_general(wr_ref[...], h, (((1,), (1,)), ((), ())),
                                   precision=lax.Precision.HIGHEST, preferred_element_type=F32)
        w = _route(logits_t, rb_ref[...])
        tm = w.shape[1]
        row = lax.broadcasted_iota(jnp.int32, (LANES - N_EXPERTS, tm), 0)
        shared = jnp.where(row == 0, 1.0, 0.0).astype(F32)
        gate_sc[...] = jnp.concatenate([w, shared], axis=0).T
        acc_sc[...] = jnp.zeros_like(acc_sc)

    h = h_sc[...]
    ab = jnp.dot(h, w13_ref[0], preferred_element_type=F32)
    a, b = ab[:, :MOE_FFN], ab[:, MOE_FFN:]
    lane = lax.broadcasted_iota(jnp.int32, (1, LANES), 1)
    g = jnp.sum(jnp.where(lane == e, gate_sc[...], 0.0), axis=-1, keepdims=True)
    hid = a * (0.5 * (1.0 + jnp.tanh(0.5 * a))) * b * g
    acc_sc[...] += jnp.dot(hid.astype(BF16), w2_ref[0], preferred_element_type=F32)

    @pl.when(e == n_e - 1)
    def _():
        o_ref[...] = x_ref[...] + mod_ref[0, 5:6, :] * acc_sc[...]


def moe_sublayer(x, mod, rows_per_mod, norm_w, w_router_t, router_bias, w13, w2):
    t, d = x.shape
    tm = MOE_TM
    assert t % tm == 0 and rows_per_mod % tm == 0
    n_slots = w13.shape[0]
    tiles_per_mod = rows_per_mod // tm
    return pl.pallas_call(
        _moe_kernel,
        grid=(t // tm, n_slots),
        in_specs=[
            pl.BlockSpec((tm, d), lambda i, e: (i, 0)),
            pl.BlockSpec((1, 6, d), lambda i, e: (i // tiles_per_mod, 0, 0)),
            pl.BlockSpec((1, d), lambda i, e: (0, 0)),
            pl.BlockSpec((N_EXPERTS, d), lambda i, e: (0, 0)),
            pl.BlockSpec((N_EXPERTS, 1), lambda i, e: (0, 0)),
            pl.BlockSpec((1, d, 2 * MOE_FFN), lambda i, e: (e, 0, 0)),
            pl.BlockSpec((1, MOE_FFN, d), lambda i, e: (e, 0, 0)),
        ],
        out_specs=pl.BlockSpec((tm, d), lambda i, e: (i, 0)),
        out_shape=jax.ShapeDtypeStruct((t, d), F32),
        scratch_shapes=[
            pltpu.VMEM((tm, d), BF16),
            pltpu.VMEM((tm, LANES), F32),
            pltpu.VMEM((tm, d), F32),
        ],
        compiler_params=pltpu.CompilerParams(
            dimension_semantics=("parallel", "arbitrary"),
            vmem_limit_bytes=VMEM_LIMIT_BYTES),
        name="moe_sublayer",
    )(x, mod, norm_w.reshape(1, d), w_router_t, router_bias.reshape(N_EXPERTS, 1), w13, w2)


def _moe_weights(w1, w3, w2, ws1, ws3, ws2):
    w13 = jnp.concatenate([jnp.concatenate([w1, w3], axis=-1),
                           jnp.concatenate([ws1, ws3], axis=-1)[None]], axis=0).astype(BF16)
    w2a = jnp.concatenate([w2, ws2[None]], axis=0).astype(BF16)
    return w13, w2a


def rmsnorm(x, w):
    xf = x.astype(F32)
    y = xf * lax.rsqrt(jnp.mean(xf * xf, axis=-1, keepdims=True) + EPS)
    return (y * w.astype(F32)).astype(x.dtype)


def modulate(x, w, shift, scale):
    return rmsnorm(x, w) * (1 + scale) + shift


def l2norm(x):
    return x * lax.rsqrt(jnp.sum(x * x, axis=-1, keepdims=True) + EPS)


def axial_rope_tables(row, col, dim, dtype):
    nf = dim // 4
    inv = ROPE_THETA ** (-jnp.arange(nf, dtype=F32) / nf)
    ang_r = row.astype(F32)[:, None] * inv
    ang_c = col.astype(F32)[:, None] * inv
    shp = (row.shape[0], 1, 1, nf)
    return tuple(t.reshape(shp).astype(dtype) for t in (jnp.cos(ang_r), jnp.sin(ang_r), jnp.cos(ang_c), jnp.sin(ang_c)))


def _rotate_half(x, cos, sin):
    x1, x2 = jnp.split(x, 2, axis=-1)
    return jnp.concatenate([x1 * cos - x2 * sin, x1 * sin + x2 * cos], axis=-1)


def apply_axial_rope(x, rope):
    cr, sr, cc, sc = rope
    x_row, x_col = jnp.split(x, 2, axis=-1)
    return jnp.concatenate([_rotate_half(x_row, cr, sr), _rotate_half(x_col, cc, sc)], axis=-1)


def diff_attention(pq, pk, pv, cq, ck, cv, rope, lam_vecs, subln_w, lam_init, ctx_out):
    b, n = pq.shape[:2]
    nc = cq.shape[1]
    h, dh = DA_HEADS, DA_HEAD_DIM
    scale = dh ** -0.5
    q = apply_axial_rope(pq.reshape(b, n, h, 2, dh), rope) * scale
    k = apply_axial_rope(pk.reshape(b, n, h, 2, dh), rope)
    v = pv.reshape(b, n, h, DA_V_DIM)
    qc = cq.reshape(b, nc, h, 2, dh) * scale
    kc = ck.reshape(b, nc, h, 2, dh)
    vc = cv.reshape(b, nc, h, DA_V_DIM)
    lq1, lk1, lq2, lk2 = lam_vecs.astype(F32)
    lam = jnp.exp(jnp.sum(lq1 * lk1)) - jnp.exp(jnp.sum(lq2 * lk2)) + lam_init

    def attend(qb, kk, vv):
        s = jnp.einsum('bqhmd,bkhmd->bhmqk', qb, kk).astype(F32)
        p = jax.nn.softmax(s, axis=-1)
        w = (p[:, :, 0] - lam * p[:, :, 1]).astype(vv.dtype)
        return jnp.einsum('bhqk,bkhe->bqhe', w, vv)

    k_all = jnp.concatenate([kc, k], axis=1)
    v_all = jnp.concatenate([vc, v], axis=1)
    nb = n // Q_BLOCK
    q_blocks = jnp.moveaxis(q.reshape(b, nb, Q_BLOCK, h, 2, dh), 1, 0)
    o = lax.map(lambda qb: attend(qb, k_all, v_all), q_blocks)
    o = jnp.moveaxis(o, 0, 1).reshape(b, n, h, DA_V_DIM)
    out = (rmsnorm(o, subln_w) * (1 - lam_init)).reshape(b, n, DA_WIDTH)
    out_c = None
    if ctx_out:
        oc = attend(qc, kc, vc)
        out_c = (rmsnorm(oc, subln_w) * (1 - lam_init)).reshape(b, nc, DA_WIDTH)
    return out, out_c


def s5_discretize(lam_re, lam_im, log_step, b_re, b_im):
    lr, li = lam_re.astype(F32), lam_im.astype(F32)
    step = jnp.exp(log_step.astype(F32))[:, None]
    mag = jnp.exp(lr * step)
    ab_re, ab_im = mag * jnp.cos(li * step), mag * jnp.sin(li * step)
    den = lr * lr + li * li
    nr, ni = ab_re - 1.0, ab_im
    f_re = (nr * lr + ni * li) / den
    f_im = (ni * lr - nr * li) / den
    br, bi = b_re.astype(F32), b_im.astype(F32)
    bb_re = f_re[..., None] * br - f_im[..., None] * bi
    bb_im = f_re[..., None] * bi + f_im[..., None] * br
    return ab_re, ab_im, bb_re, bb_im


def _complex_affine_combine(e1, e2):
    a1r, a1i, b1r, b1i = e1
    a2r, a2i, b2r, b2i = e2
    return (a2r * a1r - a2i * a1i, a2r * a1i + a2i * a1r,
            a2r * b1r - a2i * b1i + b2r, a2r * b1i + a2i * b1r + b2i)


def s5_scan(u, ab_re, ab_im, bb_re, bb_im, s0, reverse):
    bu_re = jnp.einsum('gph,blgh->blgp', bb_re, u)
    bu_im = jnp.einsum('gph,blgh->blgp', bb_im, u)
    if s0 is not None:
        s_re, s_im = s0
        first = -1 if reverse else 0
        bu_re = bu_re.at[:, first].add(ab_re * s_re - ab_im * s_im)
        bu_im = bu_im.at[:, first].add(ab_re * s_im + ab_im * s_re)
    a_re = jnp.broadcast_to(ab_re, bu_re.shape)
    a_im = jnp.broadcast_to(ab_im, bu_im.shape)
    _, _, x_re, x_im = lax.associative_scan(_complex_affine_combine, (a_re, a_im, bu_re, bu_im), reverse=reverse, axis=1)
    return x_re, x_im


def s5_readout(c_re, c_im, x_re, x_im):
    return (jnp.einsum('ghp,blgp->blgh', c_re.astype(F32), x_re)
            - jnp.einsum('ghp,blgp->blgh', c_im.astype(F32), x_im))


S5_LC = 64
S5_CW = S5_LC * S5_GROUP
S5_SW = 2 * S5_STATE


def _s5_tables(lam_re, lam_im, log_step, b_re, b_im, c_re, c_im):
    lc, g, p, hg = S5_LC, S5_GROUPS, S5_STATE, S5_GROUP
    hp = lax.Precision.HIGHEST
    bm, cm, mm, k1, k2 = [], [], [], [], []
    for d in range(2):
        lr, li = lam_re[d].astype(F32), lam_im[d].astype(F32)
        step = jnp.exp(log_step[d].astype(F32))[:, None]
        _, _, bb_re, bb_im = s5_discretize(lam_re[d], lam_im[d], log_step[d], b_re[d], b_im[d])
        tau = jnp.arange(lc + 1, dtype=F32)[:, None, None]
        mag = jnp.exp(tau * (lr * step))
        ang = tau * (li * step)
        pr, pi = mag * jnp.cos(ang), mag * jnp.sin(ang)
        abr = pr[..., None] * bb_re - pi[..., None] * bb_im
        abi = pr[..., None] * bb_im + pi[..., None] * bb_re
        cr, ci = c_re[d].astype(F32), c_im[d].astype(F32)
        kern = (jnp.einsum('gkp,tgph->tgkh', cr, abr[:lc], precision=hp)
                - jnp.einsum('gkp,tgph->tgkh', ci, abi[:lc], precision=hp))
        rank = np.arange(lc) if d == 0 else lc - 1 - np.arange(lc)
        e_in = lc - 1 - rank
        bmat = jnp.concatenate([abr[e_in].transpose(1, 0, 3, 2), abi[e_in].transpose(1, 0, 3, 2)], axis=-1)
        bm.append(bmat.reshape(g, S5_CW, S5_SW))
        e_out = rank + 1
        pro, pio = pr[e_out][:, :, None, :], pi[e_out][:, :, None, :]
        car = cr[None] * pro - ci[None] * pio
        cai = cr[None] * pio + ci[None] * pro
        cmat = jnp.concatenate([car.transpose(1, 3, 0, 2), -cai.transpose(1, 3, 0, 2)], axis=1)
        cm.append(cmat.reshape(g, S5_SW, S5_CW))
        dif = rank[None, :] - rank[:, None]
        kg = kern[np.clip(dif, 0, lc - 1)]
        kg = jnp.where((dif >= 0)[:, :, None, None, None], kg, 0.0)
        mm.append(kg.transpose(2, 0, 4, 1, 3).reshape(g, S5_CW, S5_CW))
        alr, ali = pr[lc], pi[lc]
        k1.append(jnp.repeat(jnp.concatenate([alr, alr], axis=-1), 8, axis=0))
        k2.append(jnp.repeat(jnp.concatenate([-ali, ali], axis=-1), 8, axis=0))
    return (jnp.stack(bm).astype(BF16), jnp.stack(cm).astype(BF16), jnp.stack(mm).astype(BF16),
            jnp.stack(k1), jnp.stack(k2))


def _s5_local_kernel(u_ref, bm_ref, s_ref):
    s = jnp.dot(u_ref[0], bm_ref[0, 0], preferred_element_type=F32)
    s_ref[0] = s.reshape(s_ref.shape[1:])


def _s5_carry_kernel(s_ref, k1_ref, k2_ref, xin_ref, *, n_ctx, n_chunks):
    d = pl.program_id(0)
    k1, k2 = k1_ref[0], k2_ref[0]

    def body(k, x):
        fwd = k
        rev = jnp.where(k < n_ctx, n_ctx - 1 - k, n_chunks + n_ctx - 1 - k)
        c = jnp.where(d == 0, fwd, rev)
        xin_ref[0, c] = x
        return k1 * x + k2 * pltpu.roll(x, S5_STATE, 1) + s_ref[0, c]

    lax.fori_loop(0, n_chunks, body, jnp.zeros(k1.shape, F32))


def _s5_out_kernel(u_ref, m_ref, xin_ref, cm_ref, y_ref):
    u = u_ref[0]
    acc = jnp.dot(u, m_ref[0, 0], preferred_element_type=F32)
    acc += jnp.dot(u, m_ref[1, 0], preferred_element_type=F32)
    for d in range(2):
        xin = xin_ref[d].reshape(u.shape[0], S5_SW).astype(BF16)
        acc += jnp.dot(xin, cm_ref[d, 0], preferred_element_type=F32)
    y_ref[0] = acc


def s5_scan_pallas(seq, tables, n_ctx):
    bmat, cmat, mmat, k1, k2 = tables
    b, l, _ = seq.shape
    g, lc = S5_GROUPS, S5_LC
    assert b == 8 and l % lc == 0 and n_ctx % lc == 0
    nch = l // lc
    r = nch * b
    u = seq.reshape(b, nch, lc, g, S5_GROUP).transpose(3, 1, 0, 2, 4).reshape(g, r, S5_CW).astype(BF16)
    params = pltpu.CompilerParams(dimension_semantics=("arbitrary", "arbitrary"), vmem_limit_bytes=VMEM_LIMIT_BYTES)
    s = pl.pallas_call(
        _s5_local_kernel,
        grid=(2, g),
        in_specs=[pl.BlockSpec((1, r, S5_CW), lambda d, gi: (gi, 0, 0)),
                  pl.BlockSpec((1, 1, S5_CW, S5_SW), lambda d, gi: (d, gi, 0, 0))],
        out_specs=pl.BlockSpec((1, nch, b, S5_SW), lambda d, gi: (d, 0, gi, 0)),
        out_shape=jax.ShapeDtypeStruct((2, nch, g * b, S5_SW), F32),
        compiler_params=params, name="s5_local",
    )(u, bmat)
    xin = pl.pallas_call(
        functools.partial(_s5_carry_kernel, n_ctx=n_ctx // lc, n_chunks=nch),
        grid=(2,),
        in_specs=[pl.BlockSpec((1, nch, g * b, S5_SW), lambda d: (d, 0, 0, 0)),
                  pl.BlockSpec((1, g * b, S5_SW), lambda d: (d, 0, 0)),
                  pl.BlockSpec((1, g * b, S5_SW), lambda d: (d, 0, 0))],
        out_specs=pl.BlockSpec((1, nch, g * b, S5_SW), lambda d: (d, 0, 0, 0)),
        out_shape=jax.ShapeDtypeStruct((2, nch, g * b, S5_SW), F32),
        compiler_params=pltpu.CompilerParams(dimension_semantics=("arbitrary",), vmem_limit_bytes=VMEM_LIMIT_BYTES),
        name="s5_carry",
    )(s, k1, k2)
    y = pl.pallas_call(
        _s5_out_kernel,
        grid=(g,),
        in_specs=[pl.BlockSpec((1, r, S5_CW), lambda gi: (gi, 0, 0)),
                  pl.BlockSpec((2, 1, S5_CW, S5_CW), lambda gi: (0, gi, 0, 0)),
                  pl.BlockSpec((2, nch, b, S5_SW), lambda gi: (0, 0, gi, 0)),
                  pl.BlockSpec((2, 1, S5_SW, S5_CW), lambda gi: (0, gi, 0, 0))],
        out_specs=pl.BlockSpec((1, r, S5_CW), lambda gi: (gi, 0, 0)),
        out_shape=jax.ShapeDtypeStruct((g, r, S5_CW), F32),
        compiler_params=pltpu.CompilerParams(dimension_semantics=("arbitrary",), vmem_limit_bytes=VMEM_LIMIT_BYTES),
        name="s5_out",
    )(u, mmat, xin, cmat)
    return y.reshape(g, nch, b, lc, S5_GROUP).transpose(2, 1, 3, 0, 4).reshape(b, l, S5_WIDTH)


def s5_mixer(u, uc, lam_re, lam_im, log_step, b_re, b_im, c_re, c_im, d_skip, w_glu, b_glu, ctx_out):
    b, n, _ = u.shape
    nc = uc.shape[1]
    tables = _s5_tables(lam_re, lam_im, log_step, b_re, b_im, c_re, c_im)
    seq = jnp.concatenate([uc, u], axis=1).astype(F32)
    y_all = s5_scan_pallas(seq, tables, nc) + seq * d_skip.astype(F32).reshape(S5_WIDTH)

    def glu(yy):
        z = jax.nn.gelu(yy).astype(u.dtype)
        return z * jax.nn.sigmoid(z @ w_glu + b_glu)

    return glu(y_all[:, nc:]), (glu(y_all[:, :nc]) if ctx_out else None)


def short_conv(x, w):
    ch, kw = x.shape[-1], w.shape[0]
    return lax.conv_general_dilated(x, w.reshape(kw, 1, ch).astype(x.dtype), window_strides=(1,),
                                    padding=[(kw // 2, kw // 2)], dimension_numbers=('NWC', 'WIO', 'NWC'),
                                    feature_group_count=ch)


def _to_chunks(t, n_chunks):
    b, n, h = t.shape[:3]
    t = t.reshape((b, n_chunks, n // n_chunks, h) + t.shape[3:])
    return jnp.moveaxis(jnp.swapaxes(t, 2, 3), 1, 0)


def gated_delta_chunked(q, k, v, beta, g, s0):
    b, n, h, dk = q.shape
    dv = v.shape[-1]
    c = GDN_CHUNK
    nch = n // c
    qc, kc, vc = _to_chunks(q, nch), _to_chunks(k, nch), _to_chunks(v, nch)
    bc, gc = _to_chunks(beta, nch), _to_chunks(g, nch)
    gcum = jnp.cumsum(gc, axis=-1)
    tri = jnp.tril(jnp.ones((c, c), dtype=bool))
    strict = jnp.tril(jnp.ones((c, c), dtype=bool), -1)
    diff = gcum[..., :, None] - gcum[..., None, :]
    decay = jnp.where(tri, jnp.exp(jnp.where(tri, diff, 0.0)), 0.0)
    kb = kc * bc[..., None]
    a = jnp.where(strict, jnp.einsum('nbhid,nbhjd->nbhij', kb, kc) * decay, 0.0)
    rhs = jnp.concatenate([vc * bc[..., None], kb * jnp.exp(gcum)[..., None]], axis=-1)
    sol = lax.linalg.triangular_solve(a + jnp.eye(c, dtype=F32), rhs, left_side=True, lower=True, unit_diagonal=True)
    u, w = sol[..., :dv], sol[..., dv:]
    qk = jnp.einsum('nbhid,nbhjd->nbhij', qc, kc) * decay
    q_dec = qc * jnp.exp(gcum)[..., None]
    k_dec = kc * jnp.exp(gcum[..., -1:] - gcum)[..., None]
    g_last = jnp.exp(gcum[..., -1])

    def step(s, xs):
        u_i, w_i, qk_i, qd_i, kd_i, gl_i = xs
        v_new = u_i - jnp.einsum('bhcd,bhde->bhce', w_i, s)
        o_i = jnp.einsum('bhcd,bhde->bhce', qd_i, s) + jnp.einsum('bhij,bhje->bhie', qk_i, v_new)
        s = s * gl_i[..., None, None] + jnp.einsum('bhcd,bhce->bhde', kd_i, v_new)
        return s, o_i

    s_fin, o = lax.scan(step, s0, (u, w, qk, q_dec, k_dec, g_last))
    o = jnp.swapaxes(jnp.moveaxis(o, 0, 1), 2, 3).reshape(b, n, h, dv)
    return s_fin, o


def _gdn_prepare(qkv, b_logit, a_logit, conv_w, a_log, dt_bias):
    b, n = qkv.shape[:2]
    hq = jax.nn.silu(short_conv(qkv, conv_w)).astype(F32)
    q, k, v = jnp.split(hq, (GDN_HEADS * GDN_DK, 2 * GDN_HEADS * GDN_DK), axis=-1)
    q = l2norm(q.reshape(b, n, GDN_HEADS, GDN_DK)) * GDN_DK ** -0.5
    k = l2norm(k.reshape(b, n, GDN_HEADS, GDN_DK))
    v = v.reshape(b, n, GDN_HEADS, GDN_DV)
    beta = jax.nn.sigmoid(b_logit.astype(F32).reshape(b, n, 2, GDN_HEADS))
    g = -jnp.exp(a_log.astype(F32)) * jax.nn.softplus(a_logit.astype(F32).reshape(b, n, 2, GDN_HEADS) + dt_bias.astype(F32))
    return q, k, v, beta, g


def gdn_mixer(p_qkv, p_z, p_b, p_a, c_qkv, c_z, c_b, c_a, conv_w, a_log, dt_bias, norm_w, ctx_out):
    q, k, v, beta, g = _gdn_prepare(p_qkv, p_b, p_a, conv_w, a_log, dt_bias)
    qc, kc, vc, bc, gc = _gdn_prepare(c_qkv, c_b, c_a, conv_w, a_log, dt_bias)
    s_zero = jnp.zeros((q.shape[0], GDN_HEADS, GDN_DK, GDN_DV), F32)
    outs, outs_c = [], []
    for dirn in range(2):
        fl = (lambda t: jnp.flip(t, axis=1)) if dirn == 1 else (lambda t: t)
        s_ctx, o_ctx = gated_delta_chunked(fl(qc), fl(kc), fl(vc), fl(bc[:, :, dirn]), fl(gc[:, :, dirn]), s_zero)
        _, o_lat = gated_delta_chunked(fl(q), fl(k), fl(v), fl(beta[:, :, dirn]), fl(g[:, :, dirn]), s_ctx)
        outs.append(fl(o_lat))
        outs_c.append(fl(o_ctx))

    def finish(o, z):
        b, n = o.shape[:2]
        zz = jax.nn.silu(z.astype(F32).reshape(b, n, GDN_HEADS, GDN_DV))
        return (rmsnorm(o, norm_w) * zz).reshape(b, n, GDN_WIDTH).astype(p_qkv.dtype)

    out = finish(outs[0] + outs[1], p_z)
    out_c = finish(outs_c[0] + outs_c[1], c_z) if ctx_out else None
    return out, out_c


def kernel(x, c, ctx, c_ctx, norm1_w, norm2_w, w_mod, b_mod, w_in, w_out, da_lambda, da_subln_w,
           s5_lam_re, s5_lam_im, s5_log_step, s5_b_re, s5_b_im, s5_c_re, s5_c_im, s5_d, s5_w_glu, s5_b_glu,
           gdn_conv_w, gdn_a_log, gdn_dt_bias, gdn_norm_w,
           moe_w_router, moe_router_bias, moe_w1, moe_w3, moe_w2, moe_ws1, moe_ws3, moe_ws2,
           final_norm_w):
    b, n, d = x.shape
    nc = ctx.shape[1]
    rows = n // GRID_W
    row = jnp.repeat(jnp.arange(rows, dtype=jnp.int32), GRID_W)
    col = jnp.tile(jnp.arange(GRID_W, dtype=jnp.int32), rows)
    rope = axial_rope_tables(row, col, DA_HEAD_DIM, x.dtype)
    c_act = jax.nn.silu(c)
    cc_act = jax.nn.silu(c_ctx)
    xc = ctx
    for i in range(DEPTH):
        ctx_out = i < DEPTH - 1
        lam_init = 0.8 - 0.6 * math.exp(-0.3 * i)
        mod6 = (c_act @ w_mod[i] + b_mod[i]).reshape(b, 6, d)
        modc6 = (cc_act @ w_mod[i] + b_mod[i]).reshape(1, 6, d)
        mod = mod6.reshape(b, 1, 6, d)
        sh1, sc1, g1, sh2, sc2, g2 = (mod[:, :, j] for j in range(6))
        csh1, csc1, cg1, csh2, csc2, cg2 = (modc6[0, j] for j in range(6))

        h = modulate(x, norm1_w[i], sh1, sc1)
        hc = modulate(xc, norm1_w[i], csh1, csc1)
        pq, pk, pv, pu, pqkv, pz, pb, pa = jnp.split(h @ w_in[i], IN_OFFSETS, axis=-1)
        cq, ck, cv, cu, cqkv, cz, cb, ca = jnp.split(hc @ w_in[i], IN_OFFSETS, axis=-1)
        da_o, da_oc = diff_attention(pq, pk, pv, cq, ck, cv, rope, da_lambda[i], da_subln_w[i], lam_init, ctx_out)
        s5_o, s5_oc = s5_mixer(pu, cu, s5_lam_re[i], s5_lam_im[i], s5_log_step[i], s5_b_re[i], s5_b_im[i],
                               s5_c_re[i], s5_c_im[i], s5_d[i], s5_w_glu[i], s5_b_glu[i], ctx_out)
        gdn_o, gdn_oc = gdn_mixer(pqkv, pz, pb, pa, cqkv, cz, cb, ca, gdn_conv_w[i], gdn_a_log[i],
                                  gdn_dt_bias[i], gdn_norm_w[i], ctx_out)
        x = x + g1 * (jnp.concatenate([da_o, s5_o, gdn_o], axis=-1) @ w_out[i])
        if ctx_out:
            xc = xc + cg1 * (jnp.concatenate([da_oc, s5_oc, gdn_oc], axis=-1) @ w_out[i])

        w13, w2a = _moe_weights(moe_w1[i], moe_w3[i], moe_w2[i], moe_ws1[i], moe_ws3[i], moe_ws2[i])
        wr_t = moe_w_router[i].T
        x = moe_sublayer(x.reshape(b * n, d), mod6, n, norm2_w[i], wr_t, moe_router_bias[i], w13, w2a).reshape(b, n, d)
        if ctx_out:
            xc = moe_sublayer(xc.reshape(b * nc, d), modc6, b * nc, norm2_w[i], wr_t, moe_router_bias[i],
                              w13, w2a).reshape(b, nc, d)
    return rmsnorm(x, final_norm_w)
```

```python
import functools
import math

import jax
import jax.numpy as jnp
import numpy as np
from jax import lax
from jax.experimental import pallas as pl
from jax.experimental.pallas import tpu as pltpu

F32 = jnp.float32
BF16 = jnp.bfloat16

D_MODEL = 1024
DEPTH = 2
GRID_W = 64
EPS = 1e-6

DA_HEADS = 4
DA_HEAD_DIM = D_MODEL // 16
DA_V_DIM = 2 * DA_HEAD_DIM
DA_WIDTH = DA_HEADS * DA_V_DIM
ROPE_THETA = 10000.0

S5_WIDTH = D_MODEL // 4
S5_GROUP = 16
S5_GROUPS = S5_WIDTH // S5_GROUP
S5_STATE = 64

GDN_HEADS = 4
GDN_DK = D_MODEL // 16
GDN_DV = D_MODEL // 16
GDN_WIDTH = GDN_HEADS * GDN_DV
GDN_CONV = 5
GDN_CHUNK = 64

D_MIX = DA_WIDTH + S5_WIDTH + GDN_WIDTH
DA_QK_W = 2 * DA_HEADS * DA_HEAD_DIM
GDN_QKV_W = 2 * GDN_HEADS * GDN_DK + GDN_HEADS * GDN_DV
IN_SIZES = (DA_QK_W, DA_QK_W, DA_WIDTH, S5_WIDTH, GDN_QKV_W, GDN_WIDTH, 2 * GDN_HEADS, 2 * GDN_HEADS)
IN_WIDTH = sum(IN_SIZES)

N_EXPERTS = 64
TOP_K = 8
N_GROUPS = 8
GROUP_SIZE = N_EXPERTS // N_GROUPS
TOPK_GROUPS = 4
MOE_FFN = D_MODEL // 4
ROUTED_SCALE = 2.5

LANES = 128
VMEM_LIMIT_BYTES = 56 * 1024 * 1024

HP = lax.Precision.HIGHEST
NT_DIMS = (((1,), (1,)), ((), ()))


def _params(*sem):
    return pltpu.CompilerParams(dimension_semantics=sem, vmem_limit_bytes=VMEM_LIMIT_BYTES)


def _sigmoid(x):
    return 0.5 * (1.0 + jnp.tanh(0.5 * x))


def _rms_rows(x):
    return x * lax.rsqrt(jnp.mean(x * x, axis=-1, keepdims=True) + EPS)


IN_PAD = 2944
INPROJ_TM = 512
O_Q, O_K, O_V, O_U, O_G, O_Z, O_BA = 0, 512, 1024, 1536, 1792, 2560, 2816


def _rope_apply(x, cos, sin):
    lane = lax.broadcasted_iota(jnp.int32, x.shape, 1)
    up = pltpu.roll(x, LANES - 16, 1)
    dn = pltpu.roll(x, 16, 1)
    partner = jnp.where((lane & 31) < 16, up, dn)
    return x * cos + partner * sin


def _inproj_kernel(x_ref, mod_ref, nw_ref, w_ref, cos_ref, sin_ref,
                   q_ref, k_ref, v_ref, u_ref, g_ref, z_ref, ba_ref, *, rope):
    h = (_rms_rows(x_ref[...]) * nw_ref[...] * (1.0 + mod_ref[0, 1:2, :]) + mod_ref[0, 0:1, :]).astype(BF16)

    def proj(lo, hi):
        return jnp.dot(h, w_ref[:, lo:hi], preferred_element_type=F32)

    scale = DA_HEAD_DIM ** -0.5
    for hd in range(DA_HEADS):
        lo = hd * LANES
        qs = proj(O_Q + lo, O_Q + lo + LANES)
        ks = proj(O_K + lo, O_K + lo + LANES)
        if rope:
            qs = _rope_apply(qs, cos_ref[...], sin_ref[...])
            ks = _rope_apply(ks, cos_ref[...], sin_ref[...])
        q_ref[:, lo:lo + LANES] = (qs * scale).astype(BF16)
        k_ref[:, lo:lo + LANES] = ks.astype(BF16)
    v_ref[...] = proj(O_V, O_U).astype(BF16)
    u_ref[...] = proj(O_U, O_G)
    g_ref[...] = proj(O_G, O_Z)
    z_ref[...] = proj(O_Z, O_BA)
    ba_ref[...] = proj(O_BA, IN_PAD)


def inproj(x, mod, rows_per_mod, norm_w, w_pad, cos_t, sin_t, rope):
    t, d = x.shape
    tm = min(INPROJ_TM, t)
    tpm = rows_per_mod // tm
    npos = cos_t.shape[0] // tm
    row = lambda i: (i, 0)
    widths = (DA_QK_W, DA_QK_W, DA_WIDTH, S5_WIDTH, GDN_QKV_W, GDN_WIDTH, LANES)
    dtypes = (BF16, BF16, BF16, F32, F32, F32, F32)
    return pl.pallas_call(
        functools.partial(_inproj_kernel, rope=rope),
        grid=(t // tm,),
        in_specs=[pl.BlockSpec((tm, d), row),
                  pl.BlockSpec((1, 6, d), lambda i: (i // tpm, 0, 0)),
                  pl.BlockSpec((1, d), lambda i: (0, 0)),
                  pl.BlockSpec((d, IN_PAD), lambda i: (0, 0)),
                  pl.BlockSpec((tm, LANES), lambda i: (i % npos, 0)),
                  pl.BlockSpec((tm, LANES), lambda i: (i % npos, 0))],
        out_specs=[pl.BlockSpec((tm, w), row) for w in widths],
        out_shape=[jax.ShapeDtypeStruct((t, w), dt) for w, dt in zip(widths, dtypes)],
        compiler_params=_params("parallel"), name="inproj",
    )(x, mod, norm_w.reshape(1, d), w_pad, cos_t, sin_t)


def _rope_tables(n):
    nf = DA_HEAD_DIM // 4
    t = jnp.arange(n, dtype=jnp.int32)
    inv = ROPE_THETA ** (-jnp.arange(nf, dtype=F32) / nf)
    ang_r = (t // GRID_W).astype(F32)[:, None] * inv
    ang_c = (t % GRID_W).astype(F32)[:, None] * inv
    cos64 = jnp.concatenate([jnp.cos(ang_r), jnp.cos(ang_r), jnp.cos(ang_c), jnp.cos(ang_c)], axis=-1)
    sin64 = jnp.concatenate([-jnp.sin(ang_r), jnp.sin(ang_r), -jnp.sin(ang_c), jnp.sin(ang_c)], axis=-1)
    return jnp.tile(cos64, (1, 2)), jnp.tile(sin64, (1, 2))


ATTN_TQ = 256
ATTN_KC = 1024


def _attn_kernel(*refs, n_kv, chunks, lam_init):
    q_ref = refs[0]
    k_refs = refs[1:1 + n_kv]
    v_refs = refs[1 + n_kv:1 + 2 * n_kv]
    lam_ref, w_ref, o_ref = refs[1 + 2 * n_kv:]
    q = q_ref[...]
    tq = q.shape[0]
    lane = lax.broadcasted_iota(jnp.int32, q.shape, 1)
    zero = jnp.zeros_like(q)
    qq = jnp.concatenate([jnp.where(lane < DA_HEAD_DIM, q, zero), jnp.where(lane >= DA_HEAD_DIM, q, zero)], axis=0)
    m = jnp.full((2 * tq, 1), -jnp.inf, F32)
    l = jnp.zeros((2 * tq, 1), F32)
    acc = jnp.zeros((2 * tq, DA_V_DIM), F32)
    for ki, start, size in chunks:
        k = k_refs[ki][start:start + size, :]
        v = v_refs[ki][start:start + size, :]
        s = lax.dot_general(qq, k, NT_DIMS, preferred_element_type=F32)
        m_new = jnp.maximum(m, jnp.max(s, axis=-1, keepdims=True))
        alpha = jnp.exp(m - m_new)
        p = jnp.exp(s - m_new)
        l = alpha * l + jnp.sum(p, axis=-1, keepdims=True)
        acc = alpha * acc + jnp.dot(p.astype(BF16), v, preferred_element_type=F32)
        m = m_new
    o = acc / l
    od = o[:tq] - lam_ref[...] * o[tq:]
    o_ref[...] = (_rms_rows(od) * w_ref[...] * (1.0 - lam_init)).astype(o_ref.dtype)


def diff_attention(q, ks, vs, kv_rows, q_rows, lam_row, subln_w, lam_init):
    t = q.shape[0]
    b = t // q_rows
    tq = min(ATTN_TQ, q_rows)
    nq = q_rows // tq
    chunks = []
    for ki, rows in enumerate(kv_rows):
        kc = min(ATTN_KC, rows)
        chunks += [(ki, s, kc) for s in range(0, rows, kc)]
    qmap = lambda bi, h, qi: (bi * nq + qi, h)
    kvmap = lambda bi, h, qi: (bi, h)
    const = lambda bi, h, qi: (0, 0)
    return pl.pallas_call(
        functools.partial(_attn_kernel, n_kv=len(ks), chunks=tuple(chunks), lam_init=lam_init),
        grid=(b, DA_HEADS, nq),
        in_specs=([pl.BlockSpec((tq, LANES), qmap)]
                  + [pl.BlockSpec((rows, LANES), kvmap) for rows in kv_rows] * 2
                  + [pl.BlockSpec((1, LANES), const)] * 2),
        out_specs=pl.BlockSpec((tq, LANES), qmap),
        out_shape=jax.ShapeDtypeStruct((t, DA_WIDTH), BF16),
        compiler_params=_params("parallel", "parallel", "arbitrary"), name="diff_attention",
    )(q, *ks, *vs, lam_row, subln_w.reshape(1, LANES))


S5_LC = 64
S5_CW = S5_LC * S5_GROUP
S5_SW = 2 * S5_STATE


def _s5_discretize(lam_re, lam_im, log_step, b_re, b_im):
    lr, li = lam_re.astype(F32), lam_im.astype(F32)
    step = jnp.exp(log_step.astype(F32))[:, None]
    mag = jnp.exp(lr * step)
    ab_re, ab_im = mag * jnp.cos(li * step), mag * jnp.sin(li * step)
    den = lr * lr + li * li
    nr, ni = ab_re - 1.0, ab_im
    f_re = (nr * lr + ni * li) / den
    f_im = (ni * lr - nr * li) / den
    br, bi = b_re.astype(F32), b_im.astype(F32)
    bb_re = f_re[..., None] * br - f_im[..., None] * bi
    bb_im = f_re[..., None] * bi + f_im[..., None] * br
    return bb_re, bb_im


def _s5_tables(lam_re, lam_im, log_step, b_re, b_im, c_re, c_im):
    lc, g = S5_LC, S5_GROUPS
    bm, cm, mm, k1, k2 = [], [], [], [], []
    for d in range(2):
        lr, li = lam_re[d].astype(F32), lam_im[d].astype(F32)
        step = jnp.exp(log_step[d].astype(F32))[:, None]
        bb_re, bb_im = _s5_discretize(lam_re[d], lam_im[d], log_step[d], b_re[d], b_im[d])
        tau = jnp.arange(lc + 1, dtype=F32)[:, None, None]
        mag = jnp.exp(tau * (lr * step))
        ang = tau * (li * step)
        pr, pi = mag * jnp.cos(ang), mag * jnp.sin(ang)
        abr = pr[..., None] * bb_re - pi[..., None] * bb_im
        abi = pr[..., None] * bb_im + pi[..., None] * bb_re
        cr, ci = c_re[d].astype(F32), c_im[d].astype(F32)
        kern = (jnp.einsum('gkp,tgph->tgkh', cr, abr[:lc], precision=HP)
                - jnp.einsum('gkp,tgph->tgkh', ci, abi[:lc], precision=HP))
        rank = np.arange(lc) if d == 0 else lc - 1 - np.arange(lc)
        e_in = lc - 1 - rank
        bmat = jnp.concatenate([abr[e_in].transpose(1, 0, 3, 2), abi[e_in].transpose(1, 0, 3, 2)], axis=-1)
        bm.append(bmat.reshape(g, S5_CW, S5_SW))
        e_out = rank + 1
        pro, pio = pr[e_out][:, :, None, :], pi[e_out][:, :, None, :]
        car = cr[None] * pro - ci[None] * pio
        cai = cr[None] * pio + ci[None] * pro
        cmat = jnp.concatenate([car.transpose(1, 3, 0, 2), -cai.transpose(1, 3, 0, 2)], axis=1)
        cm.append(cmat.reshape(g, S5_SW, S5_CW))
        dif = rank[None, :] - rank[:, None]
        kg = kern[np.clip(dif, 0, lc - 1)]
        kg = jnp.where((dif >= 0)[:, :, None, None, None], kg, 0.0)
        mm.append(kg.transpose(2, 0, 4, 1, 3).reshape(g, S5_CW, S5_CW))
        alr, ali = pr[lc], pi[lc]
        k1.append(jnp.repeat(jnp.concatenate([alr, alr], axis=-1), 8, axis=0))
        k2.append(jnp.repeat(jnp.concatenate([-ali, ali], axis=-1), 8, axis=0))
    return (jnp.stack(bm).astype(BF16), jnp.stack(cm).astype(BF16), jnp.stack(mm).astype(BF16),
            jnp.stack(k1), jnp.stack(k2))


def _s5_local_kernel(u_ref, bm_ref, s_ref):
    s = jnp.dot(u_ref[0], bm_ref[0, 0], preferred_element_type=F32)
    s_ref[0] = s.reshape(s_ref.shape[1:])


def _s5_carry_kernel(s_ref, k1_ref, k2_ref, xin_ref, *, n_ctx, n_chunks):
    d = pl.program_id(0)
    k1, k2 = k1_ref[0], k2_ref[0]

    def body(k, x):
        rev = jnp.where(k < n_ctx, n_ctx - 1 - k, n_chunks + n_ctx - 1 - k)
        c = jnp.where(d == 0, k, rev)
        xin_ref[0, c] = x
        return k1 * x + k2 * pltpu.roll(x, S5_STATE, 1) + s_ref[0, c]

    lax.fori_loop(0, n_chunks, body, jnp.zeros(k1.shape, F32))


def _s5_out_kernel(u_ref, m_ref, xin_ref, cm_ref, y_ref):
    u = u_ref[0]
    acc = jnp.dot(u, m_ref[0, 0], preferred_element_type=F32)
    acc += jnp.dot(u, m_ref[1, 0], preferred_element_type=F32)
    for d in range(2):
        xin = xin_ref[d].reshape(u.shape[0], S5_SW).astype(BF16)
        acc += jnp.dot(xin, cm_ref[d, 0], preferred_element_type=F32)
    y_ref[0] = acc


def s5_scan(seq, tables, n_ctx):
    bmat, cmat, mmat, k1, k2 = tables
    b, l, _ = seq.shape
    g, lc = S5_GROUPS, S5_LC
    assert b == 8 and l % lc == 0 and n_ctx % lc == 0
    nch = l // lc
    r = nch * b
    u = seq.reshape(b, nch, lc, g, S5_GROUP).transpose(3, 1, 0, 2, 4).reshape(g, r, S5_CW).astype(BF16)
    s = pl.pallas_call(
        _s5_local_kernel,
        grid=(2, g),
        in_specs=[pl.BlockSpec((1, r, S5_CW), lambda d, gi: (gi, 0, 0)),
                  pl.BlockSpec((1, 1, S5_CW, S5_SW), lambda d, gi: (d, gi, 0, 0))],
        out_specs=pl.BlockSpec((1, nch, b, S5_SW), lambda d, gi: (d, 0, gi, 0)),
        out_shape=jax.ShapeDtypeStruct((2, nch, g * b, S5_SW), F32),
        compiler_params=_params("arbitrary", "arbitrary"), name="s5_local",
    )(u, bmat)
    xin = pl.pallas_call(
        functools.partial(_s5_carry_kernel, n_ctx=n_ctx // lc, n_chunks=nch),
        grid=(2,),
        in_specs=[pl.BlockSpec((1, nch, g * b, S5_SW), lambda d: (d, 0, 0, 0)),
                  pl.BlockSpec((1, g * b, S5_SW), lambda d: (d, 0, 0)),
                  pl.BlockSpec((1, g * b, S5_SW), lambda d: (d, 0, 0))],
        out_specs=pl.BlockSpec((1, nch, g * b, S5_SW), lambda d: (d, 0, 0, 0)),
        out_shape=jax.ShapeDtypeStruct((2, nch, g * b, S5_SW), F32),
        compiler_params=_params("arbitrary"), name="s5_carry",
    )(s, k1, k2)
    y = pl.pallas_call(
        _s5_out_kernel,
        grid=(g,),
        in_specs=[pl.BlockSpec((1, r, S5_CW), lambda gi: (gi, 0, 0)),
                  pl.BlockSpec((2, 1, S5_CW, S5_CW), lambda gi: (0, gi, 0, 0)),
                  pl.BlockSpec((2, nch, b, S5_SW), lambda gi: (0, 0, gi, 0)),
                  pl.BlockSpec((2, 1, S5_SW, S5_CW), lambda gi: (0, gi, 0, 0))],
        out_specs=pl.BlockSpec((1, r, S5_CW), lambda gi: (gi, 0, 0)),
        out_shape=jax.ShapeDtypeStruct((g, r, S5_CW), F32),
        compiler_params=_params("arbitrary"), name="s5_out",
    )(u, mmat, xin, cmat)
    return y.reshape(g, nch, b, lc, S5_GROUP).transpose(2, 1, 3, 0, 4).reshape(b, l, S5_WIDTH)


def _gdn_prep_kernel(x_ref, w_ref, o_ref):
    j = pl.program_id(1)
    x = x_ref[...]
    n = x.shape[0]
    row = lax.broadcasted_iota(jnp.int32, x.shape, 0)
    half = GDN_CONV // 2
    acc = x * w_ref[half:half + 1, :]
    for sh in range(1, half + 1):
        acc += jnp.where(row >= sh, pltpu.roll(x, sh, 0), 0.0) * w_ref[half - sh:half - sh + 1, :]
        acc += jnp.where(row < n - sh, pltpu.roll(x, n - sh, 0), 0.0) * w_ref[half + sh:half + sh + 1, :]
    a = acc * _sigmoid(acc)
    lane = lax.broadcasted_iota(jnp.int32, x.shape, 1)
    lo = lane < GDN_DK
    sq = a * a
    s_lo = jnp.sum(jnp.where(lo, sq, 0.0), axis=-1, keepdims=True)
    s_hi = jnp.sum(jnp.where(lo, 0.0, sq), axis=-1, keepdims=True)
    nrm = a * lax.rsqrt(jnp.where(lo, s_lo, s_hi) + EPS)
    q_blocks = GDN_HEADS * GDN_DK // LANES
    nrm = nrm * jnp.where(j < q_blocks, GDN_DK ** -0.5, 1.0)
    o_ref[...] = jnp.where(j < 2 * q_blocks, nrm, a)


def gdn_prep(qkv, conv_w, seg):
    t, w = qkv.shape
    return pl.pallas_call(
        _gdn_prep_kernel,
        grid=(t // seg, w // LANES),
        in_specs=[pl.BlockSpec((seg, LANES), lambda s, j: (s, j)),
                  pl.BlockSpec((GDN_CONV, LANES), lambda s, j: (0, j))],
        out_specs=pl.BlockSpec((seg, LANES), lambda s, j: (s, j)),
        out_shape=jax.ShapeDtypeStruct((t, w), F32),
        compiler_params=_params("parallel", "arbitrary"), name="gdn_prep",
    )(qkv, conv_w)


def _bd(x):
    x2 = jnp.concatenate([x, x], axis=0)
    r = lax.broadcasted_iota(jnp.int32, x2.shape, 0)
    l = lax.broadcasted_iota(jnp.int32, x2.shape, 1)
    return jnp.where((r >> 6) == (l >> 6), x2, jnp.zeros_like(x2))


def _mm(a, b):
    return jnp.dot(a.astype(BF16), b.astype(BF16), preferred_element_type=F32)


def _gdn_chunk(q, k, v, beta, gcol, s_bd, rev):
    c = GDN_CHUNK
    i = lax.broadcasted_iota(jnp.int32, (c, LANES), 0)
    j = lax.broadcasted_iota(jnp.int32, (c, LANES), 1) & (c - 1)
    ti = lax.broadcasted_iota(jnp.int32, (c, c), 0)
    tj = lax.broadcasted_iota(jnp.int32, (c, c), 1)
    if rev:
        causal, strict, upto, lmat = i <= j, i < j, i >= j, (tj >= ti)
    else:
        causal, strict, upto, lmat = i >= j, i > j, i <= j, (tj <= ti)
    lg = jnp.dot(lmat.astype(F32), gcol, precision=HP, preferred_element_type=F32)
    rg = jnp.dot(jnp.ones((c, c), F32), jnp.where(upto, gcol, 0.0), precision=HP, preferred_element_type=F32)
    decay = jnp.where(causal, jnp.exp(jnp.where(causal, lg - rg, 0.0)), 0.0)
    kb = k * beta
    k_bd = _bd(k.astype(BF16))
    a = jnp.where(strict, lax.dot_general(kb.astype(BF16), k_bd, NT_DIMS, preferred_element_type=F32) * decay, 0.0)
    qk = lax.dot_general(q.astype(BF16), k_bd, NT_DIMS, preferred_element_type=F32) * decay
    eg = jnp.exp(lg)
    p = jnp.where(i == j, 1.0, 0.0) - a
    x = a
    for _ in range(5):
        x = _mm(x, _bd(x))
        p = p + _mm(p, _bd(x))
    u = _mm(p, _bd(v * beta))
    w = _mm(p, _bd(kb * eg))
    v_new = u - _mm(w, s_bd)
    o = _mm(q * eg, s_bd) + _mm(qk, _bd(v_new))
    g_last = lg[0:1, :] if rev else lg[c - 1:c, :]
    k_dec = k * jnp.exp(g_last - lg)
    upd = jnp.dot(k_dec.T.astype(BF16), v_new.astype(BF16), preferred_element_type=F32)
    r2 = lax.broadcasted_iota(jnp.int32, upd.shape, 0)
    l2 = lax.broadcasted_iota(jnp.int32, upd.shape, 1)
    s_new = s_bd * jnp.exp(g_last) + jnp.where((r2 >> 6) == (l2 >> 6), upd, 0.0)
    return o, s_new


def _gdn_scan_kernel(qf, kf, vf, baf, qr, kr, vr, bar, alog_ref, dtb_ref, s0_ref,
                     of_ref, or_ref, sfin_ref, s_sc):
    c = pl.program_id(0)

    @pl.when(c == 0)
    def _():
        s_sc[...] = s0_ref[...]

    lane = lax.broadcasted_iota(jnp.int32, (GDN_CHUNK, LANES), 1)
    first = lane < GDN_DK
    ins = ((qf, kf, vf, baf, of_ref), (qr, kr, vr, bar, or_ref))

    def body(b, carry):
        for d in range(2):
            q_ref, k_ref, v_ref, ba_ref, o_ref = ins[d]
            ba = ba_ref[b]
            bsig = _sigmoid(ba)
            sp = ba + dtb_ref[...]
            gall = -jnp.exp(alog_ref[...]) * (jnp.maximum(sp, 0.0) + jnp.log(1.0 + jnp.exp(-jnp.abs(sp))))
            for hp in range(GDN_HEADS // 2):
                col = d * GDN_HEADS + 2 * hp
                beta = jnp.where(first, bsig[:, col:col + 1], bsig[:, col + 1:col + 2])
                gcol = jnp.where(first, gall[:, 8 + col:9 + col], gall[:, 9 + col:10 + col])
                sl = slice(hp * LANES, (hp + 1) * LANES)
                o, s_new = _gdn_chunk(q_ref[b, :, sl], k_ref[b, :, sl], v_ref[b, :, sl], beta, gcol,
                                      s_sc[b, d, hp], rev=(d == 1))
                o_ref[b, :, sl] = o
                s_sc[b, d, hp] = s_new
        return carry

    lax.fori_loop(0, s_sc.shape[0], body, 0)

    @pl.when(c == pl.num_programs(0) - 1)
    def _():
        sfin_ref[...] = s_sc[...]


def gdn_scan(qkvn, ba, s0, alog_row, dtb_row):
    b, l, _ = qkvn.shape
    nch = l // GDN_CHUNK
    blk = (b, GDN_CHUNK, GDN_WIDTH)
    fwd = lambda col: (lambda c: (0, c, col))
    bwd = lambda col: (lambda c: (0, nch - 1 - c, col))
    st = pl.BlockSpec(s0.shape, lambda c: (0, 0, 0, 0, 0))
    return pl.pallas_call(
        _gdn_scan_kernel,
        grid=(nch,),
        in_specs=[pl.BlockSpec(blk, fwd(0)), pl.BlockSpec(blk, fwd(1)), pl.BlockSpec(blk, fwd(2)),
                  pl.BlockSpec((b, GDN_CHUNK, LANES), fwd(0)),
                  pl.BlockSpec(blk, bwd(0)), pl.BlockSpec(blk, bwd(1)), pl.BlockSpec(blk, bwd(2)),
                  pl.BlockSpec((b, GDN_CHUNK, LANES), bwd(0)),
                  pl.BlockSpec((1, LANES), lambda c: (0, 0)), pl.BlockSpec((1, LANES), lambda c: (0, 0)), st],
        out_specs=[pl.BlockSpec(blk, fwd(0)), pl.BlockSpec(blk, bwd(0)), st],
        out_shape=[jax.ShapeDtypeStruct((b, l, GDN_WIDTH), F32)] * 2 + [jax.ShapeDtypeStruct(s0.shape, F32)],
        scratch_shapes=[pltpu.VMEM(s0.shape, F32)],
        compiler_params=_params("arbitrary"), name="gdn_scan",
    )(qkvn, qkvn, qkvn, ba, qkvn, qkvn, qkvn, ba, alog_row, dtb_row, s0)


OUTPROJ_TM = 512


def _outproj_kernel(x_ref, mod_ref, da_ref, ys_ref, u_ref, dsk_ref, wg_ref, bg_ref,
                    of_ref, or_ref, z_ref, gnw_ref, avg_ref, w_ref, o_ref):
    y = ys_ref[...] + u_ref[...] * dsk_ref[...]
    zz = 0.5 * y * (1.0 + jnp.tanh(math.sqrt(2.0 / math.pi) * (y + 0.044715 * (y * y * y))))
    glu = zz * _sigmoid(jnp.dot(zz.astype(BF16), wg_ref[...], preferred_element_type=F32) + bg_ref[...])
    o = of_ref[...] + or_ref[...]
    ms = jnp.dot(o * o, avg_ref[...], precision=HP, preferred_element_type=F32)
    z = z_ref[...]
    gd = o * lax.rsqrt(ms + EPS) * gnw_ref[...] * (z * _sigmoid(z))
    acc = jnp.dot(da_ref[...], w_ref[0:DA_WIDTH, :], preferred_element_type=F32)
    acc += jnp.dot(glu.astype(BF16), w_ref[DA_WIDTH:DA_WIDTH + S5_WIDTH, :], preferred_element_type=F32)
    acc += jnp.dot(gd.astype(BF16), w_ref[DA_WIDTH + S5_WIDTH:, :], preferred_element_type=F32)
    o_ref[...] = x_ref[...] + mod_ref[0, 2:3, :] * acc


def outproj(x, mod, rows_per_mod, da, ys, u, dsk, w_glu, b_glu, o_f, o_r, z, gnw, avg, w_out):
    t, d = x.shape
    tm = min(OUTPROJ_TM, t)
    tpm = rows_per_mod // tm
    row = lambda i: (i, 0)
    const = lambda i: (0, 0)
    sw = S5_WIDTH
    return pl.pallas_call(
        _outproj_kernel,
        grid=(t // tm,),
        in_specs=[pl.BlockSpec((tm, d), row),
                  pl.BlockSpec((1, 6, d), lambda i: (i // tpm, 0, 0)),
                  pl.BlockSpec((tm, DA_WIDTH), row),
                  pl.BlockSpec((tm, sw), row), pl.BlockSpec((tm, sw), row),
                  pl.BlockSpec((1, sw), const), pl.BlockSpec((sw, sw), const), pl.BlockSpec((1, sw), const),
                  pl.BlockSpec((tm, GDN_WIDTH), row), pl.BlockSpec((tm, GDN_WIDTH), row),
                  pl.BlockSpec((tm, GDN_WIDTH), row),
                  pl.BlockSpec((1, GDN_WIDTH), const), pl.BlockSpec((GDN_WIDTH, GDN_WIDTH), const),
                  pl.BlockSpec((D_MIX, d), const)],
        out_specs=pl.BlockSpec((tm, d), row),
        out_shape=jax.ShapeDtypeStruct((t, d), F32),
        compiler_params=_params("parallel"), name="outproj",
    )(x, mod, da, ys, u, dsk, w_glu, b_glu, o_f, o_r, z, gnw, avg, w_out)


MOE_TM = 1024


def _pick_lowest(cur, idx, sentinel, axis):
    m = jnp.max(cur, axis=axis, keepdims=True)
    first = jnp.min(jnp.where(cur == m, idx, sentinel), axis=axis, keepdims=True)
    return idx == first


def _route(logits_t, bias):
    tm = logits_t.shape[1]
    neg = jnp.float32(-jnp.inf)
    scores = jax.nn.sigmoid(logits_t)
    biased = scores + bias
    b3 = biased.reshape(N_GROUPS, GROUP_SIZE, tm)
    eidx = lax.broadcasted_iota(jnp.int32, b3.shape, 1)
    m1 = jnp.max(b3, axis=1, keepdims=True)
    p1 = _pick_lowest(b3, eidx, GROUP_SIZE, 1)
    m2 = jnp.max(jnp.where(p1, neg, b3), axis=1, keepdims=True)
    gs = (m1 + m2).reshape(N_GROUPS, tm)
    gidx = lax.broadcasted_iota(jnp.int32, gs.shape, 0)
    gsel = jnp.zeros(gs.shape, jnp.bool_)
    cur = gs
    for _ in range(TOPK_GROUPS):
        pick = _pick_lowest(cur, gidx, N_GROUPS, 0)
        gsel = jnp.logical_or(gsel, pick)
        cur = jnp.where(pick, neg, cur)
    emask = jnp.broadcast_to(gsel.reshape(N_GROUPS, 1, tm), b3.shape)
    cur = jnp.where(emask, b3, neg).reshape(N_EXPERTS, tm)
    ridx = lax.broadcasted_iota(jnp.int32, cur.shape, 0)
    sel = jnp.zeros(cur.shape, jnp.bool_)
    for _ in range(TOP_K):
        pick = _pick_lowest(cur, ridx, N_EXPERTS, 0)
        sel = jnp.logical_or(sel, pick)
        cur = jnp.where(pick, neg, cur)
    w = jnp.where(sel, scores, 0.0)
    return w / jnp.sum(w, axis=0, keepdims=True) * ROUTED_SCALE


def _moe_kernel(x_ref, mod_ref, nw_ref, wr_ref, rb_ref, w13_ref, w2_ref, fnw_ref, o_ref,
                h_sc, gate_sc, acc_sc, *, final_norm):
    e = pl.program_id(1)
    n_e = pl.num_programs(1)

    @pl.when(e == 0)
    def _():
        h = _rms_rows(x_ref[...]) * nw_ref[...] * (1.0 + mod_ref[0, 4:5, :]) + mod_ref[0, 3:4, :]
        h_sc[...] = h.astype(BF16)
        logits_t = lax.dot_general(wr_ref[...], h, NT_DIMS, precision=HP, preferred_element_type=F32)
        w = _route(logits_t, rb_ref[...])
        tm = w.shape[1]
        row = lax.broadcasted_iota(jnp.int32, (LANES - N_EXPERTS, tm), 0)
        shared = jnp.where(row == 0, 1.0, 0.0).astype(F32)
        gate_sc[...] = jnp.concatenate([w, shared], axis=0).T
        acc_sc[...] = jnp.zeros_like(acc_sc)

    h = h_sc[...]
    ab = jnp.dot(h, w13_ref[0], preferred_element_type=F32)
    a, b = ab[:, :MOE_FFN], ab[:, MOE_FFN:]
    lane = lax.broadcasted_iota(jnp.int32, (1, LANES), 1)
    g = jnp.sum(jnp.where(lane == e, gate_sc[...], 0.0), axis=-1, keepdims=True)
    hid = a * _sigmoid(a) * b * g
    acc_sc[...] += jnp.dot(hid.astype(BF16), w2_ref[0], preferred_element_type=F32)

    @pl.when(e == n_e - 1)
    def _():
        y = x_ref[...] + mod_ref[0, 5:6, :] * acc_sc[...]
        if final_norm:
            y = _rms_rows(y) * fnw_ref[...]
        o_ref[...] = y


def moe_sublayer(x, mod, rows_per_mod, norm_w, w_router_t, router_bias, w13, w2, final_w, final_norm):
    t, d = x.shape
    tm = min(MOE_TM, rows_per_mod)
    assert t % tm == 0 and rows_per_mod % tm == 0
    n_slots = w13.shape[0]
    tiles_per_mod = rows_per_mod // tm
    return pl.pallas_call(
        functools.partial(_moe_kernel, final_norm=final_norm),
        grid=(t // tm, n_slots),
        in_specs=[
            pl.BlockSpec((tm, d), lambda i, e: (i, 0)),
            pl.BlockSpec((1, 6, d), lambda i, e: (i // tiles_per_mod, 0, 0)),
            pl.BlockSpec((1, d), lambda i, e: (0, 0)),
            pl.BlockSpec((N_EXPERTS, d), lambda i, e: (0, 0)),
            pl.BlockSpec((N_EXPERTS, 1), lambda i, e: (0, 0)),
            pl.BlockSpec((1, d, 2 * MOE_FFN), lambda i, e: (e, 0, 0)),
            pl.BlockSpec((1, MOE_FFN, d), lambda i, e: (e, 0, 0)),
            pl.BlockSpec((1, d), lambda i, e: (0, 0)),
        ],
        out_specs=pl.BlockSpec((tm, d), lambda i, e: (i, 0)),
        out_shape=jax.ShapeDtypeStruct((t, d), F32),
        scratch_shapes=[
            pltpu.VMEM((tm, d), BF16),
            pltpu.VMEM((tm, LANES), F32),
            pltpu.VMEM((tm, d), F32),
        ],
        compiler_params=_params("parallel", "arbitrary"), name="moe_sublayer",
    )(x, mod, norm_w.reshape(1, d), w_router_t, router_bias.reshape(N_EXPERTS, 1), w13, w2, final_w.reshape(1, d))


def _moe_weights(w1, w3, w2, ws1, ws3, ws2):
    w13 = jnp.concatenate([jnp.concatenate([w1, w3], axis=-1),
                           jnp.concatenate([ws1, ws3], axis=-1)[None]], axis=0).astype(BF16)
    w2a = jnp.concatenate([w2, ws2[None]], axis=0).astype(BF16)
    return w13, w2a


def kernel(x, c, ctx, c_ctx, norm1_w, norm2_w, w_mod, b_mod, w_in, w_out, da_lambda, da_subln_w,
           s5_lam_re, s5_lam_im, s5_log_step, s5_b_re, s5_b_im, s5_c_re, s5_c_im, s5_d, s5_w_glu, s5_b_glu,
           gdn_conv_w, gdn_a_log, gdn_dt_bias, gdn_norm_w,
           moe_w_router, moe_router_bias, moe_w1, moe_w3, moe_w2, moe_ws1, moe_ws3, moe_ws2,
           final_norm_w):
    b, n, d = x.shape
    nc = ctx.shape[1]
    cos_t, sin_t = _rope_tables(n)
    c_act = jax.nn.silu(c)
    cc_act = jax.nn.silu(c_ctx)
    xl = x.reshape(b * n, d)
    xc = ctx.reshape(b * nc, d)
    head_avg = jnp.kron(jnp.eye(GDN_HEADS, dtype=F32), jnp.full((GDN_DV, GDN_DV), 1.0 / GDN_DV, F32))
    s_zero = jnp.zeros((b, 2, GDN_HEADS // 2, LANES, LANES), F32)
    for i in range(DEPTH):
        ctx_out = i < DEPTH - 1
        last = i == DEPTH - 1
        lam_init = 0.8 - 0.6 * math.exp(-0.3 * i)
        mod = (c_act @ w_mod[i] + b_mod[i]).reshape(b, 6, d)
        modc = (cc_act @ w_mod[i] + b_mod[i]).reshape(1, 6, d)

        w_in_p = jnp.pad(w_in[i], ((0, 0), (0, IN_PAD - IN_WIDTH))).astype(BF16)
        q, k, v, u, gq, z, ba = inproj(xl, mod, n, norm1_w[i], w_in_p, cos_t, sin_t, rope=True)
        qc, kc, vc, uc, gqc, zc, bac = inproj(xc, modc, b * nc, norm1_w[i], w_in_p, cos_t, sin_t, rope=False)

        lq1, lk1, lq2, lk2 = da_lambda[i].astype(F32)
        lam = jnp.exp(jnp.sum(lq1 * lk1)) - jnp.exp(jnp.sum(lq2 * lk2)) + lam_init
        lam_row = jnp.full((1, LANES), lam, F32)
        da = diff_attention(q, [kc, k], [vc, v], (nc, n), n, lam_row, da_subln_w[i], lam_init)

        tables = _s5_tables(s5_lam_re[i], s5_lam_im[i], s5_log_step[i], s5_b_re[i], s5_b_im[i],
                            s5_c_re[i], s5_c_im[i])
        seq = jnp.concatenate([uc.reshape(b, nc, S5_WIDTH), u.reshape(b, n, S5_WIDTH)], axis=1)
        ys = s5_scan(seq, tables, nc)

        alog_row = jnp.zeros((1, LANES), F32).at[0, 8:16].set(gdn_a_log[i].astype(F32).reshape(-1))
        dtb_row = jnp.zeros((1, LANES), F32).at[0, 8:16].set(gdn_dt_bias[i].astype(F32).reshape(-1))
        gn = gdn_prep(gq, gdn_conv_w[i], n)
        gnc = gdn_prep(gqc, gdn_conv_w[i], nc)
        ofc, orc, s_ctx = gdn_scan(gnc.reshape(b, nc, GDN_QKV_W), bac.reshape(b, nc, LANES), s_zero, alog_row, dtb_row)
        of, orv, _ = gdn_scan(gn.reshape(b, n, GDN_QKV_W), ba.reshape(b, n, LANES), s_ctx, alog_row, dtb_row)

        w_out_b = w_out[i].astype(BF16)
        dsk = s5_d[i].astype(F32).reshape(1, S5_WIDTH)
        wg = s5_w_glu[i].astype(BF16)
        bg = s5_b_glu[i].astype(F32).reshape(1, S5_WIDTH)
        gnw = jnp.tile(gdn_norm_w[i].astype(F32), GDN_HEADS).reshape(1, GDN_WIDTH)
        xl = outproj(xl, mod, n, da, ys[:, nc:].reshape(b * n, S5_WIDTH), u, dsk, wg, bg,
                     of.reshape(b * n, GDN_WIDTH), orv.reshape(b * n, GDN_WIDTH), z, gnw, head_avg, w_out_b)
        if ctx_out:
            dac = diff_attention(qc, [kc], [vc], (nc,), nc, lam_row, da_subln_w[i], lam_init)
            xc = outproj(xc, modc, b * nc, dac, ys[:, :nc].reshape(b * nc, S5_WIDTH), uc, dsk, wg, bg,
                         ofc.reshape(b * nc, GDN_WIDTH), orc.reshape(b * nc, GDN_WIDTH), zc, gnw, head_avg, w_out_b)

        w13, w2a = _moe_weights(moe_w1[i], moe_w3[i], moe_w2[i], moe_ws1[i], moe_ws3[i], moe_ws2[i])
        wr_t = moe_w_router[i].T
        xl = moe_sublayer(xl, mod, n, norm2_w[i], wr_t, moe_router_bias[i], w13, w2a, final_norm_w, last)
        if ctx_out:
            xc = moe_sublayer(xc, modc, b * nc, norm2_w[i], wr_t, moe_router_bias[i], w13, w2a, final_norm_w, False)
    return xl.reshape(b, n, d)
```

```python
import functools
import math

import jax
import jax.numpy as jnp
import numpy as np
from jax import lax
from jax.experimental import pallas as pl
from jax.experimental.pallas import tpu as pltpu

F32 = jnp.float32
BF16 = jnp.bfloat16

D_MODEL = 1024
DEPTH = 2
GRID_W = 64
EPS = 1e-6

DA_HEADS = 4
DA_HEAD_DIM = D_MODEL // 16
DA_V_DIM = 2 * DA_HEAD_DIM
DA_WIDTH = DA_HEADS * DA_V_DIM
ROPE_THETA = 10000.0

S5_WIDTH = D_MODEL // 4
S5_GROUP = 16
S5_GROUPS = S5_WIDTH // S5_GROUP
S5_STATE = 64

GDN_HEADS = 4
GDN_DK = D_MODEL // 16
GDN_DV = D_MODEL // 16
GDN_WIDTH = GDN_HEADS * GDN_DV
GDN_CONV = 5
GDN_CHUNK = 64

D_MIX = DA_WIDTH + S5_WIDTH + GDN_WIDTH
DA_QK_W = 2 * DA_HEADS * DA_HEAD_DIM
GDN_QKV_W = 2 * GDN_HEADS * GDN_DK + GDN_HEADS * GDN_DV
IN_SIZES = (DA_QK_W, DA_QK_W, DA_WIDTH, S5_WIDTH, GDN_QKV_W, GDN_WIDTH, 2 * GDN_HEADS, 2 * GDN_HEADS)
IN_WIDTH = sum(IN_SIZES)

N_EXPERTS = 64
TOP_K = 8
N_GROUPS = 8
GROUP_SIZE = N_EXPERTS // N_GROUPS
TOPK_GROUPS = 4
MOE_FFN = D_MODEL // 4
ROUTED_SCALE = 2.5

LANES = 128
VMEM_LIMIT_BYTES = 56 * 1024 * 1024

HP = lax.Precision.HIGHEST
NT_DIMS = (((1,), (1,)), ((), ()))


def _params(*sem):
    return pltpu.CompilerParams(dimension_semantics=sem, vmem_limit_bytes=VMEM_LIMIT_BYTES)


def _sigmoid(x):
    return 0.5 * (1.0 + jnp.tanh(0.5 * x))


def _rms_rows(x):
    return x * lax.rsqrt(jnp.mean(x * x, axis=-1, keepdims=True) + EPS)


IN_PAD = 2944
INPROJ_TM = 512
O_Q, O_K, O_V, O_U, O_G, O_Z, O_BA = 0, 512, 1024, 1536, 1792, 2560, 2816


def _rope_apply(x, cos, sin):
    lane = lax.broadcasted_iota(jnp.int32, x.shape, 1)
    up = pltpu.roll(x, LANES - 16, 1)
    dn = pltpu.roll(x, 16, 1)
    partner = jnp.where((lane & 31) < 16, up, dn)
    return x * cos + partner * sin


def _inproj_kernel(x_ref, mod_ref, nw_ref, w_ref, cos_ref, sin_ref,
                   q_ref, k_ref, v_ref, u_ref, g_ref, z_ref, ba_ref, *, rope):
    h = (_rms_rows(x_ref[...]) * nw_ref[...] * (1.0 + mod_ref[0, 1:2, :]) + mod_ref[0, 0:1, :]).astype(BF16)

    def proj(lo, hi):
        return jnp.dot(h, w_ref[:, lo:hi], preferred_element_type=F32)

    scale = DA_HEAD_DIM ** -0.5 * math.log2(math.e)
    for hd in range(DA_HEADS):
        lo = hd * LANES
        qs = proj(O_Q + lo, O_Q + lo + LANES)
        ks = proj(O_K + lo, O_K + lo + LANES)
        if rope:
            qs = _rope_apply(qs, cos_ref[...], sin_ref[...])
            ks = _rope_apply(ks, cos_ref[...], sin_ref[...])
        q_ref[:, lo:lo + LANES] = (qs * scale).astype(BF16)
        k_ref[:, lo:lo + LANES] = ks.astype(BF16)
    v_ref[...] = proj(O_V, O_U).astype(BF16)
    u_ref[...] = proj(O_U, O_G)
    g_ref[...] = proj(O_G, O_Z)
    z_ref[...] = proj(O_Z, O_BA)
    ba_ref[...] = proj(O_BA, IN_PAD)


def inproj(x, mod, rows_per_mod, norm_w, w_pad, cos_t, sin_t, rope):
    t, d = x.shape
    tm = min(INPROJ_TM, t)
    tpm = rows_per_mod // tm
    npos = cos_t.shape[0] // tm
    row = lambda i: (i, 0)
    widths = (DA_QK_W, DA_QK_W, DA_WIDTH, S5_WIDTH, GDN_QKV_W, GDN_WIDTH, LANES)
    dtypes = (BF16, BF16, BF16, F32, F32, F32, F32)
    return pl.pallas_call(
        functools.partial(_inproj_kernel, rope=rope),
        grid=(t // tm,),
        in_specs=[pl.BlockSpec((tm, d), row),
                  pl.BlockSpec((1, 6, d), lambda i: (i // tpm, 0, 0)),
                  pl.BlockSpec((1, d), lambda i: (0, 0)),
                  pl.BlockSpec((d, IN_PAD), lambda i: (0, 0)),
                  pl.BlockSpec((tm, LANES), lambda i: (i % npos, 0)),
                  pl.BlockSpec((tm, LANES), lambda i: (i % npos, 0))],
        out_specs=[pl.BlockSpec((tm, w), row) for w in widths],
        out_shape=[jax.ShapeDtypeStruct((t, w), dt) for w, dt in zip(widths, dtypes)],
        compiler_params=_params("parallel"), name="inproj",
    )(x, mod, norm_w.reshape(1, d), w_pad, cos_t, sin_t)


def _rope_tables(n):
    nf = DA_HEAD_DIM // 4
    t = jnp.arange(n, dtype=jnp.int32)
    inv = ROPE_THETA ** (-jnp.arange(nf, dtype=F32) / nf)
    ang_r = (t // GRID_W).astype(F32)[:, None] * inv
    ang_c = (t % GRID_W).astype(F32)[:, None] * inv
    cos64 = jnp.concatenate([jnp.cos(ang_r), jnp.cos(ang_r), jnp.cos(ang_c), jnp.cos(ang_c)], axis=-1)
    sin64 = jnp.concatenate([-jnp.sin(ang_r), jnp.sin(ang_r), -jnp.sin(ang_c), jnp.sin(ang_c)], axis=-1)
    return jnp.tile(cos64, (1, 2)), jnp.tile(sin64, (1, 2))


ATTN_TQ = 256
ATTN_KC = 1024


def _attn_kernel(*refs, n_kv, kv_rows, chunks, lam_init):
    q_ref = refs[0]
    k_refs = refs[1:1 + n_kv]
    v_refs = refs[1 + n_kv:1 + 2 * n_kv]
    lam_ref, w_ref, o_ref, v1_sc = refs[1 + 2 * n_kv:]

    @pl.when(pl.program_id(2) == 0)
    def _():
        off = 0
        for ki, rows in enumerate(kv_rows):
            v1_sc[off:off + rows, :DA_V_DIM] = v_refs[ki][...]
            v1_sc[off:off + rows, DA_V_DIM:] = jnp.ones((rows, DA_V_DIM), BF16)
            off += rows

    q = q_ref[...]
    tq = q.shape[0]
    lane = lax.broadcasted_iota(jnp.int32, q.shape, 1)
    zero = jnp.zeros_like(q)
    qq = jnp.concatenate([jnp.where(lane < DA_HEAD_DIM, q, zero), jnp.where(lane >= DA_HEAD_DIM, q, zero)], axis=0)
    m = jnp.full((2 * tq, 1), -jnp.inf, F32)
    acc = jnp.zeros((2 * tq, 2 * DA_V_DIM), F32)
    def scores(chunk):
        ki, start, _, size = chunk
        return lax.dot_general(qq, k_refs[ki][start:start + size, :], NT_DIMS, preferred_element_type=F32)

    s_next = scores(chunks[0])
    for ci, (ki, start, off, size) in enumerate(chunks):
        s = s_next
        if ci + 1 < len(chunks):
            s_next = scores(chunks[ci + 1])
        m_new = jnp.maximum(m, jnp.max(s, axis=-1, keepdims=True))
        p = jnp.exp2(s - m_new).astype(BF16)
        acc = jnp.exp2(m - m_new) * acc + jnp.dot(p, v1_sc[off:off + size, :], preferred_element_type=F32)
        m = m_new
    o = acc[:, :DA_V_DIM] / acc[:, DA_V_DIM:]
    od = o[:tq] - lam_ref[...] * o[tq:]
    o_ref[...] = (_rms_rows(od) * w_ref[...] * (1.0 - lam_init)).astype(o_ref.dtype)


def diff_attention(q, ks, vs, kv_rows, q_rows, lam_row, subln_w, lam_init):
    t = q.shape[0]
    b = t // q_rows
    tq = min(ATTN_TQ, q_rows)
    nq = q_rows // tq
    chunks, off = [], 0
    for ki, rows in enumerate(kv_rows):
        kc = min(ATTN_KC, rows)
        chunks += [(ki, s, off + s, kc) for s in range(0, rows, kc)]
        off += rows
    qmap = lambda bi, h, qi: (bi * nq + qi, h)
    kvmap = lambda bi, h, qi: (bi, h)
    const = lambda bi, h, qi: (0, 0)
    return pl.pallas_call(
        functools.partial(_attn_kernel, n_kv=len(ks), kv_rows=tuple(kv_rows), chunks=tuple(chunks),
                          lam_init=lam_init),
        grid=(b, DA_HEADS, nq),
        in_specs=([pl.BlockSpec((tq, LANES), qmap)]
                  + [pl.BlockSpec((rows, LANES), kvmap) for rows in kv_rows] * 2
                  + [pl.BlockSpec((1, LANES), const)] * 2),
        out_specs=pl.BlockSpec((tq, LANES), qmap),
        out_shape=jax.ShapeDtypeStruct((t, DA_WIDTH), BF16),
        scratch_shapes=[pltpu.VMEM((off, 2 * DA_V_DIM), BF16)],
        compiler_params=_params("parallel", "parallel", "arbitrary"), name="diff_attention",
    )(q, *ks, *vs, lam_row, subln_w.reshape(1, LANES))


S5_LC = 64
S5_CW = S5_LC * S5_GROUP
S5_SW = 2 * S5_STATE


def _s5_discretize(lam_re, lam_im, log_step, b_re, b_im):
    lr, li = lam_re.astype(F32), lam_im.astype(F32)
    step = jnp.exp(log_step.astype(F32))[:, None]
    mag = jnp.exp(lr * step)
    ab_re, ab_im = mag * jnp.cos(li * step), mag * jnp.sin(li * step)
    den = lr * lr + li * li
    nr, ni = ab_re - 1.0, ab_im
    f_re = (nr * lr + ni * li) / den
    f_im = (ni * lr - nr * li) / den
    br, bi = b_re.astype(F32), b_im.astype(F32)
    bb_re = f_re[..., None] * br - f_im[..., None] * bi
    bb_im = f_re[..., None] * bi + f_im[..., None] * br
    return bb_re, bb_im


def _s5_tables(lam_re, lam_im, log_step, b_re, b_im, c_re, c_im):
    lc, g = S5_LC, S5_GROUPS
    bm, cm, mm, k1, k2 = [], [], [], [], []
    for d in range(2):
        lr, li = lam_re[d].astype(F32), lam_im[d].astype(F32)
        step = jnp.exp(log_step[d].astype(F32))[:, None]
        bb_re, bb_im = _s5_discretize(lam_re[d], lam_im[d], log_step[d], b_re[d], b_im[d])
        tau = jnp.arange(lc + 1, dtype=F32)[:, None, None]
        mag = jnp.exp(tau * (lr * step))
        ang = tau * (li * step)
        pr, pi = mag * jnp.cos(ang), mag * jnp.sin(ang)
        abr = pr[..., None] * bb_re - pi[..., None] * bb_im
        abi = pr[..., None] * bb_im + pi[..., None] * bb_re
        cr, ci = c_re[d].astype(F32), c_im[d].astype(F32)
        kern = (jnp.einsum('gkp,tgph->tgkh', cr, abr[:lc], precision=HP)
                - jnp.einsum('gkp,tgph->tgkh', ci, abi[:lc], precision=HP))
        rank = np.arange(lc) if d == 0 else lc - 1 - np.arange(lc)
        e_in = lc - 1 - rank
        bmat = jnp.concatenate([abr[e_in].transpose(1, 0, 3, 2), abi[e_in].transpose(1, 0, 3, 2)], axis=-1)
        bm.append(bmat.reshape(g, S5_CW, S5_SW))
        e_out = rank + 1
        pro, pio = pr[e_out][:, :, None, :], pi[e_out][:, :, None, :]
        car = cr[None] * pro - ci[None] * pio
        cai = cr[None] * pio + ci[None] * pro
        cmat = jnp.concatenate([car.transpose(1, 3, 0, 2), -cai.transpose(1, 3, 0, 2)], axis=1)
        cm.append(cmat.reshape(g, S5_SW, S5_CW))
        kt = kern.transpose(1, 3, 0, 2).astype(BF16)
        zpad = jnp.zeros((g, S5_GROUP, lc - 1, S5_GROUP), BF16)
        strip = jnp.concatenate([zpad, kt] if d == 0 else [kt[:, :, ::-1], zpad], axis=2)
        strip = strip.reshape(g, S5_GROUP, (2 * lc - 1) * S5_GROUP)
        rows = [strip[:, :, (lc - 1 - s) * S5_GROUP:(2 * lc - 1 - s) * S5_GROUP] for s in range(lc)]
        mm.append(jnp.stack(rows, axis=1).reshape(g, S5_CW, S5_CW))
        alr, ali = pr[lc], pi[lc]
        k1.append(jnp.repeat(jnp.concatenate([alr, alr], axis=-1), 8, axis=0))
        k2.append(jnp.repeat(jnp.concatenate([-ali, ali], axis=-1), 8, axis=0))
    return (jnp.stack(bm).astype(BF16), jnp.stack(cm).astype(BF16), jnp.stack(mm).astype(BF16),
            jnp.stack(k1), jnp.stack(k2))


def _s5_local_kernel(u_ref, bm_ref, s_ref):
    s = jnp.dot(u_ref[0], bm_ref[0, 0], preferred_element_type=F32)
    s_ref[0] = s.reshape(s_ref.shape[1:])


def _s5_carry_kernel(s_ref, k1_ref, k2_ref, xin_ref, *, n_ctx, n_chunks):
    d = pl.program_id(0)
    k1, k2 = k1_ref[0], k2_ref[0]

    def body(k, x):
        rev = jnp.where(k < n_ctx, n_ctx - 1 - k, n_chunks + n_ctx - 1 - k)
        c = jnp.where(d == 0, k, rev)
        xin_ref[0, c] = x
        return k1 * x + k2 * pltpu.roll(x, S5_STATE, 1) + s_ref[0, c]

    lax.fori_loop(0, n_chunks, body, jnp.zeros(k1.shape, F32))


def _s5_out_kernel(u_ref, m_ref, xin_ref, cm_ref, y_ref):
    u = u_ref[0]
    acc = jnp.dot(u, m_ref[0, 0], preferred_element_type=F32)
    acc += jnp.dot(u, m_ref[1, 0], preferred_element_type=F32)
    for d in range(2):
        xin = xin_ref[d].reshape(u.shape[0], S5_SW).astype(BF16)
        acc += jnp.dot(xin, cm_ref[d, 0], preferred_element_type=F32)
    y_ref[0] = acc


def s5_scan(seq, tables, n_ctx):
    bmat, cmat, mmat, k1, k2 = tables
    b, l, _ = seq.shape
    g, lc = S5_GROUPS, S5_LC
    assert b == 8 and l % lc == 0 and n_ctx % lc == 0
    nch = l // lc
    r = nch * b
    u = seq.reshape(b, nch, lc, g, S5_GROUP).transpose(3, 1, 0, 2, 4).reshape(g, r, S5_CW).astype(BF16)
    s = pl.pallas_call(
        _s5_local_kernel,
        grid=(2, g),
        in_specs=[pl.BlockSpec((1, r, S5_CW), lambda d, gi: (gi, 0, 0)),
                  pl.BlockSpec((1, 1, S5_CW, S5_SW), lambda d, gi: (d, gi, 0, 0))],
        out_specs=pl.BlockSpec((1, nch, b, S5_SW), lambda d, gi: (d, 0, gi, 0)),
        out_shape=jax.ShapeDtypeStruct((2, nch, g * b, S5_SW), F32),
        compiler_params=_params("arbitrary", "arbitrary"), name="s5_local",
    )(u, bmat)
    xin = pl.pallas_call(
        functools.partial(_s5_carry_kernel, n_ctx=n_ctx // lc, n_chunks=nch),
        grid=(2,),
        in_specs=[pl.BlockSpec((1, nch, g * b, S5_SW), lambda d: (d, 0, 0, 0)),
                  pl.BlockSpec((1, g * b, S5_SW), lambda d: (d, 0, 0)),
                  pl.BlockSpec((1, g * b, S5_SW), lambda d: (d, 0, 0))],
        out_specs=pl.BlockSpec((1, nch, g * b, S5_SW), lambda d: (d, 0, 0, 0)),
        out_shape=jax.ShapeDtypeStruct((2, nch, g * b, S5_SW), F32),
        compiler_params=_params("arbitrary"), name="s5_carry",
    )(s, k1, k2)
    y = pl.pallas_call(
        _s5_out_kernel,
        grid=(g,),
        in_specs=[pl.BlockSpec((1, r, S5_CW), lambda gi: (gi, 0, 0)),
                  pl.BlockSpec((2, 1, S5_CW, S5_CW), lambda gi: (0, gi, 0, 0)),
                  pl.BlockSpec((2, nch, b, S5_SW), lambda gi: (0, 0, gi, 0)),
                  pl.BlockSpec((2, 1, S5_SW, S5_CW), lambda gi: (0, gi, 0, 0))],
        out_specs=pl.BlockSpec((1, r, S5_CW), lambda gi: (gi, 0, 0)),
        out_shape=jax.ShapeDtypeStruct((g, r, S5_CW), F32),
        compiler_params=_params("arbitrary"), name="s5_out",
    )(u, mmat, xin, cmat)
    return y.reshape(g, nch, b, lc, S5_GROUP).transpose(2, 1, 3, 0, 4).reshape(b, l, S5_WIDTH)


def _gdn_prep_kernel(x_ref, w_ref, o_ref):
    j = pl.program_id(1)
    x = x_ref[...]
    n = x.shape[0]
    row = lax.broadcasted_iota(jnp.int32, x.shape, 0)
    half = GDN_CONV // 2
    acc = x * w_ref[half:half + 1, :]
    for sh in range(1, half + 1):
        acc += jnp.where(row >= sh, pltpu.roll(x, sh, 0), 0.0) * w_ref[half - sh:half - sh + 1, :]
        acc += jnp.where(row < n - sh, pltpu.roll(x, n - sh, 0), 0.0) * w_ref[half + sh:half + sh + 1, :]
    a = acc * _sigmoid(acc)
    lane = lax.broadcasted_iota(jnp.int32, x.shape, 1)
    lo = lane < GDN_DK
    sq = a * a
    s_lo = jnp.sum(jnp.where(lo, sq, 0.0), axis=-1, keepdims=True)
    s_hi = jnp.sum(jnp.where(lo, 0.0, sq), axis=-1, keepdims=True)
    nrm = a * lax.rsqrt(jnp.where(lo, s_lo, s_hi) + EPS)
    q_blocks = GDN_HEADS * GDN_DK // LANES
    nrm = nrm * jnp.where(j < q_blocks, GDN_DK ** -0.5, 1.0)
    o_ref[...] = jnp.where(j < 2 * q_blocks, nrm, a)


def gdn_prep(qkv, conv_w, seg):
    t, w = qkv.shape
    return pl.pallas_call(
        _gdn_prep_kernel,
        grid=(t // seg, w // LANES),
        in_specs=[pl.BlockSpec((seg, LANES), lambda s, j: (s, j)),
                  pl.BlockSpec((GDN_CONV, LANES), lambda s, j: (0, j))],
        out_specs=pl.BlockSpec((seg, LANES), lambda s, j: (s, j)),
        out_shape=jax.ShapeDtypeStruct((t, w), F32),
        compiler_params=_params("parallel", "arbitrary"), name="gdn_prep",
    )(qkv, conv_w)


def _bd(x):
    x2 = jnp.concatenate([x, x], axis=0)
    r = lax.broadcasted_iota(jnp.int32, x2.shape, 0)
    l = lax.broadcasted_iota(jnp.int32, x2.shape, 1)
    return jnp.where((r >> 6) == (l >> 6), x2, jnp.zeros_like(x2))


def _mm(a, b):
    return jnp.dot(a.astype(BF16), b.astype(BF16), preferred_element_type=F32)


def _gdn_chunks(insts):
    c = GDN_CHUNK
    n = len(insts)
    every = range(n)
    q, k, v, beta, gcol, s_bd, rev = (list(t) for t in zip(*insts))
    i = lax.broadcasted_iota(jnp.int32, (c, LANES), 0)
    j = lax.broadcasted_iota(jnp.int32, (c, LANES), 1) & (c - 1)
    ti = lax.broadcasted_iota(jnp.int32, (c, c), 0)
    tj = lax.broadcasted_iota(jnp.int32, (c, c), 1)
    causal = [i <= j if r else i >= j for r in rev]
    strict = [i < j if r else i > j for r in rev]
    upto = [i >= j if r else i <= j for r in rev]
    lmat = [(tj >= ti if r else tj <= ti).astype(F32) for r in rev]
    ones = jnp.ones((c, c), F32)
    eye = jnp.where(i == j, 1.0, 0.0)
    lg = [jnp.dot(lmat[t], gcol[t], precision=HP, preferred_element_type=F32) for t in every]
    rg = [jnp.dot(ones, jnp.where(upto[t], gcol[t], 0.0), precision=HP, preferred_element_type=F32) for t in every]
    decay = [jnp.where(causal[t], jnp.exp(jnp.where(causal[t], lg[t] - rg[t], 0.0)), 0.0) for t in every]
    kb = [k[t] * beta[t] for t in every]
    k_bd = [_bd(k[t].astype(BF16)) for t in every]
    a = [jnp.where(strict[t], lax.dot_general(kb[t].astype(BF16), k_bd[t], NT_DIMS, preferred_element_type=F32)
                   * decay[t], 0.0) for t in every]
    qk = [lax.dot_general(q[t].astype(BF16), k_bd[t], NT_DIMS, preferred_element_type=F32) * decay[t] for t in every]
    eg = [jnp.exp(lg[t]) for t in every]
    same = [(i >> sh) == (j >> sh) for sh in (3, 4, 5)]
    x = [jnp.where(same[0], a[t], 0.0) for t in every]
    p = [eye - x[t] for t in every]
    for _ in range(2):
        x = [_mm(x[t], _bd(x[t])) for t in every]
        p = [p[t] + _mm(p[t], _bd(x[t])) for t in every]
    for lvl in range(3):
        inner = same[lvl]
        join = jnp.logical_not(inner) if lvl == 2 else jnp.logical_and(same[lvl + 1], jnp.logical_not(inner))
        tl = [_mm(p[t], _bd(jnp.where(join, a[t], 0.0))) for t in every]
        p = [p[t] - _mm(tl[t], _bd(p[t])) for t in every]
    u = [_mm(p[t], _bd(v[t] * beta[t])) for t in every]
    w = [_mm(p[t], _bd(kb[t] * eg[t])) for t in every]
    v_new = [u[t] - _mm(w[t], s_bd[t]) for t in every]
    o_state = [_mm(q[t] * eg[t], s_bd[t]) for t in every]
    o = [o_state[t] + _mm(qk[t], _bd(v_new[t])) for t in every]
    g_last = [lg[t][0:1, :] if rev[t] else lg[t][c - 1:c, :] for t in every]
    k_dec = [k[t] * jnp.exp(g_last[t] - lg[t]) for t in every]
    upd = [jnp.dot(k_dec[t].T.astype(BF16), v_new[t].astype(BF16), preferred_element_type=F32) for t in every]
    r2 = lax.broadcasted_iota(jnp.int32, (LANES, LANES), 0)
    l2 = lax.broadcasted_iota(jnp.int32, (LANES, LANES), 1)
    diag = (r2 >> 6) == (l2 >> 6)
    s_new = [s_bd[t] * jnp.exp(g_last[t]) + jnp.where(diag, upd[t], 0.0) for t in every]
    return list(zip(o, s_new))


GDN_BATCH_UNROLL = 2


def _gdn_scan_kernel(qf, kf, vf, baf, qr, kr, vr, bar, alog_ref, dtb_ref, s0_ref,
                     of_ref, or_ref, sfin_ref, s_sc):
    c = pl.program_id(0)

    @pl.when(c == 0)
    def _():
        s_sc[...] = s0_ref[...]

    lane = lax.broadcasted_iota(jnp.int32, (GDN_CHUNK, LANES), 1)
    first = lane < GDN_DK
    ins = ((qf, kf, vf, baf, of_ref), (qr, kr, vr, bar, or_ref))

    def body(it, carry):
        work = []
        for bb in range(GDN_BATCH_UNROLL):
            b = it * GDN_BATCH_UNROLL + bb
            for d in range(2):
                q_ref, k_ref, v_ref, ba_ref, o_ref = ins[d]
                ba = ba_ref[b]
                bsig = _sigmoid(ba)
                sp = ba + dtb_ref[...]
                gall = -jnp.exp(alog_ref[...]) * (jnp.maximum(sp, 0.0) + jnp.log(1.0 + jnp.exp(-jnp.abs(sp))))
                for hp in range(GDN_HEADS // 2):
                    col = d * GDN_HEADS + 2 * hp
                    beta = jnp.where(first, bsig[:, col:col + 1], bsig[:, col + 1:col + 2])
                    gcol = jnp.where(first, gall[:, 8 + col:9 + col], gall[:, 9 + col:10 + col])
                    sl = slice(hp * LANES, (hp + 1) * LANES)
                    work.append((b, d, hp, sl, o_ref, (q_ref[b, :, sl], k_ref[b, :, sl], v_ref[b, :, sl],
                                                       beta, gcol, s_sc[b, d, hp], d == 1)))
        done = _gdn_chunks([args for (_, _, _, _, _, args) in work])
        for (b, d, hp, sl, o_ref, _), (o, s_new) in zip(work, done):
            o_ref[b, :, sl] = o
            s_sc[b, d, hp] = s_new
        return carry

    lax.fori_loop(0, s_sc.shape[0] // GDN_BATCH_UNROLL, body, 0)

    @pl.when(c == pl.num_programs(0) - 1)
    def _():
        sfin_ref[...] = s_sc[...]


def gdn_scan(qkvn, ba, s0, alog_row, dtb_row):
    b, l, _ = qkvn.shape
    nch = l // GDN_CHUNK
    blk = (b, GDN_CHUNK, GDN_WIDTH)
    fwd = lambda col: (lambda c: (0, c, col))
    bwd = lambda col: (lambda c: (0, nch - 1 - c, col))
    st = pl.BlockSpec(s0.shape, lambda c: (0, 0, 0, 0, 0))
    return pl.pallas_call(
        _gdn_scan_kernel,
        grid=(nch,),
        in_specs=[pl.BlockSpec(blk, fwd(0)), pl.BlockSpec(blk, fwd(1)), pl.BlockSpec(blk, fwd(2)),
                  pl.BlockSpec((b, GDN_CHUNK, LANES), fwd(0)),
                  pl.BlockSpec(blk, bwd(0)), pl.BlockSpec(blk, bwd(1)), pl.BlockSpec(blk, bwd(2)),
                  pl.BlockSpec((b, GDN_CHUNK, LANES), bwd(0)),
                  pl.BlockSpec((1, LANES), lambda c: (0, 0)), pl.BlockSpec((1, LANES), lambda c: (0, 0)), st],
        out_specs=[pl.BlockSpec(blk, fwd(0)), pl.BlockSpec(blk, bwd(0)), st],
        out_shape=[jax.ShapeDtypeStruct((b, l, GDN_WIDTH), F32)] * 2 + [jax.ShapeDtypeStruct(s0.shape, F32)],
        scratch_shapes=[pltpu.VMEM(s0.shape, F32)],
        compiler_params=_params("arbitrary"), name="gdn_scan",
    )(qkvn, qkvn, qkvn, ba, qkvn, qkvn, qkvn, ba, alog_row, dtb_row, s0)


OUTPROJ_TM = 512


def _outproj_kernel(x_ref, mod_ref, da_ref, ys_ref, u_ref, dsk_ref, wg_ref, bg_ref,
                    of_ref, or_ref, z_ref, gnw_ref, avg_ref, w_ref, o_ref):
    y = ys_ref[...] + u_ref[...] * dsk_ref[...]
    zz = 0.5 * y * (1.0 + jnp.tanh(math.sqrt(2.0 / math.pi) * (y + 0.044715 * (y * y * y))))
    glu = zz * _sigmoid(jnp.dot(zz.astype(BF16), wg_ref[...], preferred_element_type=F32) + bg_ref[...])
    o = of_ref[...] + or_ref[...]
    ms = jnp.dot(o * o, avg_ref[...], precision=HP, preferred_element_type=F32)
    z = z_ref[...]
    gd = o * lax.rsqrt(ms + EPS) * gnw_ref[...] * (z * _sigmoid(z))
    acc = jnp.dot(da_ref[...], w_ref[0:DA_WIDTH, :], preferred_element_type=F32)
    acc += jnp.dot(glu.astype(BF16), w_ref[DA_WIDTH:DA_WIDTH + S5_WIDTH, :], preferred_element_type=F32)
    acc += jnp.dot(gd.astype(BF16), w_ref[DA_WIDTH + S5_WIDTH:, :], preferred_element_type=F32)
    o_ref[...] = x_ref[...] + mod_ref[0, 2:3, :] * acc


def outproj(x, mod, rows_per_mod, da, ys, u, dsk, w_glu, b_glu, o_f, o_r, z, gnw, avg, w_out):
    t, d = x.shape
    tm = min(OUTPROJ_TM, t)
    tpm = rows_per_mod // tm
    row = lambda i: (i, 0)
    const = lambda i: (0, 0)
    sw = S5_WIDTH
    return pl.pallas_call(
        _outproj_kernel,
        grid=(t // tm,),
        in_specs=[pl.BlockSpec((tm, d), row),
                  pl.BlockSpec((1, 6, d), lambda i: (i // tpm, 0, 0)),
                  pl.BlockSpec((tm, DA_WIDTH), row),
                  pl.BlockSpec((tm, sw), row), pl.BlockSpec((tm, sw), row),
                  pl.BlockSpec((1, sw), const), pl.BlockSpec((sw, sw), const), pl.BlockSpec((1, sw), const),
                  pl.BlockSpec((tm, GDN_WIDTH), row), pl.BlockSpec((tm, GDN_WIDTH), row),
                  pl.BlockSpec((tm, GDN_WIDTH), row),
                  pl.BlockSpec((1, GDN_WIDTH), const), pl.BlockSpec((GDN_WIDTH, GDN_WIDTH), const),
                  pl.BlockSpec((D_MIX, d), const)],
        out_specs=pl.BlockSpec((tm, d), row),
        out_shape=jax.ShapeDtypeStruct((t, d), F32),
        compiler_params=_params("parallel"), name="outproj",
    )(x, mod, da, ys, u, dsk, w_glu, b_glu, o_f, o_r, z, gnw, avg, w_out)


MOE_TM = 1024
MOE_EP = 5


def _pick_lowest(cur, idx, sentinel, axis):
    m = jnp.max(cur, axis=axis, keepdims=True)
    first = jnp.min(jnp.where(cur == m, idx, sentinel), axis=axis, keepdims=True)
    return idx == first


def _route(logits_t, bias):
    tm = logits_t.shape[1]
    neg = jnp.float32(-jnp.inf)
    scores = jax.nn.sigmoid(logits_t)
    biased = scores + bias
    b3 = biased.reshape(N_GROUPS, GROUP_SIZE, tm)
    eidx = lax.broadcasted_iota(jnp.int32, b3.shape, 1)
    m1 = jnp.max(b3, axis=1, keepdims=True)
    p1 = _pick_lowest(b3, eidx, GROUP_SIZE, 1)
    m2 = jnp.max(jnp.where(p1, neg, b3), axis=1, keepdims=True)
    gs = (m1 + m2).reshape(N_GROUPS, tm)
    gidx = lax.broadcasted_iota(jnp.int32, gs.shape, 0)
    gsel = jnp.zeros(gs.shape, jnp.bool_)
    cur = gs
    for _ in range(TOPK_GROUPS):
        pick = _pick_lowest(cur, gidx, N_GROUPS, 0)
        gsel = jnp.logical_or(gsel, pick)
        cur = jnp.where(pick, neg, cur)
    emask = jnp.broadcast_to(gsel.reshape(N_GROUPS, 1, tm), b3.shape)
    cur = jnp.where(emask, b3, neg).reshape(N_EXPERTS, tm)
    ridx = lax.broadcasted_iota(jnp.int32, cur.shape, 0)
    sel = jnp.zeros(cur.shape, jnp.bool_)
    for _ in range(TOP_K):
        pick = _pick_lowest(cur, ridx, N_EXPERTS, 0)
        sel = jnp.logical_or(sel, pick)
        cur = jnp.where(pick, neg, cur)
    w = jnp.where(sel, scores, 0.0)
    return w / jnp.sum(w, axis=0, keepdims=True) * ROUTED_SCALE


def _moe_kernel(x_ref, mod_ref, nw_ref, wr_ref, rb_ref, w13_ref, w2_ref, fnw_ref, o_ref,
                h_sc, gate_sc, acc_sc, hid_sc, *, final_norm):
    e = pl.program_id(1)
    n_e = pl.num_programs(1)
    ep = w13_ref.shape[0]

    @pl.when(e == 0)
    def _():
        h = _rms_rows(x_ref[...]) * nw_ref[...] * (1.0 + mod_ref[0, 4:5, :]) + mod_ref[0, 3:4, :]
        h_sc[...] = h.astype(BF16)
        logits_t = lax.dot_general(wr_ref[...], h, NT_DIMS, precision=HP, preferred_element_type=F32)
        w = _route(logits_t, rb_ref[...])
        tm = w.shape[1]
        row = lax.broadcasted_iota(jnp.int32, (LANES - N_EXPERTS, tm), 0)
        shared = jnp.where(row == 0, 1.0, 0.0).astype(F32)
        gate_sc[...] = jnp.concatenate([w, shared], axis=0).T
        acc_sc[...] = jnp.zeros_like(acc_sc)

    h = h_sc[...]
    lane = lax.broadcasted_iota(jnp.int32, (1, LANES), 1)
    for j in range(ep):
        ab = jnp.dot(h, w13_ref[j], preferred_element_type=F32)
        a, b = ab[:, :MOE_FFN], ab[:, MOE_FFN:]
        g = jnp.sum(jnp.where(lane == e * ep + j, gate_sc[...], 0.0), axis=-1, keepdims=True)
        hid_sc[:, j * MOE_FFN:(j + 1) * MOE_FFN] = (a * _sigmoid(a) * b * g).astype(BF16)
    w2 = w2_ref[...].reshape(ep * MOE_FFN, w2_ref.shape[2])
    acc_sc[...] += jnp.dot(hid_sc[...], w2, preferred_element_type=F32)

    @pl.when(e == n_e - 1)
    def _():
        y = x_ref[...] + mod_ref[0, 5:6, :] * acc_sc[...]
        if final_norm:
            y = _rms_rows(y) * fnw_ref[...]
        o_ref[...] = y


def moe_sublayer(x, mod, rows_per_mod, norm_w, w_router_t, router_bias, w13, w2, final_w, final_norm):
    t, d = x.shape
    tm = min(MOE_TM, rows_per_mod)
    assert t % tm == 0 and rows_per_mod % tm == 0
    n_slots = w13.shape[0]
    ep = MOE_EP
    assert n_slots % ep == 0
    tiles_per_mod = rows_per_mod // tm
    return pl.pallas_call(
        functools.partial(_moe_kernel, final_norm=final_norm),
        grid=(t // tm, n_slots // ep),
        in_specs=[
            pl.BlockSpec((tm, d), lambda i, e: (i, 0)),
            pl.BlockSpec((1, 6, d), lambda i, e: (i // tiles_per_mod, 0, 0)),
            pl.BlockSpec((1, d), lambda i, e: (0, 0)),
            pl.BlockSpec((N_EXPERTS, d), lambda i, e: (0, 0)),
            pl.BlockSpec((N_EXPERTS, 1), lambda i, e: (0, 0)),
            pl.BlockSpec((ep, d, 2 * MOE_FFN), lambda i, e: (e, 0, 0)),
            pl.BlockSpec((ep, MOE_FFN, d), lambda i, e: (e, 0, 0)),
            pl.BlockSpec((1, d), lambda i, e: (0, 0)),
        ],
        out_specs=pl.BlockSpec((tm, d), lambda i, e: (i, 0)),
        out_shape=jax.ShapeDtypeStruct((t, d), F32),
        scratch_shapes=[
            pltpu.VMEM((tm, d), BF16),
            pltpu.VMEM((tm, LANES), F32),
            pltpu.VMEM((tm, d), F32),
            pltpu.VMEM((tm, ep * MOE_FFN), BF16),
        ],
        compiler_params=_params("parallel", "arbitrary"), name="moe_sublayer",
    )(x, mod, norm_w.reshape(1, d), w_router_t, router_bias.reshape(N_EXPERTS, 1), w13, w2, final_w.reshape(1, d))


def _moe_weights(w1, w3, w2, ws1, ws3, ws2):
    w13 = jnp.concatenate([jnp.concatenate([w1, w3], axis=-1),
                           jnp.concatenate([ws1, ws3], axis=-1)[None]], axis=0).astype(BF16)
    w2a = jnp.concatenate([w2, ws2[None]], axis=0).astype(BF16)
    return w13, w2a


def kernel(x, c, ctx, c_ctx, norm1_w, norm2_w, w_mod, b_mod, w_in, w_out, da_lambda, da_subln_w,
           s5_lam_re, s5_lam_im, s5_log_step, s5_b_re, s5_b_im, s5_c_re, s5_c_im, s5_d, s5_w_glu, s5_b_glu,
           gdn_conv_w, gdn_a_log, gdn_dt_bias, gdn_norm_w,
           moe_w_router, moe_router_bias, moe_w1, moe_w3, moe_w2, moe_ws1, moe_ws3, moe_ws2,
           final_norm_w):
    b, n, d = x.shape
    nc = ctx.shape[1]
    cos_t, sin_t = _rope_tables(n)
    c_act = jax.nn.silu(c)
    cc_act = jax.nn.silu(c_ctx)
    xl = x.reshape(b * n, d)
    xc = ctx.reshape(b * nc, d)
    head_avg = jnp.kron(jnp.eye(GDN_HEADS, dtype=F32), jnp.full((GDN_DV, GDN_DV), 1.0 / GDN_DV, F32))
    s_zero = jnp.zeros((b, 2, GDN_HEADS // 2, LANES, LANES), F32)
    for i in range(DEPTH):
        ctx_out = i < DEPTH - 1
        last = i == DEPTH - 1
        lam_init = 0.8 - 0.6 * math.exp(-0.3 * i)
        mod = (c_act @ w_mod[i] + b_mod[i]).reshape(b, 6, d)
        modc = (cc_act @ w_mod[i] + b_mod[i]).reshape(1, 6, d)

        w_in_p = jnp.pad(w_in[i], ((0, 0), (0, IN_PAD - IN_WIDTH))).astype(BF16)
        q, k, v, u, gq, z, ba = inproj(xl, mod, n, norm1_w[i], w_in_p, cos_t, sin_t, rope=True)
        qc, kc, vc, uc, gqc, zc, bac = inproj(xc, modc, b * nc, norm1_w[i], w_in_p, cos_t, sin_t, rope=False)

        lq1, lk1, lq2, lk2 = da_lambda[i].astype(F32)
        lam = jnp.exp(jnp.sum(lq1 * lk1)) - jnp.exp(jnp.sum(lq2 * lk2)) + lam_init
        lam_row = jnp.full((1, LANES), lam, F32)
        da = diff_attention(q, [kc, k], [vc, v], (nc, n), n, lam_row, da_subln_w[i], lam_init)

        tables = _s5_tables(s5_lam_re[i], s5_lam_im[i], s5_log_step[i], s5_b_re[i], s5_b_im[i],
                            s5_c_re[i], s5_c_im[i])
        seq = jnp.concatenate([uc.reshape(b, nc, S5_WIDTH), u.reshape(b, n, S5_WIDTH)], axis=1)
        ys = s5_scan(seq, tables, nc)

        alog_row = jnp.zeros((1, LANES), F32).at[0, 8:16].set(gdn_a_log[i].astype(F32).reshape(-1))
        dtb_row = jnp.zeros((1, LANES), F32).at[0, 8:16].set(gdn_dt_bias[i].astype(F32).reshape(-1))
        gn = gdn_prep(gq, gdn_conv_w[i], n)
        gnc = gdn_prep(gqc, gdn_conv_w[i], nc)
        ofc, orc, s_ctx = gdn_scan(gnc.reshape(b, nc, GDN_QKV_W), bac.reshape(b, nc, LANES), s_zero, alog_row, dtb_row)
        of, orv, _ = gdn_scan(gn.reshape(b, n, GDN_QKV_W), ba.reshape(b, n, LANES), s_ctx, alog_row, dtb_row)

        w_out_b = w_out[i].astype(BF16)
        dsk = s5_d[i].astype(F32).reshape(1, S5_WIDTH)
        wg = s5_w_glu[i].astype(BF16)
        bg = s5_b_glu[i].astype(F32).reshape(1, S5_WIDTH)
        gnw = jnp.tile(gdn_norm_w[i].astype(F32), GDN_HEADS).reshape(1, GDN_WIDTH)
        xl = outproj(xl, mod, n, da, ys[:, nc:].reshape(b * n, S5_WIDTH), u, dsk, wg, bg,
                     of.reshape(b * n, GDN_WIDTH), orv.reshape(b * n, GDN_WIDTH), z, gnw, head_avg, w_out_b)
        if ctx_out:
            dac = diff_attention(qc, [kc], [vc], (nc,), nc, lam_row, da_subln_w[i], lam_init)
            xc = outproj(xc, modc, b * nc, dac, ys[:, :nc].reshape(b * nc, S5_WIDTH), uc, dsk, wg, bg,
                         ofc.reshape(b * nc, GDN_WIDTH), orc.reshape(b * nc, GDN_WIDTH), zc, gnw, head_avg, w_out_b)

        w13, w2a = _moe_weights(moe_w1[i], moe_w3[i], moe_w2[i], moe_ws1[i], moe_ws3[i], moe_ws2[i])
        wr_t = moe_w_router[i].T
        xl = moe_sublayer(xl, mod, n, norm2_w[i], wr_t, moe_router_bias[i], w13, w2a, final_norm_w, last)
        if ctx_out:
            xc = moe_sublayer(xc, modc, b * nc, norm2_w[i], wr_t, moe_router_bias[i], w13, w2a, final_norm_w, False)
    return xl.reshape(b, n, d)
```

```python
import functools
import math

import jax
import jax.numpy as jnp
import numpy as np
from jax import lax
from jax.experimental import pallas as pl
from jax.experimental.pallas import tpu as pltpu

F32 = jnp.float32
BF16 = jnp.bfloat16

D_MODEL = 1024
DEPTH = 2
GRID_W = 64
EPS = 1e-6

DA_HEADS = 4
DA_HEAD_DIM = D_MODEL // 16
DA_V_DIM = 2 * DA_HEAD_DIM
DA_WIDTH = DA_HEADS * DA_V_DIM
ROPE_THETA = 10000.0

S5_WIDTH = D_MODEL // 4
S5_GROUP = 16
S5_GROUPS = S5_WIDTH // S5_GROUP
S5_STATE = 64

GDN_HEADS = 4
GDN_DK = D_MODEL // 16
GDN_DV = D_MODEL // 16
GDN_WIDTH = GDN_HEADS * GDN_DV
GDN_CONV = 5
GDN_CHUNK = 64

D_MIX = DA_WIDTH + S5_WIDTH + GDN_WIDTH
DA_QK_W = 2 * DA_HEADS * DA_HEAD_DIM
GDN_QKV_W = 2 * GDN_HEADS * GDN_DK + GDN_HEADS * GDN_DV
IN_SIZES = (DA_QK_W, DA_QK_W, DA_WIDTH, S5_WIDTH, GDN_QKV_W, GDN_WIDTH, 2 * GDN_HEADS, 2 * GDN_HEADS)
IN_WIDTH = sum(IN_SIZES)

N_EXPERTS = 64
TOP_K = 8
N_GROUPS = 8
GROUP_SIZE = N_EXPERTS // N_GROUPS
TOPK_GROUPS = 4
MOE_FFN = D_MODEL // 4
ROUTED_SCALE = 2.5

LANES = 128
VMEM_LIMIT_BYTES = 56 * 1024 * 1024

HP = lax.Precision.HIGHEST
NT_DIMS = (((1,), (1,)), ((), ()))


def _params(*sem):
    return pltpu.CompilerParams(dimension_semantics=sem, vmem_limit_bytes=VMEM_LIMIT_BYTES)


def _sigmoid(x):
    return 0.5 * (1.0 + jnp.tanh(0.5 * x))


def _rms_rows(x):
    return x * lax.rsqrt(jnp.mean(x * x, axis=-1, keepdims=True) + EPS)


MOD_TN = 768


def _mod_kernel(c_ref, w_ref, b_ref, o_ref):
    c = c_ref[...]
    act = c * _sigmoid(c)
    o_ref[...] = jnp.dot(act, w_ref[...], precision=HP, preferred_element_type=F32) + b_ref[...]


def mod_proj(cond, w_mod, b_mod, layer):
    r, d = cond.shape
    depth, _, n = w_mod.shape
    return pl.pallas_call(
        _mod_kernel,
        grid=(n // MOD_TN,),
        in_specs=[pl.BlockSpec((r, d), lambda j: (0, 0)),
                  pl.BlockSpec((None, d, MOD_TN), lambda j: (layer, 0, j)),
                  pl.BlockSpec((None, 1, MOD_TN), lambda j: (layer, 0, j))],
        out_specs=pl.BlockSpec((r, MOD_TN), lambda j: (0, j)),
        out_shape=jax.ShapeDtypeStruct((r, n), F32),
        compiler_params=_params("parallel"), name="mod_proj",
    )(cond, w_mod, b_mod.reshape(depth, 1, n))


IN_PAD = 2944
INPROJ_TM = 512
O_Q, O_K, O_V, O_U, O_G, O_Z, O_BA = 0, 512, 1024, 1536, 1792, 2560, 2816


def _rope_apply(x, cos, sin):
    lane = lax.broadcasted_iota(jnp.int32, x.shape, 1)
    up = pltpu.roll(x, LANES - 16, 1)
    dn = pltpu.roll(x, 16, 1)
    partner = jnp.where((lane & 31) < 16, up, dn)
    return x * cos + partner * sin


def _inproj_kernel(x_ref, mod_ref, nw_ref, w_ref, cos_ref, sin_ref,
                   q_ref, k_ref, v_ref, u_ref, g_ref, z_ref, ba_ref, *, rope):
    h = (_rms_rows(x_ref[...]) * nw_ref[...] * (1.0 + mod_ref[0, 1:2, :]) + mod_ref[0, 0:1, :]).astype(BF16)

    def proj(lo, hi):
        return jnp.dot(h, w_ref[:, lo:hi], preferred_element_type=F32)

    scale = DA_HEAD_DIM ** -0.5 * math.log2(math.e)
    for hd in range(DA_HEADS):
        lo = hd * LANES
        qs = proj(O_Q + lo, O_Q + lo + LANES)
        ks = proj(O_K + lo, O_K + lo + LANES)
        if rope:
            qs = _rope_apply(qs, cos_ref[...], sin_ref[...])
            ks = _rope_apply(ks, cos_ref[...], sin_ref[...])
        q_ref[:, lo:lo + LANES] = (qs * scale).astype(BF16)
        k_ref[:, lo:lo + LANES] = ks.astype(BF16)
    v_ref[...] = proj(O_V, O_U).astype(BF16)
    u_ref[...] = proj(O_U, O_G)
    g_ref[...] = proj(O_G, O_Z)
    z_ref[...] = proj(O_Z, O_BA)
    ba_ref[...] = proj(O_BA, IN_PAD)


def inproj(x, mod, rows_per_mod, norm_w, w_pad, cos_t, sin_t, rope):
    t, d = x.shape
    tm = min(INPROJ_TM, t)
    tpm = rows_per_mod // tm
    npos = cos_t.shape[0] // tm
    row = lambda i: (i, 0)
    widths = (DA_QK_W, DA_QK_W, DA_WIDTH, S5_WIDTH, GDN_QKV_W, GDN_WIDTH, LANES)
    dtypes = (BF16, BF16, BF16, F32, F32, F32, F32)
    return pl.pallas_call(
        functools.partial(_inproj_kernel, rope=rope),
        grid=(t // tm,),
        in_specs=[pl.BlockSpec((tm, d), row),
                  pl.BlockSpec((1, 6, d), lambda i: (i // tpm, 0, 0)),
                  pl.BlockSpec((1, d), lambda i: (0, 0)),
                  pl.BlockSpec((d, IN_PAD), lambda i: (0, 0)),
                  pl.BlockSpec((tm, LANES), lambda i: (i % npos, 0)),
                  pl.BlockSpec((tm, LANES), lambda i: (i % npos, 0))],
        out_specs=[pl.BlockSpec((tm, w), row) for w in widths],
        out_shape=[jax.ShapeDtypeStruct((t, w), dt) for w, dt in zip(widths, dtypes)],
        compiler_params=_params("parallel"), name="inproj",
    )(x, mod, norm_w.reshape(1, d), w_pad, cos_t, sin_t)


def _rope_tables(n):
    nf = DA_HEAD_DIM // 4
    t = jnp.arange(n, dtype=jnp.int32)
    inv = ROPE_THETA ** (-jnp.arange(nf, dtype=F32) / nf)
    ang_r = (t // GRID_W).astype(F32)[:, None] * inv
    ang_c = (t % GRID_W).astype(F32)[:, None] * inv
    cos64 = jnp.concatenate([jnp.cos(ang_r), jnp.cos(ang_r), jnp.cos(ang_c), jnp.cos(ang_c)], axis=-1)
    sin64 = jnp.concatenate([-jnp.sin(ang_r), jnp.sin(ang_r), -jnp.sin(ang_c), jnp.sin(ang_c)], axis=-1)
    return jnp.tile(cos64, (1, 2)), jnp.tile(sin64, (1, 2))


ATTN_TQ = 512
ATTN_KC = 1024


def _attn_kernel(*refs, n_kv, kv_rows, chunks, lam_init):
    q_ref = refs[0]
    k_refs = refs[1:1 + n_kv]
    v_refs = refs[1 + n_kv:1 + 2 * n_kv]
    lam_ref, w_ref, o_ref, v1_sc = refs[1 + 2 * n_kv:]

    @pl.when(pl.program_id(2) == 0)
    def _():
        off = 0
        for ki, rows in enumerate(kv_rows):
            v1_sc[off:off + rows, :DA_V_DIM] = v_refs[ki][...]
            v1_sc[off:off + rows, DA_V_DIM:] = jnp.ones((rows, DA_V_DIM), BF16)
            off += rows

    q = q_ref[...]
    tq = q.shape[0]
    lane = lax.broadcasted_iota(jnp.int32, q.shape, 1)
    zero = jnp.zeros_like(q)
    qq = jnp.concatenate([jnp.where(lane < DA_HEAD_DIM, q, zero), jnp.where(lane >= DA_HEAD_DIM, q, zero)], axis=0)
    m = jnp.full((2 * tq, 1), -jnp.inf, F32)
    acc = jnp.zeros((2 * tq, 2 * DA_V_DIM), F32)
    def scores(chunk):
        ki, start, _, size = chunk
        return lax.dot_general(qq, k_refs[ki][start:start + size, :], NT_DIMS, preferred_element_type=F32)

    s_next = scores(chunks[0])
    for ci, (ki, start, off, size) in enumerate(chunks):
        s = s_next
        if ci + 1 < len(chunks):
            s_next = scores(chunks[ci + 1])
        m_new = jnp.maximum(m, jnp.max(s, axis=-1, keepdims=True))
        p = jnp.exp2(s - m_new).astype(BF16)
        acc = jnp.exp2(m - m_new) * acc + jnp.dot(p, v1_sc[off:off + size, :], preferred_element_type=F32)
        m = m_new
    o = acc[:, :DA_V_DIM] / acc[:, DA_V_DIM:]
    od = o[:tq] - lam_ref[...] * o[tq:]
    o_ref[...] = (_rms_rows(od) * w_ref[...] * (1.0 - lam_init)).astype(o_ref.dtype)


def diff_attention(q, ks, vs, kv_rows, q_rows, lam_row, subln_w, lam_init):
    t = q.shape[0]
    b = t // q_rows
    tq = min(ATTN_TQ, q_rows)
    nq = q_rows // tq
    chunks, off = [], 0
    for ki, rows in enumerate(kv_rows):
        kc = min(ATTN_KC, rows)
        chunks += [(ki, s, off + s, kc) for s in range(0, rows, kc)]
        off += rows
    qmap = lambda bi, h, qi: (bi * nq + qi, h)
    kvmap = lambda bi, h, qi: (bi, h)
    const = lambda bi, h, qi: (0, 0)
    return pl.pallas_call(
        functools.partial(_attn_kernel, n_kv=len(ks), kv_rows=tuple(kv_rows), chunks=tuple(chunks),
                          lam_init=lam_init),
        grid=(b, DA_HEADS, nq),
        in_specs=([pl.BlockSpec((tq, LANES), qmap)]
                  + [pl.BlockSpec((rows, LANES), kvmap) for rows in kv_rows] * 2
                  + [pl.BlockSpec((1, LANES), const)] * 2),
        out_specs=pl.BlockSpec((tq, LANES), qmap),
        out_shape=jax.ShapeDtypeStruct((t, DA_WIDTH), BF16),
        scratch_shapes=[pltpu.VMEM((off, 2 * DA_V_DIM), BF16)],
        compiler_params=_params("parallel", "parallel", "arbitrary"), name="diff_attention",
    )(q, *ks, *vs, lam_row, subln_w.reshape(1, LANES))


S5_LC = 64
S5_CW = S5_LC * S5_GROUP
S5_SW = 2 * S5_STATE


def _s5_discretize(lam_re, lam_im, log_step, b_re, b_im):
    lr, li = lam_re.astype(F32), lam_im.astype(F32)
    step = jnp.exp(log_step.astype(F32))[:, None]
    mag = jnp.exp(lr * step)
    ab_re, ab_im = mag * jnp.cos(li * step), mag * jnp.sin(li * step)
    den = lr * lr + li * li
    nr, ni = ab_re - 1.0, ab_im
    f_re = (nr * lr + ni * li) / den
    f_im = (ni * lr - nr * li) / den
    br, bi = b_re.astype(F32), b_im.astype(F32)
    bb_re = f_re[..., None] * br - f_im[..., None] * bi
    bb_im = f_re[..., None] * bi + f_im[..., None] * br
    return bb_re, bb_im


def _s5_tables(lam_re, lam_im, log_step, b_re, b_im, c_re, c_im):
    lc, g = S5_LC, S5_GROUPS
    bm, cm, mm, k1, k2 = [], [], [], [], []
    for d in range(2):
        lr, li = lam_re[d].astype(F32), lam_im[d].astype(F32)
        step = jnp.exp(log_step[d].astype(F32))[:, None]
        bb_re, bb_im = _s5_discretize(lam_re[d], lam_im[d], log_step[d], b_re[d], b_im[d])
        tau = jnp.arange(lc + 1, dtype=F32)[:, None, None]
        mag = jnp.exp(tau * (lr * step))
        ang = tau * (li * step)
        pr, pi = mag * jnp.cos(ang), mag * jnp.sin(ang)
        abr = pr[..., None] * bb_re - pi[..., None] * bb_im
        abi = pr[..., None] * bb_im + pi[..., None] * bb_re
        cr, ci = c_re[d].astype(F32), c_im[d].astype(F32)
        kern = (jnp.einsum('gkp,tgph->tgkh', cr, abr[:lc], precision=HP)
                - jnp.einsum('gkp,tgph->tgkh', ci, abi[:lc], precision=HP))
        rank = np.arange(lc) if d == 0 else lc - 1 - np.arange(lc)
        e_in = lc - 1 - rank
        bmat = jnp.concatenate([abr[e_in].transpose(1, 0, 3, 2), abi[e_in].transpose(1, 0, 3, 2)], axis=-1)
        bm.append(bmat.reshape(g, S5_CW, S5_SW))
        e_out = rank + 1
        pro, pio = pr[e_out][:, :, None, :], pi[e_out][:, :, None, :]
        car = cr[None] * pro - ci[None] * pio
        cai = cr[None] * pio + ci[None] * pro
        cmat = jnp.concatenate([car.transpose(1, 3, 0, 2), -cai.transpose(1, 3, 0, 2)], axis=1)
        cm.append(cmat.reshape(g, S5_SW, S5_CW))
        kt = kern.transpose(1, 3, 0, 2).astype(BF16)
        zpad = jnp.zeros((g, S5_GROUP, lc - 1, S5_GROUP), BF16)
        strip = jnp.concatenate([zpad, kt] if d == 0 else [kt[:, :, ::-1], zpad], axis=2)
        strip = strip.reshape(g, S5_GROUP, (2 * lc - 1) * S5_GROUP)
        rows = [strip[:, :, (lc - 1 - s) * S5_GROUP:(2 * lc - 1 - s) * S5_GROUP] for s in range(lc)]
        mm.append(jnp.stack(rows, axis=1).reshape(g, S5_CW, S5_CW))
        alr, ali = pr[lc], pi[lc]
        k1.append(jnp.repeat(jnp.concatenate([alr, alr], axis=-1), 8, axis=0))
        k2.append(jnp.repeat(jnp.concatenate([-ali, ali], axis=-1), 8, axis=0))
    return (jnp.stack(bm).astype(BF16), jnp.stack(cm).astype(BF16), jnp.stack(mm).astype(BF16),
            jnp.stack(k1), jnp.stack(k2))


def _s5_local_kernel(u_ref, bm_ref, s_ref):
    s = jnp.dot(u_ref[0], bm_ref[0, 0], preferred_element_type=F32)
    s_ref[0] = s.reshape(s_ref.shape[1:])


def _s5_carry_kernel(s_ref, k1_ref, k2_ref, xin_ref, *, n_ctx, n_chunks):
    d = pl.program_id(0)
    k1, k2 = k1_ref[0], k2_ref[0]

    def body(k, x):
        rev = jnp.where(k < n_ctx, n_ctx - 1 - k, n_chunks + n_ctx - 1 - k)
        c = jnp.where(d == 0, k, rev)
        xin_ref[0, c] = x
        return k1 * x + k2 * pltpu.roll(x, S5_STATE, 1) + s_ref[0, c]

    lax.fori_loop(0, n_chunks, body, jnp.zeros(k1.shape, F32))


def _s5_out_kernel(u_ref, m_ref, xin_ref, cm_ref, y_ref):
    u = u_ref[0]
    acc = jnp.dot(u, m_ref[0, 0], preferred_element_type=F32)
    acc += jnp.dot(u, m_ref[1, 0], preferred_element_type=F32)
    for d in range(2):
        xin = xin_ref[d].reshape(u.shape[0], S5_SW).astype(BF16)
        acc += jnp.dot(xin, cm_ref[d, 0], preferred_element_type=F32)
    y_ref[0] = acc


def s5_scan(u_ctx, u_lat, b, tables):
    bmat, cmat, mmat, k1, k2 = tables
    g, lc = S5_GROUPS, S5_LC
    n_ctx, n_lat = u_ctx.shape[0] // b, u_lat.shape[0] // b
    assert b == 8 and n_lat % lc == 0 and n_ctx % lc == 0
    nch = (n_ctx + n_lat) // lc
    r = nch * b

    def to_groups(x, rows):
        return x.reshape(b, rows // lc, lc, g, S5_GROUP).transpose(3, 1, 0, 2, 4).astype(BF16)

    def from_groups(y, rows):
        return y.transpose(2, 1, 3, 0, 4).reshape(b * rows, S5_WIDTH)

    u = jnp.concatenate([to_groups(u_ctx, n_ctx), to_groups(u_lat, n_lat)], axis=1).reshape(g, r, S5_CW)
    s = pl.pallas_call(
        _s5_local_kernel,
        grid=(2, g),
        in_specs=[pl.BlockSpec((1, r, S5_CW), lambda d, gi: (gi, 0, 0)),
                  pl.BlockSpec((1, 1, S5_CW, S5_SW), lambda d, gi: (d, gi, 0, 0))],
        out_specs=pl.BlockSpec((1, nch, b, S5_SW), lambda d, gi: (d, 0, gi, 0)),
        out_shape=jax.ShapeDtypeStruct((2, nch, g * b, S5_SW), F32),
        compiler_params=_params("arbitrary", "arbitrary"), name="s5_local",
    )(u, bmat)
    xin = pl.pallas_call(
        functools.partial(_s5_carry_kernel, n_ctx=n_ctx // lc, n_chunks=nch),
        grid=(2,),
        in_specs=[pl.BlockSpec((1, nch, g * b, S5_SW), lambda d: (d, 0, 0, 0)),
                  pl.BlockSpec((1, g * b, S5_SW), lambda d: (d, 0, 0)),
                  pl.BlockSpec((1, g * b, S5_SW), lambda d: (d, 0, 0))],
        out_specs=pl.BlockSpec((1, nch, g * b, S5_SW), lambda d: (d, 0, 0, 0)),
        out_shape=jax.ShapeDtypeStruct((2, nch, g * b, S5_SW), F32),
        compiler_params=_params("arbitrary"), name="s5_carry",
    )(s, k1, k2)
    y = pl.pallas_call(
        _s5_out_kernel,
        grid=(g,),
        in_specs=[pl.BlockSpec((1, r, S5_CW), lambda gi: (gi, 0, 0)),
                  pl.BlockSpec((2, 1, S5_CW, S5_CW), lambda gi: (0, gi, 0, 0)),
                  pl.BlockSpec((2, nch, b, S5_SW), lambda gi: (0, 0, gi, 0)),
                  pl.BlockSpec((2, 1, S5_SW, S5_CW), lambda gi: (0, gi, 0, 0))],
        out_specs=pl.BlockSpec((1, r, S5_CW), lambda gi: (gi, 0, 0)),
        out_shape=jax.ShapeDtypeStruct((g, r, S5_CW), F32),
        compiler_params=_params("arbitrary"), name="s5_out",
    )(u, mmat, xin, cmat)
    y = y.reshape(g, nch, b, lc, S5_GROUP)
    return from_groups(y[:, :n_ctx // lc], n_ctx), from_groups(y[:, n_ctx // lc:], n_lat)


def _gdn_prep_kernel(x_ref, w_ref, o_ref):
    j = pl.program_id(1)
    x = x_ref[...]
    n = x.shape[0]
    row = lax.broadcasted_iota(jnp.int32, x.shape, 0)
    half = GDN_CONV // 2
    acc = x * w_ref[half:half + 1, :]
    for sh in range(1, half + 1):
        acc += jnp.where(row >= sh, pltpu.roll(x, sh, 0), 0.0) * w_ref[half - sh:half - sh + 1, :]
        acc += jnp.where(row < n - sh, pltpu.roll(x, n - sh, 0), 0.0) * w_ref[half + sh:half + sh + 1, :]
    a = acc * _sigmoid(acc)
    lane = lax.broadcasted_iota(jnp.int32, x.shape, 1)
    lo = lane < GDN_DK
    sq = a * a
    s_lo = jnp.sum(jnp.where(lo, sq, 0.0), axis=-1, keepdims=True)
    s_hi = jnp.sum(jnp.where(lo, 0.0, sq), axis=-1, keepdims=True)
    nrm = a * lax.rsqrt(jnp.where(lo, s_lo, s_hi) + EPS)
    q_blocks = GDN_HEADS * GDN_DK // LANES
    nrm = nrm * jnp.where(j < q_blocks, GDN_DK ** -0.5, 1.0)
    o_ref[...] = jnp.where(j < 2 * q_blocks, nrm, a)


def gdn_prep(qkv, conv_w, seg):
    t, w = qkv.shape
    return pl.pallas_call(
        _gdn_prep_kernel,
        grid=(t // seg, w // LANES),
        in_specs=[pl.BlockSpec((seg, LANES), lambda s, j: (s, j)),
                  pl.BlockSpec((GDN_CONV, LANES), lambda s, j: (0, j))],
        out_specs=pl.BlockSpec((seg, LANES), lambda s, j: (s, j)),
        out_shape=jax.ShapeDtypeStruct((t, w), F32),
        compiler_params=_params("parallel", "arbitrary"), name="gdn_prep",
    )(qkv, conv_w)


def _bd(x):
    x2 = jnp.concatenate([x, x], axis=0)
    r = lax.broadcasted_iota(jnp.int32, x2.shape, 0)
    l = lax.broadcasted_iota(jnp.int32, x2.shape, 1)
    return jnp.where((r >> 6) == (l >> 6), x2, jnp.zeros_like(x2))


def _mm(a, b):
    return jnp.dot(a.astype(BF16), b.astype(BF16), preferred_element_type=F32)


def _gdn_chunks(insts):
    c = GDN_CHUNK
    n = len(insts)
    every = range(n)
    q, k, v, beta, gcol, s_bd, rev = (list(t) for t in zip(*insts))
    i = lax.broadcasted_iota(jnp.int32, (c, LANES), 0)
    j = lax.broadcasted_iota(jnp.int32, (c, LANES), 1) & (c - 1)
    ti = lax.broadcasted_iota(jnp.int32, (c, c), 0)
    tj = lax.broadcasted_iota(jnp.int32, (c, c), 1)
    causal = [i <= j if r else i >= j for r in rev]
    strict = [i < j if r else i > j for r in rev]
    upto = [i >= j if r else i <= j for r in rev]
    lmat = [(tj >= ti if r else tj <= ti).astype(F32) for r in rev]
    ones = jnp.ones((c, c), F32)
    eye = jnp.where(i == j, 1.0, 0.0)
    lg = [jnp.dot(lmat[t], gcol[t], precision=HP, preferred_element_type=F32) for t in every]
    rg = [jnp.dot(ones, jnp.where(upto[t], gcol[t], 0.0), precision=HP, preferred_element_type=F32) for t in every]
    decay = [jnp.where(causal[t], jnp.exp(jnp.where(causal[t], lg[t] - rg[t], 0.0)), 0.0) for t in every]
    kb = [k[t] * beta[t] for t in every]
    k_bd = [_bd(k[t].astype(BF16)) for t in every]
    a = [jnp.where(strict[t], lax.dot_general(kb[t].astype(BF16), k_bd[t], NT_DIMS, preferred_element_type=F32)
                   * decay[t], 0.0) for t in every]
    qk = [lax.dot_general(q[t].astype(BF16), k_bd[t], NT_DIMS, preferred_element_type=F32) * decay[t] for t in every]
    eg = [jnp.exp(lg[t]) for t in every]
    same = [(i >> sh) == (j >> sh) for sh in (3, 4, 5)]
    x = [jnp.where(same[0], a[t], 0.0) for t in every]
    p = [eye - x[t] for t in every]
    for _ in range(2):
        x = [_mm(x[t], _bd(x[t])) for t in every]
        p = [p[t] + _mm(p[t], _bd(x[t])) for t in every]
    for lvl in range(3):
        inner = same[lvl]
        join = jnp.logical_not(inner) if lvl == 2 else jnp.logical_and(same[lvl + 1], jnp.logical_not(inner))
        tl = [_mm(p[t], _bd(jnp.where(join, a[t], 0.0))) for t in every]
        p = [p[t] - _mm(tl[t], _bd(p[t])) for t in every]
    u = [_mm(p[t], _bd(v[t] * beta[t])) for t in every]
    w = [_mm(p[t], _bd(kb[t] * eg[t])) for t in every]
    v_new = [u[t] - _mm(w[t], s_bd[t]) for t in every]
    o_state = [_mm(q[t] * eg[t], s_bd[t]) for t in every]
    o = [o_state[t] + _mm(qk[t], _bd(v_new[t])) for t in every]
    g_last = [lg[t][0:1, :] if rev[t] else lg[t][c - 1:c, :] for t in every]
    k_dec = [k[t] * jnp.exp(g_last[t] - lg[t]) for t in every]
    upd = [jnp.dot(k_dec[t].T.astype(BF16), v_new[t].astype(BF16), preferred_element_type=F32) for t in every]
    r2 = lax.broadcasted_iota(jnp.int32, (LANES, LANES), 0)
    l2 = lax.broadcasted_iota(jnp.int32, (LANES, LANES), 1)
    diag = (r2 >> 6) == (l2 >> 6)
    s_new = [s_bd[t] * jnp.exp(g_last[t]) + jnp.where(diag, upd[t], 0.0) for t in every]
    return list(zip(o, s_new))


GDN_BATCH_UNROLL = 4


def _gdn_scan_kernel(qf, kf, vf, baf, qr, kr, vr, bar, alog_ref, dtb_ref, s0_ref,
                     of_ref, or_ref, sfin_ref, s_sc):
    c = pl.program_id(0)

    @pl.when(c == 0)
    def _():
        s_sc[...] = s0_ref[...]

    lane = lax.broadcasted_iota(jnp.int32, (GDN_CHUNK, LANES), 1)
    first = lane < GDN_DK
    ins = ((qf, kf, vf, baf, of_ref), (qr, kr, vr, bar, or_ref))

    def body(it, carry):
        work = []
        for bb in range(GDN_BATCH_UNROLL):
            b = it * GDN_BATCH_UNROLL + bb
            for d in range(2):
                q_ref, k_ref, v_ref, ba_ref, o_ref = ins[d]
                ba = ba_ref[b]
                bsig = _sigmoid(ba)
                sp = ba + dtb_ref[...]
                gall = -jnp.exp(alog_ref[...]) * (jnp.maximum(sp, 0.0) + jnp.log(1.0 + jnp.exp(-jnp.abs(sp))))
                for hp in range(GDN_HEADS // 2):
                    col = d * GDN_HEADS + 2 * hp
                    beta = jnp.where(first, bsig[:, col:col + 1], bsig[:, col + 1:col + 2])
                    gcol = jnp.where(first, gall[:, 8 + col:9 + col], gall[:, 9 + col:10 + col])
                    sl = slice(hp * LANES, (hp + 1) * LANES)
                    work.append((b, d, hp, sl, o_ref, (q_ref[b, :, sl], k_ref[b, :, sl], v_ref[b, :, sl],
                                                       beta, gcol, s_sc[b, d, hp], d == 1)))
        done = _gdn_chunks([args for (_, _, _, _, _, args) in work])
        for (b, d, hp, sl, o_ref, _), (o, s_new) in zip(work, done):
            o_ref[b, :, sl] = o
            s_sc[b, d, hp] = s_new
        return carry

    lax.fori_loop(0, s_sc.shape[0] // GDN_BATCH_UNROLL, body, 0)

    @pl.when(c == pl.num_programs(0) - 1)
    def _():
        sfin_ref[...] = s_sc[...]


def gdn_scan(qkvn, ba, s0, alog_row, dtb_row):
    b, l, _ = qkvn.shape
    nch = l // GDN_CHUNK
    blk = (b, GDN_CHUNK, GDN_WIDTH)
    fwd = lambda col: (lambda c: (0, c, col))
    bwd = lambda col: (lambda c: (0, nch - 1 - c, col))
    st = pl.BlockSpec(s0.shape, lambda c: (0, 0, 0, 0, 0))
    return pl.pallas_call(
        _gdn_scan_kernel,
        grid=(nch,),
        in_specs=[pl.BlockSpec(blk, fwd(0)), pl.BlockSpec(blk, fwd(1)), pl.BlockSpec(blk, fwd(2)),
                  pl.BlockSpec((b, GDN_CHUNK, LANES), fwd(0)),
                  pl.BlockSpec(blk, bwd(0)), pl.BlockSpec(blk, bwd(1)), pl.BlockSpec(blk, bwd(2)),
                  pl.BlockSpec((b, GDN_CHUNK, LANES), bwd(0)),
                  pl.BlockSpec((1, LANES), lambda c: (0, 0)), pl.BlockSpec((1, LANES), lambda c: (0, 0)), st],
        out_specs=[pl.BlockSpec(blk, fwd(0)), pl.BlockSpec(blk, bwd(0)), st],
        out_shape=[jax.ShapeDtypeStruct((b, l, GDN_WIDTH), F32)] * 2 + [jax.ShapeDtypeStruct(s0.shape, F32)],
        scratch_shapes=[pltpu.VMEM(s0.shape, F32)],
        compiler_params=_params("arbitrary"), name="gdn_scan",
    )(qkvn, qkvn, qkvn, ba, qkvn, qkvn, qkvn, ba, alog_row, dtb_row, s0)


OUTPROJ_TM = 512


def _outproj_kernel(x_ref, mod_ref, da_ref, ys_ref, u_ref, dsk_ref, wg_ref, bg_ref,
                    of_ref, or_ref, z_ref, gnw_ref, avg_ref, w_ref, o_ref):
    y = ys_ref[...] + u_ref[...] * dsk_ref[...]
    zz = 0.5 * y * (1.0 + jnp.tanh(math.sqrt(2.0 / math.pi) * (y + 0.044715 * (y * y * y))))
    glu = zz * _sigmoid(jnp.dot(zz.astype(BF16), wg_ref[...], preferred_element_type=F32) + bg_ref[...])
    o = of_ref[...] + or_ref[...]
    ms = jnp.dot(o * o, avg_ref[...], precision=HP, preferred_element_type=F32)
    z = z_ref[...]
    gd = o * lax.rsqrt(ms + EPS) * gnw_ref[...] * (z * _sigmoid(z))
    acc = jnp.dot(da_ref[...], w_ref[0:DA_WIDTH, :], preferred_element_type=F32)
    acc += jnp.dot(glu.astype(BF16), w_ref[DA_WIDTH:DA_WIDTH + S5_WIDTH, :], preferred_element_type=F32)
    acc += jnp.dot(gd.astype(BF16), w_ref[DA_WIDTH + S5_WIDTH:, :], preferred_element_type=F32)
    o_ref[...] = x_ref[...] + mod_ref[0, 2:3, :] * acc


def outproj(x, mod, rows_per_mod, da, ys, u, dsk, w_glu, b_glu, o_f, o_r, z, gnw, avg, w_out):
    t, d = x.shape
    tm = min(OUTPROJ_TM, t)
    tpm = rows_per_mod // tm
    row = lambda i: (i, 0)
    const = lambda i: (0, 0)
    sw = S5_WIDTH
    return pl.pallas_call(
        _outproj_kernel,
        grid=(t // tm,),
        in_specs=[pl.BlockSpec((tm, d), row),
                  pl.BlockSpec((1, 6, d), lambda i: (i // tpm, 0, 0)),
                  pl.BlockSpec((tm, DA_WIDTH), row),
                  pl.BlockSpec((tm, sw), row), pl.BlockSpec((tm, sw), row),
                  pl.BlockSpec((1, sw), const), pl.BlockSpec((sw, sw), const), pl.BlockSpec((1, sw), const),
                  pl.BlockSpec((tm, GDN_WIDTH), row), pl.BlockSpec((tm, GDN_WIDTH), row),
                  pl.BlockSpec((tm, GDN_WIDTH), row),
                  pl.BlockSpec((1, GDN_WIDTH), const), pl.BlockSpec((GDN_WIDTH, GDN_WIDTH), const),
                  pl.BlockSpec((D_MIX, d), const)],
        out_specs=pl.BlockSpec((tm, d), row),
        out_shape=jax.ShapeDtypeStruct((t, d), F32),
        compiler_params=_params("parallel"), name="outproj",
    )(x, mod, da, ys, u, dsk, w_glu, b_glu, o_f, o_r, z, gnw, avg, w_out)


MOE_TM = 1024
MOE_EP = 5


def _pick_lowest(cur, idx, sentinel, axis):
    m = jnp.max(cur, axis=axis, keepdims=True)
    first = jnp.min(jnp.where(cur == m, idx, sentinel), axis=axis, keepdims=True)
    return idx == first


def _route(logits_t, bias):
    tm = logits_t.shape[1]
    neg = jnp.float32(-jnp.inf)
    scores = jax.nn.sigmoid(logits_t)
    biased = scores + bias
    b3 = biased.reshape(N_GROUPS, GROUP_SIZE, tm)
    eidx = lax.broadcasted_iota(jnp.int32, b3.shape, 1)
    m1 = jnp.max(b3, axis=1, keepdims=True)
    p1 = _pick_lowest(b3, eidx, GROUP_SIZE, 1)
    m2 = jnp.max(jnp.where(p1, neg, b3), axis=1, keepdims=True)
    gs = (m1 + m2).reshape(N_GROUPS, tm)
    gidx = lax.broadcasted_iota(jnp.int32, gs.shape, 0)
    gsel = jnp.zeros(gs.shape, jnp.bool_)
    cur = gs
    for _ in range(TOPK_GROUPS):
        pick = _pick_lowest(cur, gidx, N_GROUPS, 0)
        gsel = jnp.logical_or(gsel, pick)
        cur = jnp.where(pick, neg, cur)
    emask = jnp.broadcast_to(gsel.reshape(N_GROUPS, 1, tm), b3.shape)
    cur = jnp.where(emask, b3, neg).reshape(N_EXPERTS, tm)
    ridx = lax.broadcasted_iota(jnp.int32, cur.shape, 0)
    sel = jnp.zeros(cur.shape, jnp.bool_)
    for _ in range(TOP_K):
        pick = _pick_lowest(cur, ridx, N_EXPERTS, 0)
        sel = jnp.logical_or(sel, pick)
        cur = jnp.where(pick, neg, cur)
    w = jnp.where(sel, scores, 0.0)
    return w / jnp.sum(w, axis=0, keepdims=True) * ROUTED_SCALE


def _moe_kernel(x_ref, mod_ref, nw_ref, wr_ref, rb_ref, w13_ref, w2_ref, fnw_ref, o_ref,
                h_sc, gate_sc, acc_sc, hid_sc, *, final_norm):
    e = pl.program_id(1)
    n_e = pl.num_programs(1)
    ep = w13_ref.shape[0]

    @pl.when(e == 0)
    def _():
        h = _rms_rows(x_ref[...]) * nw_ref[...] * (1.0 + mod_ref[0, 4:5, :]) + mod_ref[0, 3:4, :]
        h_sc[...] = h.astype(BF16)
        logits_t = lax.dot_general(wr_ref[...], h, NT_DIMS, precision=HP, preferred_element_type=F32)
        w = _route(logits_t, rb_ref[...])
        tm = w.shape[1]
        row = lax.broadcasted_iota(jnp.int32, (LANES - N_EXPERTS, tm), 0)
        shared = jnp.where(row == 0, 1.0, 0.0).astype(F32)
        gate_sc[...] = jnp.concatenate([w, shared], axis=0).T
        acc_sc[...] = jnp.zeros_like(acc_sc)

    h = h_sc[...]
    lane = lax.broadcasted_iota(jnp.int32, (1, LANES), 1)
    for j in range(ep):
        ab = jnp.dot(h, w13_ref[j], preferred_element_type=F32)
        a, b = ab[:, :MOE_FFN], ab[:, MOE_FFN:]
        g = jnp.sum(jnp.where(lane == e * ep + j, gate_sc[...], 0.0), axis=-1, keepdims=True)
        hid_sc[:, j * MOE_FFN:(j + 1) * MOE_FFN] = (a * _sigmoid(a) * b * g).astype(BF16)
    w2 = w2_ref[...].reshape(ep * MOE_FFN, w2_ref.shape[2])
    acc_sc[...] += jnp.dot(hid_sc[...], w2, preferred_element_type=F32)

    @pl.when(e == n_e - 1)
    def _():
        y = x_ref[...] + mod_ref[0, 5:6, :] * acc_sc[...]
        if final_norm:
            y = _rms_rows(y) * fnw_ref[...]
        o_ref[...] = y


def moe_sublayer(x, mod, rows_per_mod, norm_w, w_router_t, router_bias, w13, w2, final_w, final_norm):
    t, d = x.shape
    tm = min(MOE_TM, rows_per_mod)
    assert t % tm == 0 and rows_per_mod % tm == 0
    n_slots = w13.shape[0]
    ep = MOE_EP
    assert n_slots % ep == 0
    tiles_per_mod = rows_per_mod // tm
    return pl.pallas_call(
        functools.partial(_moe_kernel, final_norm=final_norm),
        grid=(t // tm, n_slots // ep),
        in_specs=[
            pl.BlockSpec((tm, d), lambda i, e: (i, 0)),
            pl.BlockSpec((1, 6, d), lambda i, e: (i // tiles_per_mod, 0, 0)),
            pl.BlockSpec((1, d), lambda i, e: (0, 0)),
            pl.BlockSpec((N_EXPERTS, d), lambda i, e: (0, 0)),
            pl.BlockSpec((N_EXPERTS, 1), lambda i, e: (0, 0)),
            pl.BlockSpec((ep, d, 2 * MOE_FFN), lambda i, e: (e, 0, 0)),
            pl.BlockSpec((ep, MOE_FFN, d), lambda i, e: (e, 0, 0)),
            pl.BlockSpec((1, d), lambda i, e: (0, 0)),
        ],
        out_specs=pl.BlockSpec((tm, d), lambda i, e: (i, 0)),
        out_shape=jax.ShapeDtypeStruct((t, d), F32),
        scratch_shapes=[
            pltpu.VMEM((tm, d), BF16),
            pltpu.VMEM((tm, LANES), F32),
            pltpu.VMEM((tm, d), F32),
            pltpu.VMEM((tm, ep * MOE_FFN), BF16),
        ],
        compiler_params=_params("parallel", "arbitrary"), name="moe_sublayer",
    )(x, mod, norm_w.reshape(1, d), w_router_t, router_bias.reshape(N_EXPERTS, 1), w13, w2, final_w.reshape(1, d))


def _moe_weights(w1, w3, w2, ws1, ws3, ws2):
    w13 = jnp.concatenate([jnp.concatenate([w1, w3], axis=-1),
                           jnp.concatenate([ws1, ws3], axis=-1)[None]], axis=0).astype(BF16)
    w2a = jnp.concatenate([w2, ws2[None]], axis=0).astype(BF16)
    return w13, w2a


def kernel(x, c, ctx, c_ctx, norm1_w, norm2_w, w_mod, b_mod, w_in, w_out, da_lambda, da_subln_w,
           s5_lam_re, s5_lam_im, s5_log_step, s5_b_re, s5_b_im, s5_c_re, s5_c_im, s5_d, s5_w_glu, s5_b_glu,
           gdn_conv_w, gdn_a_log, gdn_dt_bias, gdn_norm_w,
           moe_w_router, moe_router_bias, moe_w1, moe_w3, moe_w2, moe_ws1, moe_ws3, moe_ws2,
           final_norm_w):
    b, n, d = x.shape
    nc = ctx.shape[1]
    cos_t, sin_t = _rope_tables(n)
    cond = jnp.zeros((2 * b, d), F32).at[:b].set(c).at[b].set(c_ctx)
    xl = x.reshape(b * n, d)
    xc = ctx.reshape(b * nc, d)
    head_avg = jnp.kron(jnp.eye(GDN_HEADS, dtype=F32), jnp.full((GDN_DV, GDN_DV), 1.0 / GDN_DV, F32))
    s_zero = jnp.zeros((b, 2, GDN_HEADS // 2, LANES, LANES), F32)
    for i in range(DEPTH):
        ctx_out = i < DEPTH - 1
        last = i == DEPTH - 1
        lam_init = 0.8 - 0.6 * math.exp(-0.3 * i)
        mod_all = mod_proj(cond, w_mod, b_mod, i).reshape(2 * b, 6, d)
        mod, modc = mod_all[:b], mod_all[b:b + 1]

        w_in_p = jnp.pad(w_in[i], ((0, 0), (0, IN_PAD - IN_WIDTH))).astype(BF16)
        q, k, v, u, gq, z, ba = inproj(xl, mod, n, norm1_w[i], w_in_p, cos_t, sin_t, rope=True)
        qc, kc, vc, uc, gqc, zc, bac = inproj(xc, modc, b * nc, norm1_w[i], w_in_p, cos_t, sin_t, rope=False)

        lq1, lk1, lq2, lk2 = da_lambda[i].astype(F32)
        lam = jnp.exp(jnp.sum(lq1 * lk1)) - jnp.exp(jnp.sum(lq2 * lk2)) + lam_init
        lam_row = jnp.full((1, LANES), lam, F32)
        da = diff_attention(q, [kc, k], [vc, v], (nc, n), n, lam_row, da_subln_w[i], lam_init)

        tables = _s5_tables(s5_lam_re[i], s5_lam_im[i], s5_log_step[i], s5_b_re[i], s5_b_im[i],
                            s5_c_re[i], s5_c_im[i])
        ysc, ysl = s5_scan(uc, u, b, tables)

        alog_row = jnp.zeros((1, LANES), F32).at[0, 8:16].set(gdn_a_log[i].astype(F32).reshape(-1))
        dtb_row = jnp.zeros((1, LANES), F32).at[0, 8:16].set(gdn_dt_bias[i].astype(F32).reshape(-1))
        gn = gdn_prep(gq, gdn_conv_w[i], n)
        gnc = gdn_prep(gqc, gdn_conv_w[i], nc)
        ofc, orc, s_ctx = gdn_scan(gnc.reshape(b, nc, GDN_QKV_W), bac.reshape(b, nc, LANES), s_zero, alog_row, dtb_row)
        of, orv, _ = gdn_scan(gn.reshape(b, n, GDN_QKV_W), ba.reshape(b, n, LANES), s_ctx, alog_row, dtb_row)

        w_out_b = w_out[i].astype(BF16)
        dsk = s5_d[i].astype(F32).reshape(1, S5_WIDTH)
        wg = s5_w_glu[i].astype(BF16)
        bg = s5_b_glu[i].astype(F32).reshape(1, S5_WIDTH)
        gnw = jnp.tile(gdn_norm_w[i].astype(F32), GDN_HEADS).reshape(1, GDN_WIDTH)
        xl = outproj(xl, mod, n, da, ysl, u, dsk, wg, bg,
                     of.reshape(b * n, GDN_WIDTH), orv.reshape(b * n, GDN_WIDTH), z, gnw, head_avg, w_out_b)
        if ctx_out:
            dac = diff_attention(qc, [kc], [vc], (nc,), nc, lam_row, da_subln_w[i], lam_init)
            xc = outproj(xc, modc, b * nc, dac, ysc, uc, dsk, wg, bg,
                         ofc.reshape(b * nc, GDN_WIDTH), orc.reshape(b * nc, GDN_WIDTH), zc, gnw, head_avg, w_out_b)

        w13, w2a = _moe_weights(moe_w1[i], moe_w3[i], moe_w2[i], moe_ws1[i], moe_ws3[i], moe_ws2[i])
        wr_t = moe_w_router[i].T
        xl = moe_sublayer(xl, mod, n, norm2_w[i], wr_t, moe_router_bias[i], w13, w2a, final_norm_w, last)
        if ctx_out:
            xc = moe_sublayer(xc, modc, b * nc, norm2_w[i], wr_t, moe_router_bias[i], w13, w2a, final_norm_w, False)
    return xl.reshape(b, n, d)
```

```python
import functools
import math

import jax
import jax.numpy as jnp
import numpy as np
from jax import lax
from jax.experimental import pallas as pl
from jax.experimental.pallas import tpu as pltpu

F32 = jnp.float32
BF16 = jnp.bfloat16

D_MODEL = 1024
DEPTH = 2
GRID_W = 64
EPS = 1e-6

DA_HEADS = 4
DA_HEAD_DIM = D_MODEL // 16
DA_V_DIM = 2 * DA_HEAD_DIM
DA_WIDTH = DA_HEADS * DA_V_DIM
ROPE_THETA = 10000.0

S5_WIDTH = D_MODEL // 4
S5_GROUP = 16
S5_GROUPS = S5_WIDTH // S5_GROUP
S5_STATE = 64

GDN_HEADS = 4
GDN_DK = D_MODEL // 16
GDN_DV = D_MODEL // 16
GDN_WIDTH = GDN_HEADS * GDN_DV
GDN_CONV = 5
GDN_CHUNK = 64

D_MIX = DA_WIDTH + S5_WIDTH + GDN_WIDTH
DA_QK_W = 2 * DA_HEADS * DA_HEAD_DIM
GDN_QKV_W = 2 * GDN_HEADS * GDN_DK + GDN_HEADS * GDN_DV
IN_SIZES = (DA_QK_W, DA_QK_W, DA_WIDTH, S5_WIDTH, GDN_QKV_W, GDN_WIDTH, 2 * GDN_HEADS, 2 * GDN_HEADS)
IN_WIDTH = sum(IN_SIZES)

N_EXPERTS = 64
TOP_K = 8
N_GROUPS = 8
GROUP_SIZE = N_EXPERTS // N_GROUPS
TOPK_GROUPS = 4
MOE_FFN = D_MODEL // 4
ROUTED_SCALE = 2.5

LANES = 128
VMEM_LIMIT_BYTES = 56 * 1024 * 1024

HP = lax.Precision.HIGHEST
NT_DIMS = (((1,), (1,)), ((), ()))


def _params(*sem):
    return pltpu.CompilerParams(dimension_semantics=sem, vmem_limit_bytes=VMEM_LIMIT_BYTES)


def _sigmoid(x):
    return 0.5 * (1.0 + jnp.tanh(0.5 * x))


def _rms_rows(x):
    return x * lax.rsqrt(jnp.mean(x * x, axis=-1, keepdims=True) + EPS)


MOD_TN = 768


def _mod_kernel(c_ref, w_ref, b_ref, o_ref):
    c = c_ref[...]
    act = c * _sigmoid(c)
    o_ref[...] = jnp.dot(act, w_ref[...], precision=HP, preferred_element_type=F32) + b_ref[...]


def mod_proj(cond, w_mod, b_mod, layer):
    r, d = cond.shape
    depth, _, n = w_mod.shape
    return pl.pallas_call(
        _mod_kernel,
        grid=(n // MOD_TN,),
        in_specs=[pl.BlockSpec((r, d), lambda j: (0, 0)),
                  pl.BlockSpec((None, d, MOD_TN), lambda j: (layer, 0, j)),
                  pl.BlockSpec((None, 1, MOD_TN), lambda j: (layer, 0, j))],
        out_specs=pl.BlockSpec((r, MOD_TN), lambda j: (0, j)),
        out_shape=jax.ShapeDtypeStruct((r, n), F32),
        compiler_params=_params("parallel"), name="mod_proj",
    )(cond, w_mod, b_mod.reshape(depth, 1, n))


IN_PAD = 2944
INPROJ_TM = 512
O_Q, O_K, O_V, O_U, O_G, O_Z, O_BA = 0, 512, 1024, 1536, 1792, 2560, 2816


def _rope_apply(x, cos, sin):
    lane = lax.broadcasted_iota(jnp.int32, x.shape, 1)
    up = pltpu.roll(x, LANES - 16, 1)
    dn = pltpu.roll(x, 16, 1)
    partner = jnp.where((lane & 31) < 16, up, dn)
    return x * cos + partner * sin


def _inproj_kernel(x_ref, mod_ref, nw_ref, w_ref, cos_ref, sin_ref,
                   q_ref, k_ref, v_ref, u_ref, ub_ref, g_ref, z_ref, ba_ref, *, rope):
    h = (_rms_rows(x_ref[...]) * nw_ref[...] * (1.0 + mod_ref[0, 1:2, :]) + mod_ref[0, 0:1, :]).astype(BF16)

    def proj(lo, hi):
        return jnp.dot(h, w_ref[:, lo:hi], preferred_element_type=F32)

    scale = DA_HEAD_DIM ** -0.5 * math.log2(math.e)
    for hd in range(DA_HEADS):
        lo = hd * LANES
        qs = proj(O_Q + lo, O_Q + lo + LANES)
        ks = proj(O_K + lo, O_K + lo + LANES)
        if rope:
            qs = _rope_apply(qs, cos_ref[...], sin_ref[...])
            ks = _rope_apply(ks, cos_ref[...], sin_ref[...])
        q_ref[:, lo:lo + LANES] = (qs * scale).astype(BF16)
        k_ref[:, lo:lo + LANES] = ks.astype(BF16)
    v_ref[...] = proj(O_V, O_U).astype(BF16)
    u = proj(O_U, O_G)
    u_ref[...] = u
    ub_ref[...] = u.astype(BF16)
    g_ref[...] = proj(O_G, O_Z)
    z_ref[...] = proj(O_Z, O_BA)
    ba_ref[...] = proj(O_BA, IN_PAD)


def inproj(x, mod, rows_per_mod, norm_w, w_pad, cos_t, sin_t, rope):
    t, d = x.shape
    tm = min(INPROJ_TM, t)
    tpm = rows_per_mod // tm
    npos = cos_t.shape[0] // tm
    row = lambda i: (i, 0)
    widths = (DA_QK_W, DA_QK_W, DA_WIDTH, S5_WIDTH, S5_WIDTH, GDN_QKV_W, GDN_WIDTH, LANES)
    dtypes = (BF16, BF16, BF16, F32, BF16, F32, F32, F32)
    return pl.pallas_call(
        functools.partial(_inproj_kernel, rope=rope),
        grid=(t // tm,),
        in_specs=[pl.BlockSpec((tm, d), row),
                  pl.BlockSpec((1, 6, d), lambda i: (i // tpm, 0, 0)),
                  pl.BlockSpec((1, d), lambda i: (0, 0)),
                  pl.BlockSpec((d, IN_PAD), lambda i: (0, 0)),
                  pl.BlockSpec((tm, LANES), lambda i: (i % npos, 0)),
                  pl.BlockSpec((tm, LANES), lambda i: (i % npos, 0))],
        out_specs=[pl.BlockSpec((tm, w), row) for w in widths],
        out_shape=[jax.ShapeDtypeStruct((t, w), dt) for w, dt in zip(widths, dtypes)],
        compiler_params=_params("parallel"), name="inproj",
    )(x, mod, norm_w.reshape(1, d), w_pad, cos_t, sin_t)


def _rope_tables(n):
    nf = DA_HEAD_DIM // 4
    t = jnp.arange(n, dtype=jnp.int32)
    inv = ROPE_THETA ** (-jnp.arange(nf, dtype=F32) / nf)
    ang_r = (t // GRID_W).astype(F32)[:, None] * inv
    ang_c = (t % GRID_W).astype(F32)[:, None] * inv
    cos64 = jnp.concatenate([jnp.cos(ang_r), jnp.cos(ang_r), jnp.cos(ang_c), jnp.cos(ang_c)], axis=-1)
    sin64 = jnp.concatenate([-jnp.sin(ang_r), jnp.sin(ang_r), -jnp.sin(ang_c), jnp.sin(ang_c)], axis=-1)
    return jnp.tile(cos64, (1, 2)), jnp.tile(sin64, (1, 2))


ATTN_TQ = 512
ATTN_KC = 512


def _attn_kernel(*refs, n_kv, kv_rows, chunks, lam_init):
    q_ref = refs[0]
    k_refs = refs[1:1 + n_kv]
    v_refs = refs[1 + n_kv:1 + 2 * n_kv]
    lam_ref, w_ref, o_ref, v1_sc = refs[1 + 2 * n_kv:]

    @pl.when(pl.program_id(2) == 0)
    def _():
        off = 0
        for ki, rows in enumerate(kv_rows):
            v1_sc[off:off + rows, :DA_V_DIM] = v_refs[ki][...]
            v1_sc[off:off + rows, DA_V_DIM:] = jnp.ones((rows, DA_V_DIM), BF16)
            off += rows

    q = q_ref[...]
    tq = q.shape[0]
    lane = lax.broadcasted_iota(jnp.int32, q.shape, 1)
    zero = jnp.zeros_like(q)
    qq = jnp.concatenate([jnp.where(lane < DA_HEAD_DIM, q, zero), jnp.where(lane >= DA_HEAD_DIM, q, zero)], axis=0)
    m = jnp.full((2 * tq, 1), -jnp.inf, F32)
    acc = jnp.zeros((2 * tq, 2 * DA_V_DIM), F32)
    def scores(chunk):
        ki, start, _, size = chunk
        return lax.dot_general(qq, k_refs[ki][start:start + size, :], NT_DIMS, preferred_element_type=F32)

    s_next = scores(chunks[0])
    for ci, (ki, start, off, size) in enumerate(chunks):
        s = s_next
        if ci + 1 < len(chunks):
            s_next = scores(chunks[ci + 1])
        m_new = jnp.maximum(m, jnp.max(s, axis=-1, keepdims=True))
        p = jnp.exp2(s - m_new).astype(BF16)
        acc = jnp.exp2(m - m_new) * acc + jnp.dot(p, v1_sc[off:off + size, :], preferred_element_type=F32)
        m = m_new
    o = acc[:, :DA_V_DIM] / acc[:, DA_V_DIM:]
    od = o[:tq] - lam_ref[...] * o[tq:]
    o_ref[...] = (_rms_rows(od) * w_ref[...] * (1.0 - lam_init)).astype(o_ref.dtype)


def diff_attention(q, ks, vs, kv_rows, q_rows, lam_row, subln_w, lam_init):
    t = q.shape[0]
    b = t // q_rows
    tq = min(ATTN_TQ, q_rows)
    nq = q_rows // tq
    chunks, off = [], 0
    for ki, rows in enumerate(kv_rows):
        kc = min(ATTN_KC, rows)
        chunks += [(ki, s, off + s, kc) for s in range(0, rows, kc)]
        off += rows
    qmap = lambda bi, h, qi: (bi * nq + qi, h)
    kvmap = lambda bi, h, qi: (bi, h)
    const = lambda bi, h, qi: (0, 0)
    return pl.pallas_call(
        functools.partial(_attn_kernel, n_kv=len(ks), kv_rows=tuple(kv_rows), chunks=tuple(chunks),
                          lam_init=lam_init),
        grid=(b, DA_HEADS, nq),
        in_specs=([pl.BlockSpec((tq, LANES), qmap)]
                  + [pl.BlockSpec((rows, LANES), kvmap) for rows in kv_rows] * 2
                  + [pl.BlockSpec((1, LANES), const)] * 2),
        out_specs=pl.BlockSpec((tq, LANES), qmap),
        out_shape=jax.ShapeDtypeStruct((t, DA_WIDTH), BF16),
        scratch_shapes=[pltpu.VMEM((off, 2 * DA_V_DIM), BF16)],
        compiler_params=_params("parallel", "parallel", "arbitrary"), name="diff_attention",
    )(q, *ks, *vs, lam_row, subln_w.reshape(1, LANES))


S5_LC = 64
S5_CW = S5_LC * S5_GROUP
S5_SW = 2 * S5_STATE


def _s5_discretize(lam_re, lam_im, log_step, b_re, b_im):
    lr, li = lam_re.astype(F32), lam_im.astype(F32)
    step = jnp.exp(log_step.astype(F32))[:, None]
    mag = jnp.exp(lr * step)
    ab_re, ab_im = mag * jnp.cos(li * step), mag * jnp.sin(li * step)
    den = lr * lr + li * li
    nr, ni = ab_re - 1.0, ab_im
    f_re = (nr * lr + ni * li) / den
    f_im = (ni * lr - nr * li) / den
    br, bi = b_re.astype(F32), b_im.astype(F32)
    bb_re = f_re[..., None] * br - f_im[..., None] * bi
    bb_im = f_re[..., None] * bi + f_im[..., None] * br
    return bb_re, bb_im


def _s5_tables(lam_re, lam_im, log_step, b_re, b_im, c_re, c_im):
    lc, g = S5_LC, S5_GROUPS
    bm, cm, mm, k1, k2 = [], [], [], [], []
    for d in range(2):
        lr, li = lam_re[d].astype(F32), lam_im[d].astype(F32)
        step = jnp.exp(log_step[d].astype(F32))[:, None]
        bb_re, bb_im = _s5_discretize(lam_re[d], lam_im[d], log_step[d], b_re[d], b_im[d])
        tau = jnp.arange(lc + 1, dtype=F32)[:, None, None]
        mag = jnp.exp(tau * (lr * step))
        ang = tau * (li * step)
        pr, pi = mag * jnp.cos(ang), mag * jnp.sin(ang)
        abr = pr[..., None] * bb_re - pi[..., None] * bb_im
        abi = pr[..., None] * bb_im + pi[..., None] * bb_re
        cr, ci = c_re[d].astype(F32), c_im[d].astype(F32)
        kern = (jnp.einsum('gkp,tgph->tgkh', cr, abr[:lc], precision=HP)
                - jnp.einsum('gkp,tgph->tgkh', ci, abi[:lc], precision=HP))
        rank = np.arange(lc) if d == 0 else lc - 1 - np.arange(lc)
        e_in = lc - 1 - rank
        bmat = jnp.concatenate([abr[e_in].transpose(1, 0, 3, 2), abi[e_in].transpose(1, 0, 3, 2)], axis=-1)
        bm.append(bmat.reshape(g, S5_CW, S5_SW))
        e_out = rank + 1
        pro, pio = pr[e_out][:, :, None, :], pi[e_out][:, :, None, :]
        car = cr[None] * pro - ci[None] * pio
        cai = cr[None] * pio + ci[None] * pro
        cmat = jnp.concatenate([car.transpose(1, 3, 0, 2), -cai.transpose(1, 3, 0, 2)], axis=1)
        cm.append(cmat.reshape(g, S5_SW, S5_CW))
        kt = kern.transpose(1, 3, 0, 2).astype(BF16)
        zpad = jnp.zeros((g, S5_GROUP, lc - 1, S5_GROUP), BF16)
        strip = jnp.concatenate([zpad, kt] if d == 0 else [kt[:, :, ::-1], zpad], axis=2)
        strip = strip.reshape(g, S5_GROUP, (2 * lc - 1) * S5_GROUP)
        rows = [strip[:, :, (lc - 1 - s) * S5_GROUP:(2 * lc - 1 - s) * S5_GROUP] for s in range(lc)]
        mm.append(jnp.stack(rows, axis=1).reshape(g, S5_CW, S5_CW))
        alr, ali = pr[lc], pi[lc]
        k1.append(jnp.repeat(jnp.concatenate([alr, alr], axis=-1), 8, axis=0))
        k2.append(jnp.repeat(jnp.concatenate([-ali, ali], axis=-1), 8, axis=0))
    return (jnp.stack(bm).astype(BF16), jnp.stack(cm).astype(BF16), jnp.stack(mm).astype(BF16),
            jnp.stack(k1), jnp.stack(k2))


def _s5_local_kernel(uc_ref, ul_ref, bm_ref, s_ref):
    u = jnp.concatenate([uc_ref[0], ul_ref[0]], axis=0)
    s = jnp.dot(u, bm_ref[0, 0], preferred_element_type=F32)
    s_ref[0] = s.reshape(s_ref.shape[1:])


def _s5_carry_kernel(s_ref, k1_ref, k2_ref, xin_ref, *, n_ctx, n_chunks):
    d = pl.program_id(0)
    k1, k2 = k1_ref[0], k2_ref[0]

    def body(k, x):
        rev = jnp.where(k < n_ctx, n_ctx - 1 - k, n_chunks + n_ctx - 1 - k)
        c = jnp.where(d == 0, k, rev)
        xin_ref[0, c] = x
        return k1 * x + k2 * pltpu.roll(x, S5_STATE, 1) + s_ref[0, c]

    lax.fori_loop(0, n_chunks, body, jnp.zeros(k1.shape, F32))


def _s5_out_kernel(uc_ref, ul_ref, m_ref, xin_ref, cm_ref, yc_ref, yl_ref):
    u = jnp.concatenate([uc_ref[0], ul_ref[0]], axis=0)
    acc = jnp.dot(u, m_ref[0, 0], preferred_element_type=F32)
    acc += jnp.dot(u, m_ref[1, 0], preferred_element_type=F32)
    for d in range(2):
        xin = xin_ref[d].reshape(u.shape[0], S5_SW).astype(BF16)
        acc += jnp.dot(xin, cm_ref[d, 0], preferred_element_type=F32)
    rc = yc_ref.shape[1]
    yc_ref[0] = acc[:rc].astype(yc_ref.dtype)
    yl_ref[0] = acc[rc:].astype(yl_ref.dtype)


def s5_scan(u_ctx, u_lat, b, tables):
    bmat, cmat, mmat, k1, k2 = tables
    g, lc = S5_GROUPS, S5_LC
    n_ctx, n_lat = u_ctx.shape[0] // b, u_lat.shape[0] // b
    assert b == 8 and n_lat % lc == 0 and n_ctx % lc == 0
    rc, rl = n_ctx // lc * b, n_lat // lc * b
    nch = (n_ctx + n_lat) // lc

    def to_groups(x, rows):
        x = x.reshape(b, rows // lc, lc, g, S5_GROUP).transpose(3, 1, 0, 2, 4).astype(BF16)
        return x.reshape(g, rows // lc * b, S5_CW)

    def from_groups(y, rows):
        return y.reshape(g, rows // lc, b, lc, S5_GROUP).transpose(2, 1, 3, 0, 4).reshape(b * rows, S5_WIDTH)

    ugc, ugl = to_groups(u_ctx, n_ctx), to_groups(u_lat, n_lat)
    s = pl.pallas_call(
        _s5_local_kernel,
        grid=(2, g),
        in_specs=[pl.BlockSpec((1, rc, S5_CW), lambda d, gi: (gi, 0, 0)),
                  pl.BlockSpec((1, rl, S5_CW), lambda d, gi: (gi, 0, 0)),
                  pl.BlockSpec((1, 1, S5_CW, S5_SW), lambda d, gi: (d, gi, 0, 0))],
        out_specs=pl.BlockSpec((1, nch, b, S5_SW), lambda d, gi: (d, 0, gi, 0)),
        out_shape=jax.ShapeDtypeStruct((2, nch, g * b, S5_SW), F32),
        compiler_params=_params("arbitrary", "arbitrary"), name="s5_local",
    )(ugc, ugl, bmat)
    xin = pl.pallas_call(
        functools.partial(_s5_carry_kernel, n_ctx=n_ctx // lc, n_chunks=nch),
        grid=(2,),
        in_specs=[pl.BlockSpec((1, nch, g * b, S5_SW), lambda d: (d, 0, 0, 0)),
                  pl.BlockSpec((1, g * b, S5_SW), lambda d: (d, 0, 0)),
                  pl.BlockSpec((1, g * b, S5_SW), lambda d: (d, 0, 0))],
        out_specs=pl.BlockSpec((1, nch, g * b, S5_SW), lambda d: (d, 0, 0, 0)),
        out_shape=jax.ShapeDtypeStruct((2, nch, g * b, S5_SW), F32),
        compiler_params=_params("arbitrary"), name="s5_carry",
    )(s, k1, k2)
    yc, yl = pl.pallas_call(
        _s5_out_kernel,
        grid=(g,),
        in_specs=[pl.BlockSpec((1, rc, S5_CW), lambda gi: (gi, 0, 0)),
                  pl.BlockSpec((1, rl, S5_CW), lambda gi: (gi, 0, 0)),
                  pl.BlockSpec((2, 1, S5_CW, S5_CW), lambda gi: (0, gi, 0, 0)),
                  pl.BlockSpec((2, nch, b, S5_SW), lambda gi: (0, 0, gi, 0)),
                  pl.BlockSpec((2, 1, S5_SW, S5_CW), lambda gi: (0, gi, 0, 0))],
        out_specs=[pl.BlockSpec((1, rc, S5_CW), lambda gi: (gi, 0, 0)),
                   pl.BlockSpec((1, rl, S5_CW), lambda gi: (gi, 0, 0))],
        out_shape=[jax.ShapeDtypeStruct((g, rc, S5_CW), BF16), jax.ShapeDtypeStruct((g, rl, S5_CW), BF16)],
        compiler_params=_params("arbitrary"), name="s5_out",
    )(ugc, ugl, mmat, xin, cmat)
    return from_groups(yc, n_ctx), from_groups(yl, n_lat)


def _gdn_prep_kernel(x_ref, w_ref, o_ref):
    j = pl.program_id(1)
    x = x_ref[...]
    n = x.shape[0]
    row = lax.broadcasted_iota(jnp.int32, x.shape, 0)
    half = GDN_CONV // 2
    acc = x * w_ref[half:half + 1, :]
    for sh in range(1, half + 1):
        acc += jnp.where(row >= sh, pltpu.roll(x, sh, 0), 0.0) * w_ref[half - sh:half - sh + 1, :]
        acc += jnp.where(row < n - sh, pltpu.roll(x, n - sh, 0), 0.0) * w_ref[half + sh:half + sh + 1, :]
    a = acc * _sigmoid(acc)
    lane = lax.broadcasted_iota(jnp.int32, x.shape, 1)
    lo = lane < GDN_DK
    sq = a * a
    s_lo = jnp.sum(jnp.where(lo, sq, 0.0), axis=-1, keepdims=True)
    s_hi = jnp.sum(jnp.where(lo, 0.0, sq), axis=-1, keepdims=True)
    nrm = a * lax.rsqrt(jnp.where(lo, s_lo, s_hi) + EPS)
    q_blocks = GDN_HEADS * GDN_DK // LANES
    nrm = nrm * jnp.where(j < q_blocks, GDN_DK ** -0.5, 1.0)
    o_ref[...] = jnp.where(j < 2 * q_blocks, nrm, a)


def gdn_prep(qkv, conv_w, seg):
    t, w = qkv.shape
    return pl.pallas_call(
        _gdn_prep_kernel,
        grid=(t // seg, w // LANES),
        in_specs=[pl.BlockSpec((seg, LANES), lambda s, j: (s, j)),
                  pl.BlockSpec((GDN_CONV, LANES), lambda s, j: (0, j))],
        out_specs=pl.BlockSpec((seg, LANES), lambda s, j: (s, j)),
        out_shape=jax.ShapeDtypeStruct((t, w), F32),
        compiler_params=_params("parallel", "arbitrary"), name="gdn_prep",
    )(qkv, conv_w)


def _bd(x):
    x2 = jnp.concatenate([x, x], axis=0)
    r = lax.broadcasted_iota(jnp.int32, x2.shape, 0)
    l = lax.broadcasted_iota(jnp.int32, x2.shape, 1)
    return jnp.where((r >> 6) == (l >> 6), x2, jnp.zeros_like(x2))


def _mm(a, b):
    return jnp.dot(a.astype(BF16), b.astype(BF16), preferred_element_type=F32)


def _gdn_chunks(insts):
    c = GDN_CHUNK
    n = len(insts)
    every = range(n)
    q, k, v, beta, gcol, s_bd, rev = (list(t) for t in zip(*insts))
    i = lax.broadcasted_iota(jnp.int32, (c, LANES), 0)
    j = lax.broadcasted_iota(jnp.int32, (c, LANES), 1) & (c - 1)
    ti = lax.broadcasted_iota(jnp.int32, (c, c), 0)
    tj = lax.broadcasted_iota(jnp.int32, (c, c), 1)
    causal = [i <= j if r else i >= j for r in rev]
    strict = [i < j if r else i > j for r in rev]
    upto = [i >= j if r else i <= j for r in rev]
    lmat = [(tj >= ti if r else tj <= ti).astype(F32) for r in rev]
    ones = jnp.ones((c, c), F32)
    eye = jnp.where(i == j, 1.0, 0.0)
    lg = [jnp.dot(lmat[t], gcol[t], precision=HP, preferred_element_type=F32) for t in every]
    rg = [jnp.dot(ones, jnp.where(upto[t], gcol[t], 0.0), precision=HP, preferred_element_type=F32) for t in every]
    decay = [jnp.where(causal[t], jnp.exp(jnp.where(causal[t], lg[t] - rg[t], 0.0)), 0.0) for t in every]
    kb = [k[t] * beta[t] for t in every]
    k_bd = [_bd(k[t].astype(BF16)) for t in every]
    a = [jnp.where(strict[t], lax.dot_general(kb[t].astype(BF16), k_bd[t], NT_DIMS, preferred_element_type=F32)
                   * decay[t], 0.0) for t in every]
    qk = [lax.dot_general(q[t].astype(BF16), k_bd[t], NT_DIMS, preferred_element_type=F32) * decay[t] for t in every]
    eg = [jnp.exp(lg[t]) for t in every]
    same = [(i >> sh) == (j >> sh) for sh in (3, 4, 5)]
    x = [jnp.where(same[0], a[t], 0.0) for t in every]
    p = [eye - x[t] for t in every]
    for _ in range(2):
        x = [_mm(x[t], _bd(x[t])) for t in every]
        p = [p[t] + _mm(p[t], _bd(x[t])) for t in every]
    for lvl in range(3):
        inner = same[lvl]
        join = jnp.logical_not(inner) if lvl == 2 else jnp.logical_and(same[lvl + 1], jnp.logical_not(inner))
        tl = [_mm(p[t], _bd(jnp.where(join, a[t], 0.0))) for t in every]
        p = [p[t] - _mm(tl[t], _bd(p[t])) for t in every]
    u = [_mm(p[t], _bd(v[t] * beta[t])) for t in every]
    w = [_mm(p[t], _bd(kb[t] * eg[t])) for t in every]
    v_new = [u[t] - _mm(w[t], s_bd[t]) for t in every]
    o_state = [_mm(q[t] * eg[t], s_bd[t]) for t in every]
    o = [o_state[t] + _mm(qk[t], _bd(v_new[t])) for t in every]
    g_last = [lg[t][0:1, :] if rev[t] else lg[t][c - 1:c, :] for t in every]
    k_dec = [k[t] * jnp.exp(g_last[t] - lg[t]) for t in every]
    upd = [jnp.dot(k_dec[t].T.astype(BF16), v_new[t].astype(BF16), preferred_element_type=F32) for t in every]
    r2 = lax.broadcasted_iota(jnp.int32, (LANES, LANES), 0)
    l2 = lax.broadcasted_iota(jnp.int32, (LANES, LANES), 1)
    diag = (r2 >> 6) == (l2 >> 6)
    s_new = [s_bd[t] * jnp.exp(g_last[t]) + jnp.where(diag, upd[t], 0.0) for t in every]
    return list(zip(o, s_new))


GDN_BATCH_UNROLL = 4


def _gdn_scan_kernel(qf, kf, vf, baf, qr, kr, vr, bar, alog_ref, dtb_ref, s0_ref,
                     of_ref, or_ref, sfin_ref, s_sc):
    c = pl.program_id(0)

    @pl.when(c == 0)
    def _():
        s_sc[...] = s0_ref[...]

    lane = lax.broadcasted_iota(jnp.int32, (GDN_CHUNK, LANES), 1)
    first = lane < GDN_DK
    ins = ((qf, kf, vf, baf, of_ref), (qr, kr, vr, bar, or_ref))

    def body(it, carry):
        work = []
        for bb in range(GDN_BATCH_UNROLL):
            b = it * GDN_BATCH_UNROLL + bb
            for d in range(2):
                q_ref, k_ref, v_ref, ba_ref, o_ref = ins[d]
                ba = ba_ref[b]
                bsig = _sigmoid(ba)
                sp = ba + dtb_ref[...]
                gall = -jnp.exp(alog_ref[...]) * (jnp.maximum(sp, 0.0) + jnp.log(1.0 + jnp.exp(-jnp.abs(sp))))
                for hp in range(GDN_HEADS // 2):
                    col = d * GDN_HEADS + 2 * hp
                    beta = jnp.where(first, bsig[:, col:col + 1], bsig[:, col + 1:col + 2])
                    gcol = jnp.where(first, gall[:, 8 + col:9 + col], gall[:, 9 + col:10 + col])
                    sl = slice(hp * LANES, (hp + 1) * LANES)
                    work.append((b, d, hp, sl, o_ref, (q_ref[b, :, sl], k_ref[b, :, sl], v_ref[b, :, sl],
                                                       beta, gcol, s_sc[b, d, hp], d == 1)))
        done = _gdn_chunks([args for (_, _, _, _, _, args) in work])
        for (b, d, hp, sl, o_ref, _), (o, s_new) in zip(work, done):
            o_ref[b, :, sl] = o
            s_sc[b, d, hp] = s_new
        return carry

    lax.fori_loop(0, s_sc.shape[0] // GDN_BATCH_UNROLL, body, 0)

    @pl.when(c == pl.num_programs(0) - 1)
    def _():
        sfin_ref[...] = s_sc[...]


def gdn_scan(qkvn, ba, s0, alog_row, dtb_row):
    b, l, _ = qkvn.shape
    nch = l // GDN_CHUNK
    blk = (b, GDN_CHUNK, GDN_WIDTH)
    fwd = lambda col: (lambda c: (0, c, col))
    bwd = lambda col: (lambda c: (0, nch - 1 - c, col))
    st = pl.BlockSpec(s0.shape, lambda c: (0, 0, 0, 0, 0))
    return pl.pallas_call(
        _gdn_scan_kernel,
        grid=(nch,),
        in_specs=[pl.BlockSpec(blk, fwd(0)), pl.BlockSpec(blk, fwd(1)), pl.BlockSpec(blk, fwd(2)),
                  pl.BlockSpec((b, GDN_CHUNK, LANES), fwd(0)),
                  pl.BlockSpec(blk, bwd(0)), pl.BlockSpec(blk, bwd(1)), pl.BlockSpec(blk, bwd(2)),
                  pl.BlockSpec((b, GDN_CHUNK, LANES), bwd(0)),
                  pl.BlockSpec((1, LANES), lambda c: (0, 0)), pl.BlockSpec((1, LANES), lambda c: (0, 0)), st],
        out_specs=[pl.BlockSpec(blk, fwd(0)), pl.BlockSpec(blk, bwd(0)), st],
        out_shape=[jax.ShapeDtypeStruct((b, l, GDN_WIDTH), F32)] * 2 + [jax.ShapeDtypeStruct(s0.shape, F32)],
        scratch_shapes=[pltpu.VMEM(s0.shape, F32)],
        compiler_params=_params("arbitrary"), name="gdn_scan",
    )(qkvn, qkvn, qkvn, ba, qkvn, qkvn, qkvn, ba, alog_row, dtb_row, s0)


OUTPROJ_TM = 512


def _outproj_kernel(x_ref, mod_ref, da_ref, ys_ref, u_ref, dsk_ref, wg_ref, bg_ref,
                    of_ref, or_ref, z_ref, gnw_ref, avg_ref, w_ref, o_ref):
    y = ys_ref[...] + u_ref[...] * dsk_ref[...]
    zz = 0.5 * y * (1.0 + jnp.tanh(math.sqrt(2.0 / math.pi) * (y + 0.044715 * (y * y * y))))
    glu = zz * _sigmoid(jnp.dot(zz.astype(BF16), wg_ref[...], preferred_element_type=F32) + bg_ref[...])
    o = of_ref[...] + or_ref[...]
    ms = jnp.dot(o * o, avg_ref[...], precision=HP, preferred_element_type=F32)
    z = z_ref[...]
    gd = o * lax.rsqrt(ms + EPS) * gnw_ref[...] * (z * _sigmoid(z))
    acc = jnp.dot(da_ref[...], w_ref[0:DA_WIDTH, :], preferred_element_type=F32)
    acc += jnp.dot(glu.astype(BF16), w_ref[DA_WIDTH:DA_WIDTH + S5_WIDTH, :], preferred_element_type=F32)
    acc += jnp.dot(gd.astype(BF16), w_ref[DA_WIDTH + S5_WIDTH:, :], preferred_element_type=F32)
    o_ref[...] = x_ref[...] + mod_ref[0, 2:3, :] * acc


def outproj(x, mod, rows_per_mod, da, ys, u, dsk, w_glu, b_glu, o_f, o_r, z, gnw, avg, w_out):
    t, d = x.shape
    tm = min(OUTPROJ_TM, t)
    tpm = rows_per_mod // tm
    row = lambda i: (i, 0)
    const = lambda i: (0, 0)
    sw = S5_WIDTH
    return pl.pallas_call(
        _outproj_kernel,
        grid=(t // tm,),
        in_specs=[pl.BlockSpec((tm, d), row),
                  pl.BlockSpec((1, 6, d), lambda i: (i // tpm, 0, 0)),
                  pl.BlockSpec((tm, DA_WIDTH), row),
                  pl.BlockSpec((tm, sw), row), pl.BlockSpec((tm, sw), row),
                  pl.BlockSpec((1, sw), const), pl.BlockSpec((sw, sw), const), pl.BlockSpec((1, sw), const),
                  pl.BlockSpec((tm, GDN_WIDTH), row), pl.BlockSpec((tm, GDN_WIDTH), row),
                  pl.BlockSpec((tm, GDN_WIDTH), row),
                  pl.BlockSpec((1, GDN_WIDTH), const), pl.BlockSpec((GDN_WIDTH, GDN_WIDTH), const),
                  pl.BlockSpec((D_MIX, d), const)],
        out_specs=pl.BlockSpec((tm, d), row),
        out_shape=jax.ShapeDtypeStruct((t, d), F32),
        compiler_params=_params("parallel"), name="outproj",
    )(x, mod, da, ys, u, dsk, w_glu, b_glu, o_f, o_r, z, gnw, avg, w_out)


MOE_TM = 1024
MOE_EP = 5


def _pick_lowest(cur, idx, sentinel, axis):
    m = jnp.max(cur, axis=axis, keepdims=True)
    first = jnp.min(jnp.where(cur == m, idx, sentinel), axis=axis, keepdims=True)
    return idx == first


def _route(logits_t, bias):
    tm = logits_t.shape[1]
    neg = jnp.float32(-jnp.inf)
    scores = jax.nn.sigmoid(logits_t)
    biased = scores + bias
    b3 = biased.reshape(N_GROUPS, GROUP_SIZE, tm)
    eidx = lax.broadcasted_iota(jnp.int32, b3.shape, 1)
    m1 = jnp.max(b3, axis=1, keepdims=True)
    p1 = _pick_lowest(b3, eidx, GROUP_SIZE, 1)
    m2 = jnp.max(jnp.where(p1, neg, b3), axis=1, keepdims=True)
    gs = (m1 + m2).reshape(N_GROUPS, tm)
    gidx = lax.broadcasted_iota(jnp.int32, gs.shape, 0)
    gsel = jnp.zeros(gs.shape, jnp.bool_)
    cur = gs
    for _ in range(TOPK_GROUPS):
        pick = _pick_lowest(cur, gidx, N_GROUPS, 0)
        gsel = jnp.logical_or(gsel, pick)
        cur = jnp.where(pick, neg, cur)
    emask = jnp.broadcast_to(gsel.reshape(N_GROUPS, 1, tm), b3.shape)
    cur = jnp.where(emask, b3, neg).reshape(N_EXPERTS, tm)
    ridx = lax.broadcasted_iota(jnp.int32, cur.shape, 0)
    sel = jnp.zeros(cur.shape, jnp.bool_)
    for _ in range(TOP_K):
        pick = _pick_lowest(cur, ridx, N_EXPERTS, 0)
        sel = jnp.logical_or(sel, pick)
        cur = jnp.where(pick, neg, cur)
    w = jnp.where(sel, scores, 0.0)
    return w / jnp.sum(w, axis=0, keepdims=True) * ROUTED_SCALE


def _moe_kernel(x_ref, mod_ref, nw_ref, wr_ref, rb_ref, w13_ref, w2_ref, fnw_ref, o_ref,
                h_sc, gate_sc, acc_sc, hid_sc, *, final_norm):
    e = pl.program_id(1)
    n_e = pl.num_programs(1)
    ep = w13_ref.shape[0]

    @pl.when(e == 0)
    def _():
        h = _rms_rows(x_ref[...]) * nw_ref[...] * (1.0 + mod_ref[0, 4:5, :]) + mod_ref[0, 3:4, :]
        h_sc[...] = h.astype(BF16)
        logits_t = lax.dot_general(wr_ref[...], h, NT_DIMS, precision=HP, preferred_element_type=F32)
        w = _route(logits_t, rb_ref[...])
        tm = w.shape[1]
        row = lax.broadcasted_iota(jnp.int32, (LANES - N_EXPERTS, tm), 0)
        shared = jnp.where(row == 0, 1.0, 0.0).astype(F32)
        gate_sc[...] = jnp.concatenate([w, shared], axis=0).T
        acc_sc[...] = jnp.zeros_like(acc_sc)

    h = h_sc[...]
    lane = lax.broadcasted_iota(jnp.int32, (1, LANES), 1)
    for j in range(ep):
        ab = jnp.dot(h, w13_ref[j], preferred_element_type=F32)
        a, b = ab[:, :MOE_FFN], ab[:, MOE_FFN:]
        g = jnp.sum(jnp.where(lane == e * ep + j, gate_sc[...], 0.0), axis=-1, keepdims=True)
        hid_sc[:, j * MOE_FFN:(j + 1) * MOE_FFN] = (a * _sigmoid(a) * b * g).astype(BF16)
    w2 = w2_ref[...].reshape(ep * MOE_FFN, w2_ref.shape[2])
    acc_sc[...] += jnp.dot(hid_sc[...], w2, preferred_element_type=F32)

    @pl.when(e == n_e - 1)
    def _():
        y = x_ref[...] + mod_ref[0, 5:6, :] * acc_sc[...]
        if final_norm:
            y = _rms_rows(y) * fnw_ref[...]
        o_ref[...] = y


def moe_sublayer(x, mod, rows_per_mod, norm_w, w_router_t, router_bias, w13, w2, final_w, final_norm):
    t, d = x.shape
    tm = min(MOE_TM, rows_per_mod)
    assert t % tm == 0 and rows_per_mod % tm == 0
    n_slots = w13.shape[0]
    ep = MOE_EP
    assert n_slots % ep == 0
    tiles_per_mod = rows_per_mod // tm
    return pl.pallas_call(
        functools.partial(_moe_kernel, final_norm=final_norm),
        grid=(t // tm, n_slots // ep),
        in_specs=[
            pl.BlockSpec((tm, d), lambda i, e: (i, 0)),
            pl.BlockSpec((1, 6, d), lambda i, e: (i // tiles_per_mod, 0, 0)),
            pl.BlockSpec((1, d), lambda i, e: (0, 0)),
            pl.BlockSpec((N_EXPERTS, d), lambda i, e: (0, 0)),
            pl.BlockSpec((N_EXPERTS, 1), lambda i, e: (0, 0)),
            pl.BlockSpec((ep, d, 2 * MOE_FFN), lambda i, e: (e, 0, 0)),
            pl.BlockSpec((ep, MOE_FFN, d), lambda i, e: (e, 0, 0)),
            pl.BlockSpec((1, d), lambda i, e: (0, 0)),
        ],
        out_specs=pl.BlockSpec((tm, d), lambda i, e: (i, 0)),
        out_shape=jax.ShapeDtypeStruct((t, d), F32),
        scratch_shapes=[
            pltpu.VMEM((tm, d), BF16),
            pltpu.VMEM((tm, LANES), F32),
            pltpu.VMEM((tm, d), F32),
            pltpu.VMEM((tm, ep * MOE_FFN), BF16),
        ],
        compiler_params=_params("parallel", "arbitrary"), name="moe_sublayer",
    )(x, mod, norm_w.reshape(1, d), w_router_t, router_bias.reshape(N_EXPERTS, 1), w13, w2, final_w.reshape(1, d))


def _moe_weights(w1, w3, w2, ws1, ws3, ws2):
    w13 = jnp.concatenate([jnp.concatenate([w1, w3], axis=-1),
                           jnp.concatenate([ws1, ws3], axis=-1)[None]], axis=0).astype(BF16)
    w2a = jnp.concatenate([w2, ws2[None]], axis=0).astype(BF16)
    return w13, w2a


def kernel(x, c, ctx, c_ctx, norm1_w, norm2_w, w_mod, b_mod, w_in, w_out, da_lambda, da_subln_w,
           s5_lam_re, s5_lam_im, s5_log_step, s5_b_re, s5_b_im, s5_c_re, s5_c_im, s5_d, s5_w_glu, s5_b_glu,
           gdn_conv_w, gdn_a_log, gdn_dt_bias, gdn_norm_w,
           moe_w_router, moe_router_bias, moe_w1, moe_w3, moe_w2, moe_ws1, moe_ws3, moe_ws2,
           final_norm_w):
    b, n, d = x.shape
    nc = ctx.shape[1]
    cos_t, sin_t = _rope_tables(n)
    cond = jnp.zeros((2 * b, d), F32).at[:b].set(c).at[b].set(c_ctx)
    xl = x.reshape(b * n, d)
    xc = ctx.reshape(b * nc, d)
    head_avg = jnp.kron(jnp.eye(GDN_HEADS, dtype=F32), jnp.full((GDN_DV, GDN_DV), 1.0 / GDN_DV, F32))
    s_zero = jnp.zeros((b, 2, GDN_HEADS // 2, LANES, LANES), F32)
    for i in range(DEPTH):
        ctx_out = i < DEPTH - 1
        last = i == DEPTH - 1
        lam_init = 0.8 - 0.6 * math.exp(-0.3 * i)
        mod_all = mod_proj(cond, w_mod, b_mod, i).reshape(2 * b, 6, d)
        mod, modc = mod_all[:b], mod_all[b:b + 1]

        w_in_p = jnp.pad(w_in[i], ((0, 0), (0, IN_PAD - IN_WIDTH))).astype(BF16)
        q, k, v, u, ub, gq, z, ba = inproj(xl, mod, n, norm1_w[i], w_in_p, cos_t, sin_t, rope=True)
        qc, kc, vc, uc, ubc, gqc, zc, bac = inproj(xc, modc, b * nc, norm1_w[i], w_in_p, cos_t, sin_t, rope=False)

        lq1, lk1, lq2, lk2 = da_lambda[i].astype(F32)
        lam = jnp.exp(jnp.sum(lq1 * lk1)) - jnp.exp(jnp.sum(lq2 * lk2)) + lam_init
        lam_row = jnp.full((1, LANES), lam, F32)
        da = diff_attention(q, [kc, k], [vc, v], (nc, n), n, lam_row, da_subln_w[i], lam_init)

        tables = _s5_tables(s5_lam_re[i], s5_lam_im[i], s5_log_step[i], s5_b_re[i], s5_b_im[i],
                            s5_c_re[i], s5_c_im[i])
        ysc, ysl = s5_scan(ubc, ub, b, tables)

        alog_row = jnp.zeros((1, LANES), F32).at[0, 8:16].set(gdn_a_log[i].astype(F32).reshape(-1))
        dtb_row = jnp.zeros((1, LANES), F32).at[0, 8:16].set(gdn_dt_bias[i].astype(F32).reshape(-1))
        gn = gdn_prep(gq, gdn_conv_w[i], n)
        gnc = gdn_prep(gqc, gdn_conv_w[i], nc)
        ofc, orc, s_ctx = gdn_scan(gnc.reshape(b, nc, GDN_QKV_W), bac.reshape(b, nc, LANES), s_zero, alog_row, dtb_row)
        of, orv, _ = gdn_scan(gn.reshape(b, n, GDN_QKV_W), ba.reshape(b, n, LANES), s_ctx, alog_row, dtb_row)

        w_out_b = w_out[i].astype(BF16)
        dsk = s5_d[i].astype(F32).reshape(1, S5_WIDTH)
        wg = s5_w_glu[i].astype(BF16)
        bg = s5_b_glu[i].astype(F32).reshape(1, S5_WIDTH)
        gnw = jnp.tile(gdn_norm_w[i].astype(F32), GDN_HEADS).reshape(1, GDN_WIDTH)
        xl = outproj(xl, mod, n, da, ysl, u, dsk, wg, bg,
                     of.reshape(b * n, GDN_WIDTH), orv.reshape(b * n, GDN_WIDTH), z, gnw, head_avg, w_out_b)
        if ctx_out:
            dac = diff_attention(qc, [kc], [vc], (nc,), nc, lam_row, da_subln_w[i], lam_init)
            xc = outproj(xc, modc, b * nc, dac, ysc, uc, dsk, wg, bg,
                         ofc.reshape(b * nc, GDN_WIDTH), orc.reshape(b * nc, GDN_WIDTH), zc, gnw, head_avg, w_out_b)

        w13, w2a = _moe_weights(moe_w1[i], moe_w3[i], moe_w2[i], moe_ws1[i], moe_ws3[i], moe_ws2[i])
        wr_t = moe_w_router[i].T
        xl = moe_sublayer(xl, mod, n, norm2_w[i], wr_t, moe_router_bias[i], w13, w2a, final_norm_w, last)
        if ctx_out:
            xc = moe_sublayer(xc, modc, b * nc, norm2_w[i], wr_t, moe_router_bias[i], w13, w2a, final_norm_w, False)
    return xl.reshape(b, n, d)
```

```python
import functools
import math

import jax
import jax.numpy as jnp
import numpy as np
from jax import lax
from jax.experimental import pallas as pl
from jax.experimental.pallas import tpu as pltpu

F32 = jnp.float32
BF16 = jnp.bfloat16

D_MODEL = 1024
DEPTH = 2
GRID_W = 64
EPS = 1e-6

DA_HEADS = 4
DA_HEAD_DIM = D_MODEL // 16
DA_V_DIM = 2 * DA_HEAD_DIM
DA_WIDTH = DA_HEADS * DA_V_DIM
ROPE_THETA = 10000.0

S5_WIDTH = D_MODEL // 4
S5_GROUP = 16
S5_GROUPS = S5_WIDTH // S5_GROUP
S5_STATE = 64

GDN_HEADS = 4
GDN_DK = D_MODEL // 16
GDN_DV = D_MODEL // 16
GDN_WIDTH = GDN_HEADS * GDN_DV
GDN_CONV = 5
GDN_CHUNK = 64

D_MIX = DA_WIDTH + S5_WIDTH + GDN_WIDTH
DA_QK_W = 2 * DA_HEADS * DA_HEAD_DIM
GDN_QKV_W = 2 * GDN_HEADS * GDN_DK + GDN_HEADS * GDN_DV
IN_SIZES = (DA_QK_W, DA_QK_W, DA_WIDTH, S5_WIDTH, GDN_QKV_W, GDN_WIDTH, 2 * GDN_HEADS, 2 * GDN_HEADS)
IN_WIDTH = sum(IN_SIZES)

N_EXPERTS = 64
TOP_K = 8
N_GROUPS = 8
GROUP_SIZE = N_EXPERTS // N_GROUPS
TOPK_GROUPS = 4
MOE_FFN = D_MODEL // 4
ROUTED_SCALE = 2.5

LANES = 128
VMEM_LIMIT_BYTES = 56 * 1024 * 1024

HP = lax.Precision.HIGHEST
NT_DIMS = (((1,), (1,)), ((), ()))


def _params(*sem):
    return pltpu.CompilerParams(dimension_semantics=sem, vmem_limit_bytes=VMEM_LIMIT_BYTES)


def _sigmoid(x):
    return 0.5 * (1.0 + jnp.tanh(0.5 * x))


def _rms_rows(x):
    return x * lax.rsqrt(jnp.mean(x * x, axis=-1, keepdims=True) + EPS)


MOD_TN = 768


def _mod_kernel(c_ref, w_ref, b_ref, o_ref):
    c = c_ref[...]
    act = c * _sigmoid(c)
    o_ref[...] = jnp.dot(act, w_ref[...], precision=HP, preferred_element_type=F32) + b_ref[...]


def mod_proj(cond, w_mod, b_mod, layer):
    r, d = cond.shape
    depth, _, n = w_mod.shape
    return pl.pallas_call(
        _mod_kernel,
        grid=(n // MOD_TN,),
        in_specs=[pl.BlockSpec((r, d), lambda j: (0, 0)),
                  pl.BlockSpec((None, d, MOD_TN), lambda j: (layer, 0, j)),
                  pl.BlockSpec((None, 1, MOD_TN), lambda j: (layer, 0, j))],
        out_specs=pl.BlockSpec((r, MOD_TN), lambda j: (0, j)),
        out_shape=jax.ShapeDtypeStruct((r, n), F32),
        compiler_params=_params("parallel"), name="mod_proj",
    )(cond, w_mod, b_mod.reshape(depth, 1, n))


IN_PAD = 2944
INPROJ_TM = 512
O_Q, O_K, O_V, O_U, O_G, O_Z, O_BA = 0, 512, 1024, 1536, 1792, 2560, 2816


def _rope_apply(x, cos, sin):
    lane = lax.broadcasted_iota(jnp.int32, x.shape, 1)
    up = pltpu.roll(x, LANES - 16, 1)
    dn = pltpu.roll(x, 16, 1)
    partner = jnp.where((lane & 31) < 16, up, dn)
    return x * cos + partner * sin


def _inproj_kernel(x_ref, mod_ref, nw_ref, w_ref, cos_ref, sin_ref,
                   q_ref, k_ref, v_ref, u_ref, ub_ref, g_ref, z_ref, ba_ref, *, rope):
    h = (_rms_rows(x_ref[...]) * nw_ref[...] * (1.0 + mod_ref[0, 1:2, :]) + mod_ref[0, 0:1, :]).astype(BF16)

    def proj(lo, hi):
        return jnp.dot(h, w_ref[:, lo:hi], preferred_element_type=F32)

    scale = DA_HEAD_DIM ** -0.5 * math.log2(math.e)
    for hd in range(DA_HEADS):
        lo = hd * LANES
        qs = proj(O_Q + lo, O_Q + lo + LANES)
        ks = proj(O_K + lo, O_K + lo + LANES)
        if rope:
            qs = _rope_apply(qs, cos_ref[...], sin_ref[...])
            ks = _rope_apply(ks, cos_ref[...], sin_ref[...])
        q_ref[:, lo:lo + LANES] = (qs * scale).astype(BF16)
        k_ref[:, lo:lo + LANES] = ks.astype(BF16)
    v_ref[...] = proj(O_V, O_U).astype(BF16)
    u = proj(O_U, O_G)
    u_ref[...] = u
    ub_ref[...] = u.astype(BF16)
    g_ref[...] = proj(O_G, O_Z)
    z_ref[...] = proj(O_Z, O_BA)
    ba_ref[...] = proj(O_BA, IN_PAD)


def inproj(x, mod, rows_per_mod, norm_w, w_pad, cos_t, sin_t, rope):
    t, d = x.shape
    tm = min(INPROJ_TM, t)
    tpm = rows_per_mod // tm
    npos = cos_t.shape[0] // tm
    row = lambda i: (i, 0)
    widths = (DA_QK_W, DA_QK_W, DA_WIDTH, S5_WIDTH, S5_WIDTH, GDN_QKV_W, GDN_WIDTH, LANES)
    dtypes = (BF16, BF16, BF16, F32, BF16, F32, F32, F32)
    return pl.pallas_call(
        functools.partial(_inproj_kernel, rope=rope),
        grid=(t // tm,),
        in_specs=[pl.BlockSpec((tm, d), row),
                  pl.BlockSpec((1, 6, d), lambda i: (i // tpm, 0, 0)),
                  pl.BlockSpec((1, d), lambda i: (0, 0)),
                  pl.BlockSpec((d, IN_PAD), lambda i: (0, 0)),
                  pl.BlockSpec((tm, LANES), lambda i: (i % npos, 0)),
                  pl.BlockSpec((tm, LANES), lambda i: (i % npos, 0))],
        out_specs=[pl.BlockSpec((tm, w), row) for w in widths],
        out_shape=[jax.ShapeDtypeStruct((t, w), dt) for w, dt in zip(widths, dtypes)],
        compiler_params=_params("parallel"), name="inproj",
    )(x, mod, norm_w.reshape(1, d), w_pad, cos_t, sin_t)


def _rope_tables(n):
    nf = DA_HEAD_DIM // 4
    t = jnp.arange(n, dtype=jnp.int32)
    inv = ROPE_THETA ** (-jnp.arange(nf, dtype=F32) / nf)
    ang_r = (t // GRID_W).astype(F32)[:, None] * inv
    ang_c = (t % GRID_W).astype(F32)[:, None] * inv
    cos64 = jnp.concatenate([jnp.cos(ang_r), jnp.cos(ang_r), jnp.cos(ang_c), jnp.cos(ang_c)], axis=-1)
    sin64 = jnp.concatenate([-jnp.sin(ang_r), jnp.sin(ang_r), -jnp.sin(ang_c), jnp.sin(ang_c)], axis=-1)
    return jnp.tile(cos64, (1, 2)), jnp.tile(sin64, (1, 2))


ATTN_TQ = 512
ATTN_KC = 512


def _attn_kernel(*refs, n_kv, kv_rows, chunks, lam_init):
    q_ref = refs[0]
    k_refs = refs[1:1 + n_kv]
    v_refs = refs[1 + n_kv:1 + 2 * n_kv]
    lam_ref, w_ref, o_ref, v1_sc = refs[1 + 2 * n_kv:]

    @pl.when(pl.program_id(2) == 0)
    def _():
        off = 0
        for ki, rows in enumerate(kv_rows):
            v1_sc[off:off + rows, :DA_V_DIM] = v_refs[ki][...]
            v1_sc[off:off + rows, DA_V_DIM:] = jnp.ones((rows, DA_V_DIM), BF16)
            off += rows

    q = q_ref[...]
    tq = q.shape[0]
    lane = lax.broadcasted_iota(jnp.int32, q.shape, 1)
    zero = jnp.zeros_like(q)
    qq = jnp.concatenate([jnp.where(lane < DA_HEAD_DIM, q, zero), jnp.where(lane >= DA_HEAD_DIM, q, zero)], axis=0)
    m = jnp.full((2 * tq, 1), -jnp.inf, F32)
    acc = jnp.zeros((2 * tq, 2 * DA_V_DIM), F32)
    def scores(chunk):
        ki, start, _, size = chunk
        return lax.dot_general(qq, k_refs[ki][start:start + size, :], NT_DIMS, preferred_element_type=F32)

    s_next = scores(chunks[0])
    for ci, (ki, start, off, size) in enumerate(chunks):
        s = s_next
        if ci + 1 < len(chunks):
            s_next = scores(chunks[ci + 1])
        m_new = jnp.maximum(m, jnp.max(s, axis=-1, keepdims=True))
        p = jnp.exp2(s - m_new).astype(BF16)
        acc = jnp.exp2(m - m_new) * acc + jnp.dot(p, v1_sc[off:off + size, :], preferred_element_type=F32)
        m = m_new
    o = acc[:, :DA_V_DIM] / acc[:, DA_V_DIM:]
    od = o[:tq] - lam_ref[...] * o[tq:]
    o_ref[...] = (_rms_rows(od) * w_ref[...] * (1.0 - lam_init)).astype(o_ref.dtype)


def diff_attention(q, ks, vs, kv_rows, q_rows, lam_row, subln_w, lam_init):
    t = q.shape[0]
    b = t // q_rows
    tq = min(ATTN_TQ, q_rows)
    nq = q_rows // tq
    chunks, off = [], 0
    for ki, rows in enumerate(kv_rows):
        kc = min(ATTN_KC, rows)
        chunks += [(ki, s, off + s, kc) for s in range(0, rows, kc)]
        off += rows
    qmap = lambda bi, h, qi: (bi * nq + qi, h)
    kvmap = lambda bi, h, qi: (bi, h)
    const = lambda bi, h, qi: (0, 0)
    return pl.pallas_call(
        functools.partial(_attn_kernel, n_kv=len(ks), kv_rows=tuple(kv_rows), chunks=tuple(chunks),
                          lam_init=lam_init),
        grid=(b, DA_HEADS, nq),
        in_specs=([pl.BlockSpec((tq, LANES), qmap)]
                  + [pl.BlockSpec((rows, LANES), kvmap) for rows in kv_rows] * 2
                  + [pl.BlockSpec((1, LANES), const)] * 2),
        out_specs=pl.BlockSpec((tq, LANES), qmap),
        out_shape=jax.ShapeDtypeStruct((t, DA_WIDTH), BF16),
        scratch_shapes=[pltpu.VMEM((off, 2 * DA_V_DIM), BF16)],
        compiler_params=_params("parallel", "parallel", "arbitrary"), name="diff_attention",
    )(q, *ks, *vs, lam_row, subln_w.reshape(1, LANES))


S5_LC = 64
S5_CW = S5_LC * S5_GROUP
S5_SW = 2 * S5_STATE


def _s5_discretize(lam_re, lam_im, log_step, b_re, b_im):
    lr, li = lam_re.astype(F32), lam_im.astype(F32)
    step = jnp.exp(log_step.astype(F32))[:, None]
    mag = jnp.exp(lr * step)
    ab_re, ab_im = mag * jnp.cos(li * step), mag * jnp.sin(li * step)
    den = lr * lr + li * li
    nr, ni = ab_re - 1.0, ab_im
    f_re = (nr * lr + ni * li) / den
    f_im = (ni * lr - nr * li) / den
    br, bi = b_re.astype(F32), b_im.astype(F32)
    bb_re = f_re[..., None] * br - f_im[..., None] * bi
    bb_im = f_re[..., None] * bi + f_im[..., None] * br
    return bb_re, bb_im


def _s5_tables(lam_re, lam_im, log_step, b_re, b_im, c_re, c_im):
    lc, g = S5_LC, S5_GROUPS
    bm, cm, mm, k1, k2 = [], [], [], [], []
    for d in range(2):
        lr, li = lam_re[d].astype(F32), lam_im[d].astype(F32)
        step = jnp.exp(log_step[d].astype(F32))[:, None]
        bb_re, bb_im = _s5_discretize(lam_re[d], lam_im[d], log_step[d], b_re[d], b_im[d])
        tau = jnp.arange(lc + 1, dtype=F32)[:, None, None]
        mag = jnp.exp(tau * (lr * step))
        ang = tau * (li * step)
        pr, pi = mag * jnp.cos(ang), mag * jnp.sin(ang)
        abr = pr[..., None] * bb_re - pi[..., None] * bb_im
        abi = pr[..., None] * bb_im + pi[..., None] * bb_re
        cr, ci = c_re[d].astype(F32), c_im[d].astype(F32)
        kern = (jnp.einsum('gkp,tgph->tgkh', cr, abr[:lc], precision=HP)
                - jnp.einsum('gkp,tgph->tgkh', ci, abi[:lc], precision=HP))
        rank = np.arange(lc) if d == 0 else lc - 1 - np.arange(lc)
        e_in = lc - 1 - rank
        bmat = jnp.concatenate([abr[e_in].transpose(1, 3, 0, 2), abi[e_in].transpose(1, 3, 0, 2)], axis=-1)
        bm.append(bmat.reshape(g, S5_CW, S5_SW))
        e_out = rank + 1
        pro, pio = pr[e_out][:, :, None, :], pi[e_out][:, :, None, :]
        car = cr[None] * pro - ci[None] * pio
        cai = cr[None] * pio + ci[None] * pro
        cmat = jnp.concatenate([car.transpose(1, 3, 2, 0), -cai.transpose(1, 3, 2, 0)], axis=1)
        cm.append(cmat.reshape(g, S5_SW, S5_CW))
        kt = kern.transpose(1, 3, 2, 0).astype(BF16)
        zpad = jnp.zeros((g, S5_GROUP, S5_GROUP, lc - 1), BF16)
        strip = jnp.concatenate([zpad, kt] if d == 0 else [kt[..., ::-1], zpad], axis=3)
        rows = [strip[..., lc - 1 - s:2 * lc - 1 - s] for s in range(lc)]
        mm.append(jnp.stack(rows, axis=2).reshape(g, S5_CW, S5_CW))
        alr, ali = pr[lc], pi[lc]
        k1.append(jnp.repeat(jnp.concatenate([alr, alr], axis=-1), 8, axis=0))
        k2.append(jnp.repeat(jnp.concatenate([-ali, ali], axis=-1), 8, axis=0))
    return (jnp.stack(bm).astype(BF16), jnp.stack(cm).astype(BF16), jnp.stack(mm).astype(BF16),
            jnp.stack(k1), jnp.stack(k2))


def _s5_local_kernel(uc_ref, ul_ref, bm_ref, s_ref):
    u = jnp.concatenate([uc_ref[0], ul_ref[0]], axis=0)
    s = jnp.dot(u, bm_ref[0, 0], preferred_element_type=F32)
    s_ref[0] = s.reshape(s_ref.shape[1:])


def _s5_carry_kernel(s_ref, k1_ref, k2_ref, xin_ref, *, n_ctx, n_chunks):
    d = pl.program_id(0)
    k1, k2 = k1_ref[0], k2_ref[0]

    def body(k, x):
        rev = jnp.where(k < n_ctx, n_ctx - 1 - k, n_chunks + n_ctx - 1 - k)
        c = jnp.where(d == 0, k, rev)
        xin_ref[0, c] = x
        return k1 * x + k2 * pltpu.roll(x, S5_STATE, 1) + s_ref[0, c]

    lax.fori_loop(0, n_chunks, body, jnp.zeros(k1.shape, F32))


def _s5_out_kernel(uc_ref, ul_ref, m_ref, xin_ref, cm_ref, yc_ref, yl_ref):
    u = jnp.concatenate([uc_ref[0], ul_ref[0]], axis=0)
    acc = jnp.dot(u, m_ref[0, 0], preferred_element_type=F32)
    acc += jnp.dot(u, m_ref[1, 0], preferred_element_type=F32)
    for d in range(2):
        xin = xin_ref[d].reshape(u.shape[0], S5_SW).astype(BF16)
        acc += jnp.dot(xin, cm_ref[d, 0], preferred_element_type=F32)
    rc = yc_ref.shape[1]
    yc_ref[0] = acc[:rc].astype(yc_ref.dtype)
    yl_ref[0] = acc[rc:].astype(yl_ref.dtype)


def s5_scan(u_ctx, u_lat, b, tables):
    bmat, cmat, mmat, k1, k2 = tables
    g, lc = S5_GROUPS, S5_LC
    n_ctx, n_lat = u_ctx.shape[0] // b, u_lat.shape[0] // b
    assert b == 8 and n_lat % lc == 0 and n_ctx % lc == 0
    rc, rl = n_ctx // lc * b, n_lat // lc * b
    nch = (n_ctx + n_lat) // lc

    def to_groups(x, rows):
        x = x.reshape(b, rows // lc, lc, g, S5_GROUP).transpose(3, 1, 0, 4, 2)
        return x.reshape(g, rows // lc * b, S5_CW)

    def from_groups(y, rows):
        return y.reshape(g, rows // lc, b, S5_GROUP, lc).transpose(2, 1, 4, 0, 3).reshape(b * rows, S5_WIDTH)

    ugc, ugl = to_groups(u_ctx, n_ctx), to_groups(u_lat, n_lat)
    s = pl.pallas_call(
        _s5_local_kernel,
        grid=(2, g),
        in_specs=[pl.BlockSpec((1, rc, S5_CW), lambda d, gi: (gi, 0, 0)),
                  pl.BlockSpec((1, rl, S5_CW), lambda d, gi: (gi, 0, 0)),
                  pl.BlockSpec((1, 1, S5_CW, S5_SW), lambda d, gi: (d, gi, 0, 0))],
        out_specs=pl.BlockSpec((1, nch, b, S5_SW), lambda d, gi: (d, 0, gi, 0)),
        out_shape=jax.ShapeDtypeStruct((2, nch, g * b, S5_SW), F32),
        compiler_params=_params("arbitrary", "arbitrary"), name="s5_local",
    )(ugc, ugl, bmat)
    xin = pl.pallas_call(
        functools.partial(_s5_carry_kernel, n_ctx=n_ctx // lc, n_chunks=nch),
        grid=(2,),
        in_specs=[pl.BlockSpec((1, nch, g * b, S5_SW), lambda d: (d, 0, 0, 0)),
                  pl.BlockSpec((1, g * b, S5_SW), lambda d: (d, 0, 0)),
                  pl.BlockSpec((1, g * b, S5_SW), lambda d: (d, 0, 0))],
        out_specs=pl.BlockSpec((1, nch, g * b, S5_SW), lambda d: (d, 0, 0, 0)),
        out_shape=jax.ShapeDtypeStruct((2, nch, g * b, S5_SW), F32),
        compiler_params=_params("arbitrary"), name="s5_carry",
    )(s, k1, k2)
    yc, yl = pl.pallas_call(
        _s5_out_kernel,
        grid=(g,),
        in_specs=[pl.BlockSpec((1, rc, S5_CW), lambda gi: (gi, 0, 0)),
                  pl.BlockSpec((1, rl, S5_CW), lambda gi: (gi, 0, 0)),
                  pl.BlockSpec((2, 1, S5_CW, S5_CW), lambda gi: (0, gi, 0, 0)),
                  pl.BlockSpec((2, nch, b, S5_SW), lambda gi: (0, 0, gi, 0)),
                  pl.BlockSpec((2, 1, S5_SW, S5_CW), lambda gi: (0, gi, 0, 0))],
        out_specs=[pl.BlockSpec((1, rc, S5_CW), lambda gi: (gi, 0, 0)),
                   pl.BlockSpec((1, rl, S5_CW), lambda gi: (gi, 0, 0))],
        out_shape=[jax.ShapeDtypeStruct((g, rc, S5_CW), BF16), jax.ShapeDtypeStruct((g, rl, S5_CW), BF16)],
        compiler_params=_params("arbitrary"), name="s5_out",
    )(ugc, ugl, mmat, xin, cmat)
    return from_groups(yc, n_ctx), from_groups(yl, n_lat)


def _gdn_prep_kernel(x_ref, w_ref, o_ref):
    j = pl.program_id(1)
    x = x_ref[...]
    n = x.shape[0]
    row = lax.broadcasted_iota(jnp.int32, x.shape, 0)
    half = GDN_CONV // 2
    acc = x * w_ref[half:half + 1, :]
    for sh in range(1, half + 1):
        acc += jnp.where(row >= sh, pltpu.roll(x, sh, 0), 0.0) * w_ref[half - sh:half - sh + 1, :]
        acc += jnp.where(row < n - sh, pltpu.roll(x, n - sh, 0), 0.0) * w_ref[half + sh:half + sh + 1, :]
    a = acc * _sigmoid(acc)
    lane = lax.broadcasted_iota(jnp.int32, x.shape, 1)
    lo = lane < GDN_DK
    sq = a * a
    s_lo = jnp.sum(jnp.where(lo, sq, 0.0), axis=-1, keepdims=True)
    s_hi = jnp.sum(jnp.where(lo, 0.0, sq), axis=-1, keepdims=True)
    nrm = a * lax.rsqrt(jnp.where(lo, s_lo, s_hi) + EPS)
    q_blocks = GDN_HEADS * GDN_DK // LANES
    nrm = nrm * jnp.where(j < q_blocks, GDN_DK ** -0.5, 1.0)
    o_ref[...] = jnp.where(j < 2 * q_blocks, nrm, a)


def gdn_prep(qkv, conv_w, seg):
    t, w = qkv.shape
    return pl.pallas_call(
        _gdn_prep_kernel,
        grid=(t // seg, w // LANES),
        in_specs=[pl.BlockSpec((seg, LANES), lambda s, j: (s, j)),
                  pl.BlockSpec((GDN_CONV, LANES), lambda s, j: (0, j))],
        out_specs=pl.BlockSpec((seg, LANES), lambda s, j: (s, j)),
        out_shape=jax.ShapeDtypeStruct((t, w), F32),
        compiler_params=_params("parallel", "arbitrary"), name="gdn_prep",
    )(qkv, conv_w)


def _bd(x):
    x2 = jnp.concatenate([x, x], axis=0)
    r = lax.broadcasted_iota(jnp.int32, x2.shape, 0)
    l = lax.broadcasted_iota(jnp.int32, x2.shape, 1)
    return jnp.where((r >> 6) == (l >> 6), x2, jnp.zeros_like(x2))


def _mm(a, b):
    return jnp.dot(a.astype(BF16), b.astype(BF16), preferred_element_type=F32)


def _dot01(m01, x):
    hi = x.astype(BF16)
    r1 = x - hi.astype(F32)
    mid = r1.astype(BF16)
    lo = (r1 - mid.astype(F32)).astype(BF16)
    dot = lambda p: jnp.dot(m01, p, preferred_element_type=F32)
    return (dot(lo) + dot(mid)) + dot(hi)


def _gdn_chunks(insts):
    c = GDN_CHUNK
    n = len(insts)
    every = range(n)
    q, k, v, beta, gcol, s_bd, rev = (list(t) for t in zip(*insts))
    i = lax.broadcasted_iota(jnp.int32, (c, LANES), 0)
    j = lax.broadcasted_iota(jnp.int32, (c, LANES), 1) & (c - 1)
    ti = lax.broadcasted_iota(jnp.int32, (c, c), 0)
    tj = lax.broadcasted_iota(jnp.int32, (c, c), 1)
    causal = [i <= j if r else i >= j for r in rev]
    strict = [i < j if r else i > j for r in rev]
    upto = [i >= j if r else i <= j for r in rev]
    lmat = [(tj >= ti if r else tj <= ti).astype(BF16) for r in rev]
    ones = jnp.ones((c, c), BF16)
    eye = jnp.where(i == j, 1.0, 0.0)
    lg = [_dot01(lmat[t], gcol[t]) for t in every]
    rg = [_dot01(ones, jnp.where(upto[t], gcol[t], 0.0)) for t in every]
    decay = [jnp.where(causal[t], jnp.exp(jnp.where(causal[t], lg[t] - rg[t], 0.0)), 0.0) for t in every]
    kb = [k[t] * beta[t] for t in every]
    k_bd = [_bd(k[t].astype(BF16)) for t in every]
    a = [jnp.where(strict[t], lax.dot_general(kb[t].astype(BF16), k_bd[t], NT_DIMS, preferred_element_type=F32)
                   * decay[t], 0.0) for t in every]
    qk = [lax.dot_general(q[t].astype(BF16), k_bd[t], NT_DIMS, preferred_element_type=F32) * decay[t] for t in every]
    eg = [jnp.exp(lg[t]) for t in every]
    same = [(i >> sh) == (j >> sh) for sh in (3, 4, 5)]
    x = [jnp.where(same[0], a[t], 0.0) for t in every]
    p = [eye - x[t] for t in every]
    for _ in range(2):
        x = [_mm(x[t], _bd(x[t])) for t in every]
        p = [p[t] + _mm(p[t], _bd(x[t])) for t in every]
    for lvl in range(3):
        inner = same[lvl]
        join = jnp.logical_not(inner) if lvl == 2 else jnp.logical_and(same[lvl + 1], jnp.logical_not(inner))
        tl = [_mm(p[t], _bd(jnp.where(join, a[t], 0.0))) for t in every]
        p = [p[t] - _mm(tl[t], _bd(p[t])) for t in every]
    u = [_mm(p[t], _bd(v[t] * beta[t])) for t in every]
    w = [_mm(p[t], _bd(kb[t] * eg[t])) for t in every]
    v_new = [u[t] - _mm(w[t], s_bd[t]) for t in every]
    o_state = [_mm(q[t] * eg[t], s_bd[t]) for t in every]
    o = [o_state[t] + _mm(qk[t], _bd(v_new[t])) for t in every]
    g_last = [lg[t][0:1, :] if rev[t] else lg[t][c - 1:c, :] for t in every]
    k_dec = [k[t] * jnp.exp(g_last[t] - lg[t]) for t in every]
    upd = [jnp.dot(k_dec[t].T.astype(BF16), v_new[t].astype(BF16), preferred_element_type=F32) for t in every]
    r2 = lax.broadcasted_iota(jnp.int32, (LANES, LANES), 0)
    l2 = lax.broadcasted_iota(jnp.int32, (LANES, LANES), 1)
    diag = (r2 >> 6) == (l2 >> 6)
    s_new = [s_bd[t] * jnp.exp(g_last[t]) + jnp.where(diag, upd[t], 0.0) for t in every]
    return list(zip(o, s_new))


GDN_BATCH_UNROLL = 4


def _gdn_scan_kernel(qf, kf, vf, baf, qr, kr, vr, bar, alog_ref, dtb_ref, s0_ref,
                     of_ref, or_ref, sfin_ref, s_sc):
    c = pl.program_id(0)

    @pl.when(c == 0)
    def _():
        s_sc[...] = s0_ref[...]

    lane = lax.broadcasted_iota(jnp.int32, (GDN_CHUNK, LANES), 1)
    first = lane < GDN_DK
    ins = ((qf, kf, vf, baf, of_ref), (qr, kr, vr, bar, or_ref))

    def body(it, carry):
        work = []
        for bb in range(GDN_BATCH_UNROLL):
            b = it * GDN_BATCH_UNROLL + bb
            for d in range(2):
                q_ref, k_ref, v_ref, ba_ref, o_ref = ins[d]
                ba = ba_ref[b]
                bsig = _sigmoid(ba)
                sp = ba + dtb_ref[...]
                gall = -jnp.exp(alog_ref[...]) * (jnp.maximum(sp, 0.0) + jnp.log(1.0 + jnp.exp(-jnp.abs(sp))))
                for hp in range(GDN_HEADS // 2):
                    col = d * GDN_HEADS + 2 * hp
                    beta = jnp.where(first, bsig[:, col:col + 1], bsig[:, col + 1:col + 2])
                    gcol = jnp.where(first, gall[:, 8 + col:9 + col], gall[:, 9 + col:10 + col])
                    sl = slice(hp * LANES, (hp + 1) * LANES)
                    work.append((b, d, hp, sl, o_ref, (q_ref[b, :, sl], k_ref[b, :, sl], v_ref[b, :, sl],
                                                       beta, gcol, s_sc[b, d, hp], d == 1)))
        done = _gdn_chunks([args for (_, _, _, _, _, args) in work])
        for (b, d, hp, sl, o_ref, _), (o, s_new) in zip(work, done):
            o_ref[b, :, sl] = o
            s_sc[b, d, hp] = s_new
        return carry

    lax.fori_loop(0, s_sc.shape[0] // GDN_BATCH_UNROLL, body, 0)

    @pl.when(c == pl.num_programs(0) - 1)
    def _():
        sfin_ref[...] = s_sc[...]


def gdn_scan(qkvn, ba, s0, alog_row, dtb_row):
    b, l, _ = qkvn.shape
    nch = l // GDN_CHUNK
    blk = (b, GDN_CHUNK, GDN_WIDTH)
    fwd = lambda col: (lambda c: (0, c, col))
    bwd = lambda col: (lambda c: (0, nch - 1 - c, col))
    st = pl.BlockSpec(s0.shape, lambda c: (0, 0, 0, 0, 0))
    return pl.pallas_call(
        _gdn_scan_kernel,
        grid=(nch,),
        in_specs=[pl.BlockSpec(blk, fwd(0)), pl.BlockSpec(blk, fwd(1)), pl.BlockSpec(blk, fwd(2)),
                  pl.BlockSpec((b, GDN_CHUNK, LANES), fwd(0)),
                  pl.BlockSpec(blk, bwd(0)), pl.BlockSpec(blk, bwd(1)), pl.BlockSpec(blk, bwd(2)),
                  pl.BlockSpec((b, GDN_CHUNK, LANES), bwd(0)),
                  pl.BlockSpec((1, LANES), lambda c: (0, 0)), pl.BlockSpec((1, LANES), lambda c: (0, 0)), st],
        out_specs=[pl.BlockSpec(blk, fwd(0)), pl.BlockSpec(blk, bwd(0)), st],
        out_shape=[jax.ShapeDtypeStruct((b, l, GDN_WIDTH), F32)] * 2 + [jax.ShapeDtypeStruct(s0.shape, F32)],
        scratch_shapes=[pltpu.VMEM(s0.shape, F32)],
        compiler_params=_params("arbitrary"), name="gdn_scan",
    )(qkvn, qkvn, qkvn, ba, qkvn, qkvn, qkvn, ba, alog_row, dtb_row, s0)


OUTPROJ_TM = 512


def _outproj_kernel(x_ref, mod_ref, da_ref, ys_ref, u_ref, dsk_ref, wg_ref, bg_ref,
                    of_ref, or_ref, z_ref, gnw_ref, avg_ref, w_ref, o_ref):
    y = ys_ref[...] + u_ref[...] * dsk_ref[...]
    zz = 0.5 * y * (1.0 + jnp.tanh(math.sqrt(2.0 / math.pi) * (y + 0.044715 * (y * y * y))))
    glu = zz * _sigmoid(jnp.dot(zz.astype(BF16), wg_ref[...], preferred_element_type=F32) + bg_ref[...])
    o = of_ref[...] + or_ref[...]
    ms = jnp.dot(o * o, avg_ref[...], precision=HP, preferred_element_type=F32)
    z = z_ref[...]
    gd = o * lax.rsqrt(ms + EPS) * gnw_ref[...] * (z * _sigmoid(z))
    acc = jnp.dot(da_ref[...], w_ref[0:DA_WIDTH, :], preferred_element_type=F32)
    acc += jnp.dot(glu.astype(BF16), w_ref[DA_WIDTH:DA_WIDTH + S5_WIDTH, :], preferred_element_type=F32)
    acc += jnp.dot(gd.astype(BF16), w_ref[DA_WIDTH + S5_WIDTH:, :], preferred_element_type=F32)
    o_ref[...] = x_ref[...] + mod_ref[0, 2:3, :] * acc


def outproj(x, mod, rows_per_mod, da, ys, u, dsk, w_glu, b_glu, o_f, o_r, z, gnw, avg, w_out):
    t, d = x.shape
    tm = min(OUTPROJ_TM, t)
    tpm = rows_per_mod // tm
    row = lambda i: (i, 0)
    const = lambda i: (0, 0)
    sw = S5_WIDTH
    return pl.pallas_call(
        _outproj_kernel,
        grid=(t // tm,),
        in_specs=[pl.BlockSpec((tm, d), row),
                  pl.BlockSpec((1, 6, d), lambda i: (i // tpm, 0, 0)),
                  pl.BlockSpec((tm, DA_WIDTH), row),
                  pl.BlockSpec((tm, sw), row), pl.BlockSpec((tm, sw), row),
                  pl.BlockSpec((1, sw), const), pl.BlockSpec((sw, sw), const), pl.BlockSpec((1, sw), const),
                  pl.BlockSpec((tm, GDN_WIDTH), row), pl.BlockSpec((tm, GDN_WIDTH), row),
                  pl.BlockSpec((tm, GDN_WIDTH), row),
                  pl.BlockSpec((1, GDN_WIDTH), const), pl.BlockSpec((GDN_WIDTH, GDN_WIDTH), const),
                  pl.BlockSpec((D_MIX, d), const)],
        out_specs=pl.BlockSpec((tm, d), row),
        out_shape=jax.ShapeDtypeStruct((t, d), F32),
        compiler_params=_params("parallel"), name="outproj",
    )(x, mod, da, ys, u, dsk, w_glu, b_glu, o_f, o_r, z, gnw, avg, w_out)


MOE_TM = 1024
MOE_EP = 5


def _pick_lowest(cur, idx, sentinel, axis):
    m = jnp.max(cur, axis=axis, keepdims=True)
    first = jnp.min(jnp.where(cur == m, idx, sentinel), axis=axis, keepdims=True)
    return idx == first


def _route(logits_t, bias):
    tm = logits_t.shape[1]
    neg = jnp.float32(-jnp.inf)
    scores = jax.nn.sigmoid(logits_t)
    biased = scores + bias
    b3 = biased.reshape(N_GROUPS, GROUP_SIZE, tm)
    eidx = lax.broadcasted_iota(jnp.int32, b3.shape, 1)
    m1 = jnp.max(b3, axis=1, keepdims=True)
    p1 = _pick_lowest(b3, eidx, GROUP_SIZE, 1)
    m2 = jnp.max(jnp.where(p1, neg, b3), axis=1, keepdims=True)
    gs = (m1 + m2).reshape(N_GROUPS, tm)
    gidx = lax.broadcasted_iota(jnp.int32, gs.shape, 0)
    gsel = jnp.zeros(gs.shape, jnp.bool_)
    cur = gs
    for _ in range(TOPK_GROUPS):
        pick = _pick_lowest(cur, gidx, N_GROUPS, 0)
        gsel = jnp.logical_or(gsel, pick)
        cur = jnp.where(pick, neg, cur)
    emask = jnp.broadcast_to(gsel.reshape(N_GROUPS, 1, tm), b3.shape)
    cur = jnp.where(emask, b3, neg).reshape(N_EXPERTS, tm)
    ridx = lax.broadcasted_iota(jnp.int32, cur.shape, 0)
    sel = jnp.zeros(cur.shape, jnp.bool_)
    for _ in range(TOP_K):
        pick = _pick_lowest(cur, ridx, N_EXPERTS, 0)
        sel = jnp.logical_or(sel, pick)
        cur = jnp.where(pick, neg, cur)
    w = jnp.where(sel, scores, 0.0)
    return w / jnp.sum(w, axis=0, keepdims=True) * ROUTED_SCALE


def _moe_kernel(x_ref, mod_ref, nw_ref, wr_ref, rb_ref, w13_ref, w2_ref, fnw_ref, o_ref,
                h_sc, gate_sc, acc_sc, hid_sc, *, final_norm):
    e = pl.program_id(1)
    n_e = pl.num_programs(1)
    ep = w13_ref.shape[0]

    @pl.when(e == 0)
    def _():
        h = _rms_rows(x_ref[...]) * nw_ref[...] * (1.0 + mod_ref[0, 4:5, :]) + mod_ref[0, 3:4, :]
        h_sc[...] = h.astype(BF16)
        logits_t = lax.dot_general(wr_ref[...], h, NT_DIMS, precision=HP, preferred_element_type=F32)
        w = _route(logits_t, rb_ref[...])
        tm = w.shape[1]
        row = lax.broadcasted_iota(jnp.int32, (LANES - N_EXPERTS, tm), 0)
        shared = jnp.where(row == 0, 1.0, 0.0).astype(F32)
        gate_sc[...] = jnp.concatenate([w, shared], axis=0).T
        acc_sc[...] = jnp.zeros_like(acc_sc)

    h = h_sc[...]
    lane = lax.broadcasted_iota(jnp.int32, (1, LANES), 1)
    for j in range(ep):
        ab = jnp.dot(h, w13_ref[j], preferred_element_type=F32)
        a, b = ab[:, :MOE_FFN], ab[:, MOE_FFN:]
        g = jnp.sum(jnp.where(lane == e * ep + j, gate_sc[...], 0.0), axis=-1, keepdims=True)
        hid_sc[:, j * MOE_FFN:(j + 1) * MOE_FFN] = (a * _sigmoid(a) * b * g).astype(BF16)
    w2 = w2_ref[...].reshape(ep * MOE_FFN, w2_ref.shape[2])
    acc_sc[...] += jnp.dot(hid_sc[...], w2, preferred_element_type=F32)

    @pl.when(e == n_e - 1)
    def _():
        y = x_ref[...] + mod_ref[0, 5:6, :] * acc_sc[...]
        if final_norm:
            y = _rms_rows(y) * fnw_ref[...]
        o_ref[...] = y


def moe_sublayer(x, mod, rows_per_mod, norm_w, w_router_t, router_bias, w13, w2, final_w, final_norm):
    t, d = x.shape
    tm = min(MOE_TM, rows_per_mod)
    assert t % tm == 0 and rows_per_mod % tm == 0
    n_slots = w13.shape[0]
    ep = MOE_EP
    assert n_slots % ep == 0
    tiles_per_mod = rows_per_mod // tm
    return pl.pallas_call(
        functools.partial(_moe_kernel, final_norm=final_norm),
        grid=(t // tm, n_slots // ep),
        in_specs=[
            pl.BlockSpec((tm, d), lambda i, e: (i, 0)),
            pl.BlockSpec((1, 6, d), lambda i, e: (i // tiles_per_mod, 0, 0)),
            pl.BlockSpec((1, d), lambda i, e: (0, 0)),
            pl.BlockSpec((N_EXPERTS, d), lambda i, e: (0, 0)),
            pl.BlockSpec((N_EXPERTS, 1), lambda i, e: (0, 0)),
            pl.BlockSpec((ep, d, 2 * MOE_FFN), lambda i, e: (e, 0, 0)),
            pl.BlockSpec((ep, MOE_FFN, d), lambda i, e: (e, 0, 0)),
            pl.BlockSpec((1, d), lambda i, e: (0, 0)),
        ],
        out_specs=pl.BlockSpec((tm, d), lambda i, e: (i, 0)),
        out_shape=jax.ShapeDtypeStruct((t, d), F32),
        scratch_shapes=[
            pltpu.VMEM((tm, d), BF16),
            pltpu.VMEM((tm, LANES), F32),
            pltpu.VMEM((tm, d), F32),
            pltpu.VMEM((tm, ep * MOE_FFN), BF16),
        ],
        compiler_params=_params("parallel", "arbitrary"), name="moe_sublayer",
    )(x, mod, norm_w.reshape(1, d), w_router_t, router_bias.reshape(N_EXPERTS, 1), w13, w2, final_w.reshape(1, d))


def _moe_weights(w1, w3, w2, ws1, ws3, ws2):
    w13 = jnp.concatenate([jnp.concatenate([w1, w3], axis=-1),
                           jnp.concatenate([ws1, ws3], axis=-1)[None]], axis=0).astype(BF16)
    w2a = jnp.concatenate([w2, ws2[None]], axis=0).astype(BF16)
    return w13, w2a


def kernel(x, c, ctx, c_ctx, norm1_w, norm2_w, w_mod, b_mod, w_in, w_out, da_lambda, da_subln_w,
           s5_lam_re, s5_lam_im, s5_log_step, s5_b_re, s5_b_im, s5_c_re, s5_c_im, s5_d, s5_w_glu, s5_b_glu,
           gdn_conv_w, gdn_a_log, gdn_dt_bias, gdn_norm_w,
           moe_w_router, moe_router_bias, moe_w1, moe_w3, moe_w2, moe_ws1, moe_ws3, moe_ws2,
           final_norm_w):
    b, n, d = x.shape
    nc = ctx.shape[1]
    cos_t, sin_t = _rope_tables(n)
    cond = jnp.zeros((2 * b, d), F32).at[:b].set(c).at[b].set(c_ctx)
    xl = x.reshape(b * n, d)
    xc = ctx.reshape(b * nc, d)
    head_avg = jnp.kron(jnp.eye(GDN_HEADS, dtype=F32), jnp.full((GDN_DV, GDN_DV), 1.0 / GDN_DV, F32))
    s_zero = jnp.zeros((b, 2, GDN_HEADS // 2, LANES, LANES), F32)
    for i in range(DEPTH):
        ctx_out = i < DEPTH - 1
        last = i == DEPTH - 1
        lam_init = 0.8 - 0.6 * math.exp(-0.3 * i)
        mod_all = mod_proj(cond, w_mod, b_mod, i).reshape(2 * b, 6, d)
        mod, modc = mod_all[:b], mod_all[b:b + 1]

        w_in_p = jnp.pad(w_in[i], ((0, 0), (0, IN_PAD - IN_WIDTH))).astype(BF16)
        q, k, v, u, ub, gq, z, ba = inproj(xl, mod, n, norm1_w[i], w_in_p, cos_t, sin_t, rope=True)
        qc, kc, vc, uc, ubc, gqc, zc, bac = inproj(xc, modc, b * nc, norm1_w[i], w_in_p, cos_t, sin_t, rope=False)

        lq1, lk1, lq2, lk2 = da_lambda[i].astype(F32)
        lam = jnp.exp(jnp.sum(lq1 * lk1)) - jnp.exp(jnp.sum(lq2 * lk2)) + lam_init
        lam_row = jnp.full((1, LANES), lam, F32)
        da = diff_attention(q, [kc, k], [vc, v], (nc, n), n, lam_row, da_subln_w[i], lam_init)

        tables = _s5_tables(s5_lam_re[i], s5_lam_im[i], s5_log_step[i], s5_b_re[i], s5_b_im[i],
                            s5_c_re[i], s5_c_im[i])
        ysc, ysl = s5_scan(ubc, ub, b, tables)

        alog_row = jnp.zeros((1, LANES), F32).at[0, 8:16].set(gdn_a_log[i].astype(F32).reshape(-1))
        dtb_row = jnp.zeros((1, LANES), F32).at[0, 8:16].set(gdn_dt_bias[i].astype(F32).reshape(-1))
        gn = gdn_prep(gq, gdn_conv_w[i], n)
        gnc = gdn_prep(gqc, gdn_conv_w[i], nc)
        ofc, orc, s_ctx = gdn_scan(gnc.reshape(b, nc, GDN_QKV_W), bac.reshape(b, nc, LANES), s_zero, alog_row, dtb_row)
        of, orv, _ = gdn_scan(gn.reshape(b, n, GDN_QKV_W), ba.reshape(b, n, LANES), s_ctx, alog_row, dtb_row)

        w_out_b = w_out[i].astype(BF16)
        dsk = s5_d[i].astype(F32).reshape(1, S5_WIDTH)
        wg = s5_w_glu[i].astype(BF16)
        bg = s5_b_glu[i].astype(F32).reshape(1, S5_WIDTH)
        gnw = jnp.tile(gdn_norm_w[i].astype(F32), GDN_HEADS).reshape(1, GDN_WIDTH)
        xl = outproj(xl, mod, n, da, ysl, u, dsk, wg, bg,
                     of.reshape(b * n, GDN_WIDTH), orv.reshape(b * n, GDN_WIDTH), z, gnw, head_avg, w_out_b)
        if ctx_out:
            dac = diff_attention(qc, [kc], [vc], (nc,), nc, lam_row, da_subln_w[i], lam_init)
            xc = outproj(xc, modc, b * nc, dac, ysc, uc, dsk, wg, bg,
                         ofc.reshape(b * nc, GDN_WIDTH), orc.reshape(b * nc, GDN_WIDTH), zc, gnw, head_avg, w_out_b)

        w13, w2a = _moe_weights(moe_w1[i], moe_w3[i], moe_w2[i], moe_ws1[i], moe_ws3[i], moe_ws2[i])
        wr_t = moe_w_router[i].T
        xl = moe_sublayer(xl, mod, n, norm2_w[i], wr_t, moe_router_bias[i], w13, w2a, final_norm_w, last)
        if ctx_out:
            xc = moe_sublayer(xc, modc, b * nc, norm2_w[i], wr_t, moe_router_bias[i], w13, w2a, final_norm_w, False)
    return xl.reshape(b, n, d)
```

```python
import functools
import math

import jax
import jax.numpy as jnp
import numpy as np
from jax import lax
from jax.experimental import pallas as pl
from jax.experimental.pallas import tpu as pltpu

F32 = jnp.float32
BF16 = jnp.bfloat16

D_MODEL = 1024
DEPTH = 2
GRID_W = 64
EPS = 1e-6

DA_HEADS = 4
DA_HEAD_DIM = D_MODEL // 16
DA_V_DIM = 2 * DA_HEAD_DIM
DA_WIDTH = DA_HEADS * DA_V_DIM
ROPE_THETA = 10000.0

S5_WIDTH = D_MODEL // 4
S5_GROUP = 16
S5_GROUPS = S5_WIDTH // S5_GROUP
S5_STATE = 64

GDN_HEADS = 4
GDN_DK = D_MODEL // 16
GDN_DV = D_MODEL // 16
GDN_WIDTH = GDN_HEADS * GDN_DV
GDN_CONV = 5
GDN_CHUNK = 64

D_MIX = DA_WIDTH + S5_WIDTH + GDN_WIDTH
DA_QK_W = 2 * DA_HEADS * DA_HEAD_DIM
GDN_QKV_W = 2 * GDN_HEADS * GDN_DK + GDN_HEADS * GDN_DV
IN_SIZES = (DA_QK_W, DA_QK_W, DA_WIDTH, S5_WIDTH, GDN_QKV_W, GDN_WIDTH, 2 * GDN_HEADS, 2 * GDN_HEADS)
IN_WIDTH = sum(IN_SIZES)

N_EXPERTS = 64
TOP_K = 8
N_GROUPS = 8
GROUP_SIZE = N_EXPERTS // N_GROUPS
TOPK_GROUPS = 4
MOE_FFN = D_MODEL // 4
ROUTED_SCALE = 2.5

LANES = 128
VMEM_LIMIT_BYTES = 56 * 1024 * 1024

HP = lax.Precision.HIGHEST
NT_DIMS = (((1,), (1,)), ((), ()))


def _params(*sem):
    return pltpu.CompilerParams(dimension_semantics=sem, vmem_limit_bytes=VMEM_LIMIT_BYTES)


def _sigmoid(x):
    return 0.5 * (1.0 + jnp.tanh(0.5 * x))


def _rms_rows(x):
    return x * lax.rsqrt(jnp.mean(x * x, axis=-1, keepdims=True) + EPS)


MOD_TN = 768


def _mod_kernel(c_ref, w_ref, b_ref, o_ref):
    c = c_ref[...]
    act = c * _sigmoid(c)
    o_ref[...] = jnp.dot(act, w_ref[...], precision=HP, preferred_element_type=F32) + b_ref[...]


def mod_proj(cond, w_mod, b_mod, layer):
    r, d = cond.shape
    depth, _, n = w_mod.shape
    return pl.pallas_call(
        _mod_kernel,
        grid=(n // MOD_TN,),
        in_specs=[pl.BlockSpec((r, d), lambda j: (0, 0)),
                  pl.BlockSpec((None, d, MOD_TN), lambda j: (layer, 0, j)),
                  pl.BlockSpec((None, 1, MOD_TN), lambda j: (layer, 0, j))],
        out_specs=pl.BlockSpec((r, MOD_TN), lambda j: (0, j)),
        out_shape=jax.ShapeDtypeStruct((r, n), F32),
        compiler_params=_params("parallel"), name="mod_proj",
    )(cond, w_mod, b_mod.reshape(depth, 1, n))


IN_PAD = 2944
INPROJ_TM = 512
O_Q, O_K, O_V, O_U, O_G, O_Z, O_BA = 0, 512, 1024, 1536, 1792, 2560, 2816


def _rope_apply(x, cos, sin):
    lane = lax.broadcasted_iota(jnp.int32, x.shape, 1)
    up = pltpu.roll(x, LANES - 16, 1)
    dn = pltpu.roll(x, 16, 1)
    partner = jnp.where((lane & 31) < 16, up, dn)
    return x * cos + partner * sin


def _inproj_kernel(x_ref, mod_ref, nw_ref, w_ref, cos_ref, sin_ref,
                   q_ref, k_ref, v_ref, u_ref, ub_ref, g_ref, z_ref, ba_ref, *, rope):
    h = (_rms_rows(x_ref[...]) * nw_ref[...] * (1.0 + mod_ref[0, 1:2, :]) + mod_ref[0, 0:1, :]).astype(BF16)

    def proj(lo, hi):
        return jnp.dot(h, w_ref[:, lo:hi], preferred_element_type=F32)

    scale = DA_HEAD_DIM ** -0.5 * math.log2(math.e)
    for hd in range(DA_HEADS):
        lo = hd * LANES
        qs = proj(O_Q + lo, O_Q + lo + LANES)
        ks = proj(O_K + lo, O_K + lo + LANES)
        if rope:
            qs = _rope_apply(qs, cos_ref[...], sin_ref[...])
            ks = _rope_apply(ks, cos_ref[...], sin_ref[...])
        q_ref[:, lo:lo + LANES] = (qs * scale).astype(BF16)
        k_ref[:, lo:lo + LANES] = ks.astype(BF16)
    v_ref[...] = proj(O_V, O_U).astype(BF16)
    u = proj(O_U, O_G)
    u_ref[...] = u
    ub_ref[...] = u.astype(BF16)
    g_ref[...] = proj(O_G, O_Z)
    z_ref[...] = proj(O_Z, O_BA)
    ba_ref[...] = proj(O_BA, IN_PAD)


def inproj(x, mod, rows_per_mod, norm_w, w_pad, cos_t, sin_t, rope):
    t, d = x.shape
    tm = min(INPROJ_TM, t)
    tpm = rows_per_mod // tm
    npos = cos_t.shape[0] // tm
    row = lambda i: (i, 0)
    widths = (DA_QK_W, DA_QK_W, DA_WIDTH, S5_WIDTH, S5_WIDTH, GDN_QKV_W, GDN_WIDTH, LANES)
    dtypes = (BF16, BF16, BF16, F32, BF16, F32, F32, F32)
    return pl.pallas_call(
        functools.partial(_inproj_kernel, rope=rope),
        grid=(t // tm,),
        in_specs=[pl.BlockSpec((tm, d), row),
                  pl.BlockSpec((1, 6, d), lambda i: (i // tpm, 0, 0)),
                  pl.BlockSpec((1, d), lambda i: (0, 0)),
                  pl.BlockSpec((d, IN_PAD), lambda i: (0, 0)),
                  pl.BlockSpec((tm, LANES), lambda i: (i % npos, 0)),
                  pl.BlockSpec((tm, LANES), lambda i: (i % npos, 0))],
        out_specs=[pl.BlockSpec((tm, w), row) for w in widths],
        out_shape=[jax.ShapeDtypeStruct((t, w), dt) for w, dt in zip(widths, dtypes)],
        compiler_params=_params("parallel"), name="inproj",
    )(x, mod, norm_w.reshape(1, d), w_pad, cos_t, sin_t)


def _rope_tables(n):
    nf = DA_HEAD_DIM // 4
    t = jnp.arange(n, dtype=jnp.int32)
    inv = ROPE_THETA ** (-jnp.arange(nf, dtype=F32) / nf)
    ang_r = (t // GRID_W).astype(F32)[:, None] * inv
    ang_c = (t % GRID_W).astype(F32)[:, None] * inv
    cos64 = jnp.concatenate([jnp.cos(ang_r), jnp.cos(ang_r), jnp.cos(ang_c), jnp.cos(ang_c)], axis=-1)
    sin64 = jnp.concatenate([-jnp.sin(ang_r), jnp.sin(ang_r), -jnp.sin(ang_c), jnp.sin(ang_c)], axis=-1)
    return jnp.tile(cos64, (1, 2)), jnp.tile(sin64, (1, 2))


ATTN_TQ = 512
ATTN_KC = 512


def _attn_kernel(*refs, n_kv, kv_rows, chunks, lam_init):
    q_ref = refs[0]
    k_refs = refs[1:1 + n_kv]
    v_refs = refs[1 + n_kv:1 + 2 * n_kv]
    lam_ref, w_ref, o_ref, v1_sc = refs[1 + 2 * n_kv:]

    @pl.when(pl.program_id(2) == 0)
    def _():
        off = 0
        for ki, rows in enumerate(kv_rows):
            v1_sc[off:off + rows, :DA_V_DIM] = v_refs[ki][...]
            v1_sc[off:off + rows, DA_V_DIM:] = jnp.ones((rows, DA_V_DIM), BF16)
            off += rows

    q = q_ref[...]
    tq = q.shape[0]
    lane = lax.broadcasted_iota(jnp.int32, q.shape, 1)
    zero = jnp.zeros_like(q)
    qq = jnp.concatenate([jnp.where(lane < DA_HEAD_DIM, q, zero), jnp.where(lane >= DA_HEAD_DIM, q, zero)], axis=0)
    m = jnp.full((2 * tq, 1), -jnp.inf, F32)
    acc = jnp.zeros((2 * tq, 2 * DA_V_DIM), F32)
    def scores(chunk):
        ki, start, _, size = chunk
        return lax.dot_general(qq, k_refs[ki][start:start + size, :], NT_DIMS, preferred_element_type=F32)

    s_next = scores(chunks[0])
    for ci, (ki, start, off, size) in enumerate(chunks):
        s = s_next
        if ci + 1 < len(chunks):
            s_next = scores(chunks[ci + 1])
        m_new = jnp.maximum(m, jnp.max(s, axis=-1, keepdims=True))
        p = jnp.exp2(s - m_new).astype(BF16)
        acc = jnp.exp2(m - m_new) * acc + jnp.dot(p, v1_sc[off:off + size, :], preferred_element_type=F32)
        m = m_new
    o = acc[:, :DA_V_DIM] / acc[:, DA_V_DIM:]
    od = o[:tq] - lam_ref[...] * o[tq:]
    o_ref[...] = (_rms_rows(od) * w_ref[...] * (1.0 - lam_init)).astype(o_ref.dtype)


def diff_attention(q, ks, vs, kv_rows, q_rows, lam_row, subln_w, lam_init):
    t = q.shape[0]
    b = t // q_rows
    tq = min(ATTN_TQ, q_rows)
    nq = q_rows // tq
    chunks, off = [], 0
    for ki, rows in enumerate(kv_rows):
        kc = min(ATTN_KC, rows)
        chunks += [(ki, s, off + s, kc) for s in range(0, rows, kc)]
        off += rows
    qmap = lambda bi, h, qi: (bi * nq + qi, h)
    kvmap = lambda bi, h, qi: (bi, h)
    const = lambda bi, h, qi: (0, 0)
    return pl.pallas_call(
        functools.partial(_attn_kernel, n_kv=len(ks), kv_rows=tuple(kv_rows), chunks=tuple(chunks),
                          lam_init=lam_init),
        grid=(b, DA_HEADS, nq),
        in_specs=([pl.BlockSpec((tq, LANES), qmap)]
                  + [pl.BlockSpec((rows, LANES), kvmap) for rows in kv_rows] * 2
                  + [pl.BlockSpec((1, LANES), const)] * 2),
        out_specs=pl.BlockSpec((tq, LANES), qmap),
        out_shape=jax.ShapeDtypeStruct((t, DA_WIDTH), BF16),
        scratch_shapes=[pltpu.VMEM((off, 2 * DA_V_DIM), BF16)],
        compiler_params=_params("parallel", "parallel", "arbitrary"), name="diff_attention",
    )(q, *ks, *vs, lam_row, subln_w.reshape(1, LANES))


S5_LC = 64
S5_CW = S5_LC * S5_GROUP
S5_SW = 2 * S5_STATE


def _s5_discretize(lam_re, lam_im, log_step, b_re, b_im):
    lr, li = lam_re.astype(F32), lam_im.astype(F32)
    step = jnp.exp(log_step.astype(F32))[:, None]
    mag = jnp.exp(lr * step)
    ab_re, ab_im = mag * jnp.cos(li * step), mag * jnp.sin(li * step)
    den = lr * lr + li * li
    nr, ni = ab_re - 1.0, ab_im
    f_re = (nr * lr + ni * li) / den
    f_im = (ni * lr - nr * li) / den
    br, bi = b_re.astype(F32), b_im.astype(F32)
    bb_re = f_re[..., None] * br - f_im[..., None] * bi
    bb_im = f_re[..., None] * bi + f_im[..., None] * br
    return bb_re, bb_im


def _s5_tables(lam_re, lam_im, log_step, b_re, b_im, c_re, c_im):
    lc, g = S5_LC, S5_GROUPS
    bm, cm, mm, k1, k2 = [], [], [], [], []
    for d in range(2):
        lr, li = lam_re[d].astype(F32), lam_im[d].astype(F32)
        step = jnp.exp(log_step[d].astype(F32))[:, None]
        bb_re, bb_im = _s5_discretize(lam_re[d], lam_im[d], log_step[d], b_re[d], b_im[d])
        tau = jnp.arange(lc + 1, dtype=F32)[:, None, None]
        mag = jnp.exp(tau * (lr * step))
        ang = tau * (li * step)
        pr, pi = mag * jnp.cos(ang), mag * jnp.sin(ang)
        abr = pr[..., None] * bb_re - pi[..., None] * bb_im
        abi = pr[..., None] * bb_im + pi[..., None] * bb_re
        cr, ci = c_re[d].astype(F32), c_im[d].astype(F32)
        kern = (jnp.einsum('gkp,tgph->tgkh', cr, abr[:lc], precision=HP)
                - jnp.einsum('gkp,tgph->tgkh', ci, abi[:lc], precision=HP))
        rank = np.arange(lc) if d == 0 else lc - 1 - np.arange(lc)
        e_in = lc - 1 - rank
        bmat = jnp.concatenate([abr[e_in].transpose(1, 3, 0, 2), abi[e_in].transpose(1, 3, 0, 2)], axis=-1)
        bm.append(bmat.reshape(g, S5_CW, S5_SW))
        e_out = rank + 1
        pro, pio = pr[e_out][:, :, None, :], pi[e_out][:, :, None, :]
        car = cr[None] * pro - ci[None] * pio
        cai = cr[None] * pio + ci[None] * pro
        cmat = jnp.concatenate([car.transpose(1, 3, 2, 0), -cai.transpose(1, 3, 2, 0)], axis=1)
        cm.append(cmat.reshape(g, S5_SW, S5_CW))
        kt = kern.transpose(1, 3, 2, 0)
        zpad = jnp.zeros((g, S5_GROUP, S5_GROUP, lc - 1), F32)
        strip = jnp.concatenate([zpad, kt] if d == 0 else [kt[..., ::-1], zpad], axis=3)
        mm.append(jnp.pad(strip, ((0, 0), (0, 0), (0, 0), (0, 1))))
        alr, ali = pr[lc], pi[lc]
        k1.append(jnp.repeat(jnp.concatenate([alr, alr], axis=-1), 8, axis=0))
        k2.append(jnp.repeat(jnp.concatenate([-ali, ali], axis=-1), 8, axis=0))
    return (jnp.stack(bm).astype(BF16), jnp.stack(cm).astype(BF16), _s5_toeplitz(jnp.stack(mm)),
            jnp.stack(k1), jnp.stack(k2))


def _s5_toeplitz_kernel(strip_ref, m_ref):
    lc = S5_LC
    lane = lax.broadcasted_iota(jnp.int32, (lc, 2 * lc), 1)
    for h in range(S5_GROUP):
        for kp in range(S5_GROUP // 2):
            halves = []
            for half in range(2):
                row = jnp.broadcast_to(strip_ref[0, 0, h, 2 * kp + half:2 * kp + half + 1, :], (lc, 2 * lc))
                base = lc + 1 if half == 0 else 1
                halves.append(pltpu.roll(row, base, 1, stride=1, stride_axis=0))
            tile = jnp.where(lane < lc, halves[0], halves[1])
            m_ref[0, 0, h * lc:(h + 1) * lc, kp * 2 * lc:(kp + 1) * 2 * lc] = tile.astype(m_ref.dtype)


def _s5_toeplitz(strips):
    nd, g = strips.shape[:2]
    return pl.pallas_call(
        _s5_toeplitz_kernel,
        grid=(nd, g),
        in_specs=[pl.BlockSpec((1, 1) + strips.shape[2:], lambda d, gi: (d, gi, 0, 0, 0))],
        out_specs=pl.BlockSpec((1, 1, S5_CW, S5_CW), lambda d, gi: (d, gi, 0, 0)),
        out_shape=jax.ShapeDtypeStruct((nd, g, S5_CW, S5_CW), BF16),
        compiler_params=_params("parallel", "parallel"), name="s5_toeplitz",
    )(strips)


def _s5_local_kernel(uc_ref, ul_ref, bm_ref, s_ref):
    u = jnp.concatenate([uc_ref[0], ul_ref[0]], axis=0)
    s = jnp.dot(u, bm_ref[0, 0], preferred_element_type=F32)
    s_ref[0] = s.reshape(s_ref.shape[1:])


def _s5_carry_kernel(s_ref, k1_ref, k2_ref, xin_ref, *, n_ctx, n_chunks):
    d = pl.program_id(0)
    k1, k2 = k1_ref[0], k2_ref[0]

    def body(k, x):
        rev = jnp.where(k < n_ctx, n_ctx - 1 - k, n_chunks + n_ctx - 1 - k)
        c = jnp.where(d == 0, k, rev)
        xin_ref[0, c] = x
        return k1 * x + k2 * pltpu.roll(x, S5_STATE, 1) + s_ref[0, c]

    lax.fori_loop(0, n_chunks, body, jnp.zeros(k1.shape, F32))


def _s5_out_kernel(uc_ref, ul_ref, m_ref, xin_ref, cm_ref, yc_ref, yl_ref):
    u = jnp.concatenate([uc_ref[0], ul_ref[0]], axis=0)
    acc = jnp.dot(u, m_ref[0, 0], preferred_element_type=F32)
    acc += jnp.dot(u, m_ref[1, 0], preferred_element_type=F32)
    for d in range(2):
        xin = xin_ref[d].reshape(u.shape[0], S5_SW).astype(BF16)
        acc += jnp.dot(xin, cm_ref[d, 0], preferred_element_type=F32)
    rc = yc_ref.shape[1]
    yc_ref[0] = acc[:rc].astype(yc_ref.dtype)
    yl_ref[0] = acc[rc:].astype(yl_ref.dtype)


def s5_scan(u_ctx, u_lat, b, tables):
    bmat, cmat, mmat, k1, k2 = tables
    g, lc = S5_GROUPS, S5_LC
    n_ctx, n_lat = u_ctx.shape[0] // b, u_lat.shape[0] // b
    assert b == 8 and n_lat % lc == 0 and n_ctx % lc == 0
    rc, rl = n_ctx // lc * b, n_lat // lc * b
    nch = (n_ctx + n_lat) // lc

    def to_groups(x, rows):
        x = lax.optimization_barrier(jnp.swapaxes(x.reshape(b, rows // lc, lc, S5_WIDTH), 2, 3))
        return x.reshape(b, rows // lc, g, S5_CW).transpose(2, 1, 0, 3).reshape(g, rows // lc * b, S5_CW)

    def from_groups(y, rows):
        y = lax.optimization_barrier(y.reshape(g, rows // lc, b, S5_CW).transpose(2, 1, 0, 3))
        return jnp.swapaxes(y.reshape(b, rows // lc, S5_WIDTH, lc), 2, 3).reshape(b * rows, S5_WIDTH)

    ugc, ugl = to_groups(u_ctx, n_ctx), to_groups(u_lat, n_lat)
    s = pl.pallas_call(
        _s5_local_kernel,
        grid=(2, g),
        in_specs=[pl.BlockSpec((1, rc, S5_CW), lambda d, gi: (gi, 0, 0)),
                  pl.BlockSpec((1, rl, S5_CW), lambda d, gi: (gi, 0, 0)),
                  pl.BlockSpec((1, 1, S5_CW, S5_SW), lambda d, gi: (d, gi, 0, 0))],
        out_specs=pl.BlockSpec((1, nch, b, S5_SW), lambda d, gi: (d, 0, gi, 0)),
        out_shape=jax.ShapeDtypeStruct((2, nch, g * b, S5_SW), F32),
        compiler_params=_params("arbitrary", "arbitrary"), name="s5_local",
    )(ugc, ugl, bmat)
    xin = pl.pallas_call(
        functools.partial(_s5_carry_kernel, n_ctx=n_ctx // lc, n_chunks=nch),
        grid=(2,),
        in_specs=[pl.BlockSpec((1, nch, g * b, S5_SW), lambda d: (d, 0, 0, 0)),
                  pl.BlockSpec((1, g * b, S5_SW), lambda d: (d, 0, 0)),
                  pl.BlockSpec((1, g * b, S5_SW), lambda d: (d, 0, 0))],
        out_specs=pl.BlockSpec((1, nch, g * b, S5_SW), lambda d: (d, 0, 0, 0)),
        out_shape=jax.ShapeDtypeStruct((2, nch, g * b, S5_SW), F32),
        compiler_params=_params("arbitrary"), name="s5_carry",
    )(s, k1, k2)
    yc, yl = pl.pallas_call(
        _s5_out_kernel,
        grid=(g,),
        in_specs=[pl.BlockSpec((1, rc, S5_CW), lambda gi: (gi, 0, 0)),
                  pl.BlockSpec((1, rl, S5_CW), lambda gi: (gi, 0, 0)),
                  pl.BlockSpec((2, 1, S5_CW, S5_CW), lambda gi: (0, gi, 0, 0)),
                  pl.BlockSpec((2, nch, b, S5_SW), lambda gi: (0, 0, gi, 0)),
                  pl.BlockSpec((2, 1, S5_SW, S5_CW), lambda gi: (0, gi, 0, 0))],
        out_specs=[pl.BlockSpec((1, rc, S5_CW), lambda gi: (gi, 0, 0)),
                   pl.BlockSpec((1, rl, S5_CW), lambda gi: (gi, 0, 0))],
        out_shape=[jax.ShapeDtypeStruct((g, rc, S5_CW), BF16), jax.ShapeDtypeStruct((g, rl, S5_CW), BF16)],
        compiler_params=_params("arbitrary"), name="s5_out",
    )(ugc, ugl, mmat, xin, cmat)
    return from_groups(yc, n_ctx), from_groups(yl, n_lat)


def _gdn_prep_kernel(x_ref, w_ref, o_ref):
    j = pl.program_id(1)
    x = x_ref[...]
    n = x.shape[0]
    row = lax.broadcasted_iota(jnp.int32, x.shape, 0)
    half = GDN_CONV // 2
    acc = x * w_ref[half:half + 1, :]
    for sh in range(1, half + 1):
        acc += jnp.where(row >= sh, pltpu.roll(x, sh, 0), 0.0) * w_ref[half - sh:half - sh + 1, :]
        acc += jnp.where(row < n - sh, pltpu.roll(x, n - sh, 0), 0.0) * w_ref[half + sh:half + sh + 1, :]
    a = acc * _sigmoid(acc)
    lane = lax.broadcasted_iota(jnp.int32, x.shape, 1)
    lo = lane < GDN_DK
    sq = a * a
    s_lo = jnp.sum(jnp.where(lo, sq, 0.0), axis=-1, keepdims=True)
    s_hi = jnp.sum(jnp.where(lo, 0.0, sq), axis=-1, keepdims=True)
    nrm = a * lax.rsqrt(jnp.where(lo, s_lo, s_hi) + EPS)
    q_blocks = GDN_HEADS * GDN_DK // LANES
    nrm = nrm * jnp.where(j < q_blocks, GDN_DK ** -0.5, 1.0)
    o_ref[...] = jnp.where(j < 2 * q_blocks, nrm, a)


def gdn_prep(qkv, conv_w, seg):
    t, w = qkv.shape
    return pl.pallas_call(
        _gdn_prep_kernel,
        grid=(t // seg, w // LANES),
        in_specs=[pl.BlockSpec((seg, LANES), lambda s, j: (s, j)),
                  pl.BlockSpec((GDN_CONV, LANES), lambda s, j: (0, j))],
        out_specs=pl.BlockSpec((seg, LANES), lambda s, j: (s, j)),
        out_shape=jax.ShapeDtypeStruct((t, w), F32),
        compiler_params=_params("parallel", "arbitrary"), name="gdn_prep",
    )(qkv, conv_w)


def _bd(x):
    x2 = jnp.concatenate([x, x], axis=0)
    r = lax.broadcasted_iota(jnp.int32, x2.shape, 0)
    l = lax.broadcasted_iota(jnp.int32, x2.shape, 1)
    return jnp.where((r >> 6) == (l >> 6), x2, jnp.zeros_like(x2))


def _mm(a, b):
    return jnp.dot(a.astype(BF16), b.astype(BF16), preferred_element_type=F32)


def _dot01(m01, x):
    hi = x.astype(BF16)
    r1 = x - hi.astype(F32)
    mid = r1.astype(BF16)
    lo = (r1 - mid.astype(F32)).astype(BF16)
    dot = lambda p: jnp.dot(m01, p, preferred_element_type=F32)
    return (dot(lo) + dot(mid)) + dot(hi)


def _gdn_chunks(insts):
    c = GDN_CHUNK
    n = len(insts)
    every = range(n)
    q, k, v, beta, gcol, s_bd, rev = (list(t) for t in zip(*insts))
    i = lax.broadcasted_iota(jnp.int32, (c, LANES), 0)
    j = lax.broadcasted_iota(jnp.int32, (c, LANES), 1) & (c - 1)
    ti = lax.broadcasted_iota(jnp.int32, (c, c), 0)
    tj = lax.broadcasted_iota(jnp.int32, (c, c), 1)
    causal = [i <= j if r else i >= j for r in rev]
    strict = [i < j if r else i > j for r in rev]
    upto = [i >= j if r else i <= j for r in rev]
    lmat = [(tj >= ti if r else tj <= ti).astype(BF16) for r in rev]
    ones = jnp.ones((c, c), BF16)
    eye = jnp.where(i == j, 1.0, 0.0)
    lg = [_dot01(lmat[t], gcol[t]) for t in every]
    rg = [_dot01(ones, jnp.where(upto[t], gcol[t], 0.0)) for t in every]
    decay = [jnp.where(causal[t], jnp.exp(jnp.where(causal[t], lg[t] - rg[t], 0.0)), 0.0) for t in every]
    kb = [k[t] * beta[t] for t in every]
    k_bd = [_bd(k[t].astype(BF16)) for t in every]
    a = [jnp.where(strict[t], lax.dot_general(kb[t].astype(BF16), k_bd[t], NT_DIMS, preferred_element_type=F32)
                   * decay[t], 0.0) for t in every]
    qk = [lax.dot_general(q[t].astype(BF16), k_bd[t], NT_DIMS, preferred_element_type=F32) * decay[t] for t in every]
    eg = [jnp.exp(lg[t]) for t in every]
    same = [(i >> sh) == (j >> sh) for sh in (3, 4, 5)]
    x = [jnp.where(same[0], a[t], 0.0) for t in every]
    p = [eye - x[t] for t in every]
    for _ in range(2):
        x = [_mm(x[t], _bd(x[t])) for t in every]
        p = [p[t] + _mm(p[t], _bd(x[t])) for t in every]
    for lvl in range(3):
        inner = same[lvl]
        join = jnp.logical_not(inner) if lvl == 2 else jnp.logical_and(same[lvl + 1], jnp.logical_not(inner))
        tl = [_mm(p[t], _bd(jnp.where(join, a[t], 0.0))) for t in every]
        p = [p[t] - _mm(tl[t], _bd(p[t])) for t in every]
    u = [_mm(p[t], _bd(v[t] * beta[t])) for t in every]
    w = [_mm(p[t], _bd(kb[t] * eg[t])) for t in every]
    v_new = [u[t] - _mm(w[t], s_bd[t]) for t in every]
    o_state = [_mm(q[t] * eg[t], s_bd[t]) for t in every]
    o = [o_state[t] + _mm(qk[t], _bd(v_new[t])) for t in every]
    g_last = [lg[t][0:1, :] if rev[t] else lg[t][c - 1:c, :] for t in every]
    k_dec = [k[t] * jnp.exp(g_last[t] - lg[t]) for t in every]
    upd = [jnp.dot(k_dec[t].T.astype(BF16), v_new[t].astype(BF16), preferred_element_type=F32) for t in every]
    r2 = lax.broadcasted_iota(jnp.int32, (LANES, LANES), 0)
    l2 = lax.broadcasted_iota(jnp.int32, (LANES, LANES), 1)
    diag = (r2 >> 6) == (l2 >> 6)
    s_new = [s_bd[t] * jnp.exp(g_last[t]) + jnp.where(diag, upd[t], 0.0) for t in every]
    return list(zip(o, s_new))


GDN_BATCH_UNROLL = 4


def _gdn_scan_kernel(qf, kf, vf, baf, qr, kr, vr, bar, alog_ref, dtb_ref, s0_ref,
                     of_ref, or_ref, sfin_ref, s_sc):
    c = pl.program_id(0)

    @pl.when(c == 0)
    def _():
        s_sc[...] = s0_ref[...]

    lane = lax.broadcasted_iota(jnp.int32, (GDN_CHUNK, LANES), 1)
    first = lane < GDN_DK
    ins = ((qf, kf, vf, baf, of_ref), (qr, kr, vr, bar, or_ref))

    def body(it, carry):
        work = []
        for bb in range(GDN_BATCH_UNROLL):
            b = it * GDN_BATCH_UNROLL + bb
            for d in range(2):
                q_ref, k_ref, v_ref, ba_ref, o_ref = ins[d]
                ba = ba_ref[b]
                bsig = _sigmoid(ba)
                sp = ba + dtb_ref[...]
                gall = -jnp.exp(alog_ref[...]) * (jnp.maximum(sp, 0.0) + jnp.log(1.0 + jnp.exp(-jnp.abs(sp))))
                for hp in range(GDN_HEADS // 2):
                    col = d * GDN_HEADS + 2 * hp
                    beta = jnp.where(first, bsig[:, col:col + 1], bsig[:, col + 1:col + 2])
                    gcol = jnp.where(first, gall[:, 8 + col:9 + col], gall[:, 9 + col:10 + col])
                    sl = slice(hp * LANES, (hp + 1) * LANES)
                    work.append((b, d, hp, sl, o_ref, (q_ref[b, :, sl], k_ref[b, :, sl], v_ref[b, :, sl],
                                                       beta, gcol, s_sc[b, d, hp], d == 1)))
        done = _gdn_chunks([args for (_, _, _, _, _, args) in work])
        for (b, d, hp, sl, o_ref, _), (o, s_new) in zip(work, done):
            o_ref[b, :, sl] = o
            s_sc[b, d, hp] = s_new
        return carry

    lax.fori_loop(0, s_sc.shape[0] // GDN_BATCH_UNROLL, body, 0)

    @pl.when(c == pl.num_programs(0) - 1)
    def _():
        sfin_ref[...] = s_sc[...]


def gdn_scan(qkvn, ba, s0, alog_row, dtb_row):
    b, l, _ = qkvn.shape
    nch = l // GDN_CHUNK
    blk = (b, GDN_CHUNK, GDN_WIDTH)
    fwd = lambda col: (lambda c: (0, c, col))
    bwd = lambda col: (lambda c: (0, nch - 1 - c, col))
    st = pl.BlockSpec(s0.shape, lambda c: (0, 0, 0, 0, 0))
    return pl.pallas_call(
        _gdn_scan_kernel,
        grid=(nch,),
        in_specs=[pl.BlockSpec(blk, fwd(0)), pl.BlockSpec(blk, fwd(1)), pl.BlockSpec(blk, fwd(2)),
                  pl.BlockSpec((b, GDN_CHUNK, LANES), fwd(0)),
                  pl.BlockSpec(blk, bwd(0)), pl.BlockSpec(blk, bwd(1)), pl.BlockSpec(blk, bwd(2)),
                  pl.BlockSpec((b, GDN_CHUNK, LANES), bwd(0)),
                  pl.BlockSpec((1, LANES), lambda c: (0, 0)), pl.BlockSpec((1, LANES), lambda c: (0, 0)), st],
        out_specs=[pl.BlockSpec(blk, fwd(0)), pl.BlockSpec(blk, bwd(0)), st],
        out_shape=[jax.ShapeDtypeStruct((b, l, GDN_WIDTH), F32)] * 2 + [jax.ShapeDtypeStruct(s0.shape, F32)],
        scratch_shapes=[pltpu.VMEM(s0.shape, F32)],
        compiler_params=_params("arbitrary"), name="gdn_scan",
    )(qkvn, qkvn, qkvn, ba, qkvn, qkvn, qkvn, ba, alog_row, dtb_row, s0)


OUTPROJ_TM = 512


def _outproj_kernel(x_ref, mod_ref, da_ref, ys_ref, u_ref, dsk_ref, wg_ref, bg_ref,
                    of_ref, or_ref, z_ref, gnw_ref, avg_ref, w_ref, o_ref):
    y = ys_ref[...] + u_ref[...] * dsk_ref[...]
    zz = 0.5 * y * (1.0 + jnp.tanh(math.sqrt(2.0 / math.pi) * (y + 0.044715 * (y * y * y))))
    glu = zz * _sigmoid(jnp.dot(zz.astype(BF16), wg_ref[...], preferred_element_type=F32) + bg_ref[...])
    o = of_ref[...] + or_ref[...]
    ms = jnp.dot(o * o, avg_ref[...], precision=HP, preferred_element_type=F32)
    z = z_ref[...]
    gd = o * lax.rsqrt(ms + EPS) * gnw_ref[...] * (z * _sigmoid(z))
    acc = jnp.dot(da_ref[...], w_ref[0:DA_WIDTH, :], preferred_element_type=F32)
    acc += jnp.dot(glu.astype(BF16), w_ref[DA_WIDTH:DA_WIDTH + S5_WIDTH, :], preferred_element_type=F32)
    acc += jnp.dot(gd.astype(BF16), w_ref[DA_WIDTH + S5_WIDTH:, :], preferred_element_type=F32)
    o_ref[...] = x_ref[...] + mod_ref[0, 2:3, :] * acc


def outproj(x, mod, rows_per_mod, da, ys, u, dsk, w_glu, b_glu, o_f, o_r, z, gnw, avg, w_out):
    t, d = x.shape
    tm = min(OUTPROJ_TM, t)
    tpm = rows_per_mod // tm
    row = lambda i: (i, 0)
    const = lambda i: (0, 0)
    sw = S5_WIDTH
    return pl.pallas_call(
        _outproj_kernel,
        grid=(t // tm,),
        in_specs=[pl.BlockSpec((tm, d), row),
                  pl.BlockSpec((1, 6, d), lambda i: (i // tpm, 0, 0)),
                  pl.BlockSpec((tm, DA_WIDTH), row),
                  pl.BlockSpec((tm, sw), row), pl.BlockSpec((tm, sw), row),
                  pl.BlockSpec((1, sw), const), pl.BlockSpec((sw, sw), const), pl.BlockSpec((1, sw), const),
                  pl.BlockSpec((tm, GDN_WIDTH), row), pl.BlockSpec((tm, GDN_WIDTH), row),
                  pl.BlockSpec((tm, GDN_WIDTH), row),
                  pl.BlockSpec((1, GDN_WIDTH), const), pl.BlockSpec((GDN_WIDTH, GDN_WIDTH), const),
                  pl.BlockSpec((D_MIX, d), const)],
        out_specs=pl.BlockSpec((tm, d), row),
        out_shape=jax.ShapeDtypeStruct((t, d), F32),
        compiler_params=_params("parallel"), name="outproj",
    )(x, mod, da, ys, u, dsk, w_glu, b_glu, o_f, o_r, z, gnw, avg, w_out)


MOE_TM = 1024
MOE_EP = 5


def _pick_lowest(cur, idx, sentinel, axis):
    m = jnp.max(cur, axis=axis, keepdims=True)
    first = jnp.min(jnp.where(cur == m, idx, sentinel), axis=axis, keepdims=True)
    return idx == first


def _route(logits_t, bias):
    tm = logits_t.shape[1]
    neg = jnp.float32(-jnp.inf)
    scores = jax.nn.sigmoid(logits_t)
    biased = scores + bias
    b3 = biased.reshape(N_GROUPS, GROUP_SIZE, tm)
    eidx = lax.broadcasted_iota(jnp.int32, b3.shape, 1)
    m1 = jnp.max(b3, axis=1, keepdims=True)
    p1 = _pick_lowest(b3, eidx, GROUP_SIZE, 1)
    m2 = jnp.max(jnp.where(p1, neg, b3), axis=1, keepdims=True)
    gs = (m1 + m2).reshape(N_GROUPS, tm)
    gidx = lax.broadcasted_iota(jnp.int32, gs.shape, 0)
    gsel = jnp.zeros(gs.shape, jnp.bool_)
    cur = gs
    for _ in range(TOPK_GROUPS):
        pick = _pick_lowest(cur, gidx, N_GROUPS, 0)
        gsel = jnp.logical_or(gsel, pick)
        cur = jnp.where(pick, neg, cur)
    emask = jnp.broadcast_to(gsel.reshape(N_GROUPS, 1, tm), b3.shape)
    cur = jnp.where(emask, b3, neg).reshape(N_EXPERTS, tm)
    ridx = lax.broadcasted_iota(jnp.int32, cur.shape, 0)
    sel = jnp.zeros(cur.shape, jnp.bool_)
    for _ in range(TOP_K):
        pick = _pick_lowest(cur, ridx, N_EXPERTS, 0)
        sel = jnp.logical_or(sel, pick)
        cur = jnp.where(pick, neg, cur)
    w = jnp.where(sel, scores, 0.0)
    return w / jnp.sum(w, axis=0, keepdims=True) * ROUTED_SCALE


def _moe_kernel(x_ref, mod_ref, nw_ref, wr_ref, rb_ref, w13_ref, w2_ref, fnw_ref, o_ref,
                h_sc, gate_sc, acc_sc, hid_sc, *, final_norm):
    e = pl.program_id(1)
    n_e = pl.num_programs(1)
    ep = w13_ref.shape[0]

    @pl.when(e == 0)
    def _():
        h = _rms_rows(x_ref[...]) * nw_ref[...] * (1.0 + mod_ref[0, 4:5, :]) + mod_ref[0, 3:4, :]
        h_sc[...] = h.astype(BF16)
        logits_t = lax.dot_general(wr_ref[...], h, NT_DIMS, precision=HP, preferred_element_type=F32)
        w = _route(logits_t, rb_ref[...])
        tm = w.shape[1]
        row = lax.broadcasted_iota(jnp.int32, (LANES - N_EXPERTS, tm), 0)
        shared = jnp.where(row == 0, 1.0, 0.0).astype(F32)
        gate_sc[...] = jnp.concatenate([w, shared], axis=0).T
        acc_sc[...] = jnp.zeros_like(acc_sc)

    h = h_sc[...]
    lane = lax.broadcasted_iota(jnp.int32, (1, LANES), 1)
    for j in range(ep):
        ab = jnp.dot(h, w13_ref[j], preferred_element_type=F32)
        a, b = ab[:, :MOE_FFN], ab[:, MOE_FFN:]
        g = jnp.sum(jnp.where(lane == e * ep + j, gate_sc[...], 0.0), axis=-1, keepdims=True)
        hid_sc[:, j * MOE_FFN:(j + 1) * MOE_FFN] = (a * _sigmoid(a) * b * g).astype(BF16)
    w2 = w2_ref[...].reshape(ep * MOE_FFN, w2_ref.shape[2])
    acc_sc[...] += jnp.dot(hid_sc[...], w2, preferred_element_type=F32)

    @pl.when(e == n_e - 1)
    def _():
        y = x_ref[...] + mod_ref[0, 5:6, :] * acc_sc[...]
        if final_norm:
            y = _rms_rows(y) * fnw_ref[...]
        o_ref[...] = y


def moe_sublayer(x, mod, rows_per_mod, norm_w, w_router_t, router_bias, w13, w2, final_w, final_norm):
    t, d = x.shape
    tm = min(MOE_TM, rows_per_mod)
    assert t % tm == 0 and rows_per_mod % tm == 0
    n_slots = w13.shape[0]
    ep = MOE_EP
    assert n_slots % ep == 0
    tiles_per_mod = rows_per_mod // tm
    return pl.pallas_call(
        functools.partial(_moe_kernel, final_norm=final_norm),
        grid=(t // tm, n_slots // ep),
        in_specs=[
            pl.BlockSpec((tm, d), lambda i, e: (i, 0)),
            pl.BlockSpec((1, 6, d), lambda i, e: (i // tiles_per_mod, 0, 0)),
            pl.BlockSpec((1, d), lambda i, e: (0, 0)),
            pl.BlockSpec((N_EXPERTS, d), lambda i, e: (0, 0)),
            pl.BlockSpec((N_EXPERTS, 1), lambda i, e: (0, 0)),
            pl.BlockSpec((ep, d, 2 * MOE_FFN), lambda i, e: (e, 0, 0)),
            pl.BlockSpec((ep, MOE_FFN, d), lambda i, e: (e, 0, 0)),
            pl.BlockSpec((1, d), lambda i, e: (0, 0)),
        ],
        out_specs=pl.BlockSpec((tm, d), lambda i, e: (i, 0)),
        out_shape=jax.ShapeDtypeStruct((t, d), F32),
        scratch_shapes=[
            pltpu.VMEM((tm, d), BF16),
            pltpu.VMEM((tm, LANES), F32),
            pltpu.VMEM((tm, d), F32),
            pltpu.VMEM((tm, ep * MOE_FFN), BF16),
        ],
        compiler_params=_params("parallel", "arbitrary"), name="moe_sublayer",
    )(x, mod, norm_w.reshape(1, d), w_router_t, router_bias.reshape(N_EXPERTS, 1), w13, w2, final_w.reshape(1, d))


def _moe_weights(w1, w3, w2, ws1, ws3, ws2):
    w13 = jnp.concatenate([jnp.concatenate([w1, w3], axis=-1),
                           jnp.concatenate([ws1, ws3], axis=-1)[None]], axis=0).astype(BF16)
    w2a = jnp.concatenate([w2, ws2[None]], axis=0).astype(BF16)
    return w13, w2a


def kernel(x, c, ctx, c_ctx, norm1_w, norm2_w, w_mod, b_mod, w_in, w_out, da_lambda, da_subln_w,
           s5_lam_re, s5_lam_im, s5_log_step, s5_b_re, s5_b_im, s5_c_re, s5_c_im, s5_d, s5_w_glu, s5_b_glu,
           gdn_conv_w, gdn_a_log, gdn_dt_bias, gdn_norm_w,
           moe_w_router, moe_router_bias, moe_w1, moe_w3, moe_w2, moe_ws1, moe_ws3, moe_ws2,
           final_norm_w):
    b, n, d = x.shape
    nc = ctx.shape[1]
    cos_t, sin_t = _rope_tables(n)
    cond = jnp.zeros((2 * b, d), F32).at[:b].set(c).at[b].set(c_ctx)
    xl = x.reshape(b * n, d)
    xc = ctx.reshape(b * nc, d)
    head_avg = jnp.kron(jnp.eye(GDN_HEADS, dtype=F32), jnp.full((GDN_DV, GDN_DV), 1.0 / GDN_DV, F32))
    s_zero = jnp.zeros((b, 2, GDN_HEADS // 2, LANES, LANES), F32)
    for i in range(DEPTH):
        ctx_out = i < DEPTH - 1
        last = i == DEPTH - 1
        lam_init = 0.8 - 0.6 * math.exp(-0.3 * i)
        mod_all = mod_proj(cond, w_mod, b_mod, i).reshape(2 * b, 6, d)
        mod, modc = mod_all[:b], mod_all[b:b + 1]

        w_in_p = jnp.pad(w_in[i], ((0, 0), (0, IN_PAD - IN_WIDTH))).astype(BF16)
        q, k, v, u, ub, gq, z, ba = inproj(xl, mod, n, norm1_w[i], w_in_p, cos_t, sin_t, rope=True)
        qc, kc, vc, uc, ubc, gqc, zc, bac = inproj(xc, modc, b * nc, norm1_w[i], w_in_p, cos_t, sin_t, rope=False)

        lq1, lk1, lq2, lk2 = da_lambda[i].astype(F32)
        lam = jnp.exp(jnp.sum(lq1 * lk1)) - jnp.exp(jnp.sum(lq2 * lk2)) + lam_init
        lam_row = jnp.full((1, LANES), lam, F32)
        da = diff_attention(q, [kc, k], [vc, v], (nc, n), n, lam_row, da_subln_w[i], lam_init)

        tables = _s5_tables(s5_lam_re[i], s5_lam_im[i], s5_log_step[i], s5_b_re[i], s5_b_im[i],
                            s5_c_re[i], s5_c_im[i])
        ysc, ysl = s5_scan(ubc, ub, b, tables)

        alog_row = jnp.zeros((1, LANES), F32).at[0, 8:16].set(gdn_a_log[i].astype(F32).reshape(-1))
        dtb_row = jnp.zeros((1, LANES), F32).at[0, 8:16].set(gdn_dt_bias[i].astype(F32).reshape(-1))
        gn = gdn_prep(gq, gdn_conv_w[i], n)
        gnc = gdn_prep(gqc, gdn_conv_w[i], nc)
        ofc, orc, s_ctx = gdn_scan(gnc.reshape(b, nc, GDN_QKV_W), bac.reshape(b, nc, LANES), s_zero, alog_row, dtb_row)
        of, orv, _ = gdn_scan(gn.reshape(b, n, GDN_QKV_W), ba.reshape(b, n, LANES), s_ctx, alog_row, dtb_row)

        w_out_b = w_out[i].astype(BF16)
        dsk = s5_d[i].astype(F32).reshape(1, S5_WIDTH)
        wg = s5_w_glu[i].astype(BF16)
        bg = s5_b_glu[i].astype(F32).reshape(1, S5_WIDTH)
        gnw = jnp.tile(gdn_norm_w[i].astype(F32), GDN_HEADS).reshape(1, GDN_WIDTH)
        xl = outproj(xl, mod, n, da, ysl, u, dsk, wg, bg,
                     of.reshape(b * n, GDN_WIDTH), orv.reshape(b * n, GDN_WIDTH), z, gnw, head_avg, w_out_b)
        if ctx_out:
            dac = diff_attention(qc, [kc], [vc], (nc,), nc, lam_row, da_subln_w[i], lam_init)
            xc = outproj(xc, modc, b * nc, dac, ysc, uc, dsk, wg, bg,
                         ofc.reshape(b * nc, GDN_WIDTH), orc.reshape(b * nc, GDN_WIDTH), zc, gnw, head_avg, w_out_b)

        w13, w2a = _moe_weights(moe_w1[i], moe_w3[i], moe_w2[i], moe_ws1[i], moe_ws3[i], moe_ws2[i])
        wr_t = moe_w_router[i].T
        xl = moe_sublayer(xl, mod, n, norm2_w[i], wr_t, moe_router_bias[i], w13, w2a, final_norm_w, last)
        if ctx_out:
            xc = moe_sublayer(xc, modc, b * nc, norm2_w[i], wr_t, moe_router_bias[i], w13, w2a, final_norm_w, False)
    return xl.reshape(b, n, d)
```

```python
import functools
import math

import jax
import jax.numpy as jnp
import numpy as np
from jax import lax
from jax.experimental import pallas as pl
from jax.experimental.pallas import tpu as pltpu

F32 = jnp.float32
BF16 = jnp.bfloat16

D_MODEL = 1024
DEPTH = 2
GRID_W = 64
EPS = 1e-6

DA_HEADS = 4
DA_HEAD_DIM = D_MODEL // 16
DA_V_DIM = 2 * DA_HEAD_DIM
DA_WIDTH = DA_HEADS * DA_V_DIM
ROPE_THETA = 10000.0

S5_WIDTH = D_MODEL // 4
S5_GROUP = 16
S5_GROUPS = S5_WIDTH // S5_GROUP
S5_STATE = 64

GDN_HEADS = 4
GDN_DK = D_MODEL // 16
GDN_DV = D_MODEL // 16
GDN_WIDTH = GDN_HEADS * GDN_DV
GDN_CONV = 5
GDN_CHUNK = 64

D_MIX = DA_WIDTH + S5_WIDTH + GDN_WIDTH
DA_QK_W = 2 * DA_HEADS * DA_HEAD_DIM
GDN_QKV_W = 2 * GDN_HEADS * GDN_DK + GDN_HEADS * GDN_DV
IN_SIZES = (DA_QK_W, DA_QK_W, DA_WIDTH, S5_WIDTH, GDN_QKV_W, GDN_WIDTH, 2 * GDN_HEADS, 2 * GDN_HEADS)
IN_WIDTH = sum(IN_SIZES)

N_EXPERTS = 64
TOP_K = 8
N_GROUPS = 8
GROUP_SIZE = N_EXPERTS // N_GROUPS
TOPK_GROUPS = 4
MOE_FFN = D_MODEL // 4
ROUTED_SCALE = 2.5

LANES = 128
SUBLANES = 8
V7X_VMEM_BYTES = 64 * 1024 * 1024
VMEM_LIMIT_BYTES = V7X_VMEM_BYTES * 7 // 8

GDN_A_LANE = 2 * GDN_HEADS

HP = lax.Precision.HIGHEST
NT_DIMS = (((1,), (1,)), ((), ()))


def _params(*sem):
    return pltpu.CompilerParams(dimension_semantics=sem, vmem_limit_bytes=VMEM_LIMIT_BYTES)


def _sigmoid(x):
    return 0.5 * (1.0 + jnp.tanh(0.5 * x))


def _rms_rows(x):
    return x * lax.rsqrt(jnp.mean(x * x, axis=-1, keepdims=True) + EPS)


MOD_TN = 768


def _mod_kernel(c_ref, w_ref, b_ref, o_ref):
    c = c_ref[...]
    act = c * _sigmoid(c)
    o_ref[...] = jnp.dot(act, w_ref[...], precision=HP, preferred_element_type=F32) + b_ref[...]


def mod_proj(cond, w_mod, b_mod, layer):
    r, d = cond.shape
    depth, _, n = w_mod.shape
    return pl.pallas_call(
        _mod_kernel,
        grid=(n // MOD_TN,),
        in_specs=[pl.BlockSpec((r, d), lambda j: (0, 0)),
                  pl.BlockSpec((None, d, MOD_TN), lambda j: (layer, 0, j)),
                  pl.BlockSpec((None, 1, MOD_TN), lambda j: (layer, 0, j))],
        out_specs=pl.BlockSpec((r, MOD_TN), lambda j: (0, j)),
        out_shape=jax.ShapeDtypeStruct((r, n), F32),
        compiler_params=_params("parallel"), name="mod_proj",
    )(cond, w_mod, b_mod.reshape(depth, 1, n))


IN_PAD = 2944
INPROJ_TM = 512
O_Q, O_K, O_V, O_U, O_G, O_Z, O_BA = 0, 512, 1024, 1536, 1792, 2560, 2816


def _rope_apply(x, cos, sin):
    lane = lax.broadcasted_iota(jnp.int32, x.shape, 1)
    up = pltpu.roll(x, LANES - 16, 1)
    dn = pltpu.roll(x, 16, 1)
    partner = jnp.where((lane & 31) < 16, up, dn)
    return x * cos + partner * sin


def _inproj_kernel(x_ref, mod_ref, nw_ref, w_ref, cos_ref, sin_ref,
                   q_ref, k_ref, v_ref, u_ref, ub_ref, g_ref, z_ref, ba_ref, *, rope):
    h = (_rms_rows(x_ref[...]) * nw_ref[...] * (1.0 + mod_ref[0, 1:2, :]) + mod_ref[0, 0:1, :]).astype(BF16)

    def proj(lo, hi):
        return jnp.dot(h, w_ref[:, lo:hi], preferred_element_type=F32)

    scale = DA_HEAD_DIM ** -0.5 * math.log2(math.e)
    q_all = proj(O_Q, O_K)
    k_all = proj(O_K, O_V)
    for hd in range(DA_HEADS):
        lo = hd * LANES
        qs = q_all[:, lo:lo + LANES]
        ks = k_all[:, lo:lo + LANES]
        if rope:
            qs = _rope_apply(qs, cos_ref[...], sin_ref[...])
            ks = _rope_apply(ks, cos_ref[...], sin_ref[...])
        q_ref[:, lo:lo + LANES] = (qs * scale).astype(BF16)
        k_ref[:, lo:lo + LANES] = ks.astype(BF16)
    v_ref[...] = proj(O_V, O_U).astype(BF16)
    u = proj(O_U, O_G)
    u_ref[...] = u
    ub_ref[...] = u.astype(BF16)
    g_ref[...] = proj(O_G, O_Z)
    z_ref[...] = proj(O_Z, O_BA)
    ba_ref[...] = proj(O_BA, IN_PAD)


def inproj(x, mod, rows_per_mod, norm_w, w_pad, cos_t, sin_t, rope):
    t, d = x.shape
    tm = min(INPROJ_TM, t)
    tpm = rows_per_mod // tm
    npos = cos_t.shape[0] // tm
    row = lambda i: (i, 0)
    widths = (DA_QK_W, DA_QK_W, DA_WIDTH, S5_WIDTH, S5_WIDTH, GDN_QKV_W, GDN_WIDTH, LANES)
    dtypes = (BF16, BF16, BF16, F32, BF16, F32, F32, F32)
    return pl.pallas_call(
        functools.partial(_inproj_kernel, rope=rope),
        grid=(t // tm,),
        in_specs=[pl.BlockSpec((tm, d), row),
                  pl.BlockSpec((1, 6, d), lambda i: (i // tpm, 0, 0)),
                  pl.BlockSpec((1, d), lambda i: (0, 0)),
                  pl.BlockSpec((d, IN_PAD), lambda i: (0, 0)),
                  pl.BlockSpec((tm, LANES), lambda i: (i % npos, 0)),
                  pl.BlockSpec((tm, LANES), lambda i: (i % npos, 0))],
        out_specs=[pl.BlockSpec((tm, w), row) for w in widths],
        out_shape=[jax.ShapeDtypeStruct((t, w), dt) for w, dt in zip(widths, dtypes)],
        compiler_params=_params("parallel"), name="inproj",
    )(x, mod, norm_w.reshape(1, d), w_pad, cos_t, sin_t)


def _rope_tables(n):
    nf = DA_HEAD_DIM // 4
    t = jnp.arange(n, dtype=jnp.int32)
    inv = ROPE_THETA ** (-jnp.arange(nf, dtype=F32) / nf)
    ang_r = (t // GRID_W).astype(F32)[:, None] * inv
    ang_c = (t % GRID_W).astype(F32)[:, None] * inv
    cos64 = jnp.concatenate([jnp.cos(ang_r), jnp.cos(ang_r), jnp.cos(ang_c), jnp.cos(ang_c)], axis=-1)
    sin64 = jnp.concatenate([-jnp.sin(ang_r), jnp.sin(ang_r), -jnp.sin(ang_c), jnp.sin(ang_c)], axis=-1)
    return jnp.tile(cos64, (1, 2)), jnp.tile(sin64, (1, 2))


ATTN_TQ = 512
ATTN_KC = 512


def _attn_kernel(*refs, n_kv, kv_rows, chunks, lam_init):
    q_ref = refs[0]
    k_refs = refs[1:1 + n_kv]
    v_refs = refs[1 + n_kv:1 + 2 * n_kv]
    lam_ref, w_ref, o_ref, v1_sc = refs[1 + 2 * n_kv:]

    @pl.when(pl.program_id(2) == 0)
    def _():
        off = 0
        for ki, rows in enumerate(kv_rows):
            v1_sc[off:off + rows, :DA_V_DIM] = v_refs[ki][...]
            v1_sc[off:off + rows, DA_V_DIM:] = jnp.ones((rows, DA_V_DIM), BF16)
            off += rows

    q = q_ref[...]
    tq = q.shape[0]
    lane = lax.broadcasted_iota(jnp.int32, q.shape, 1)
    zero = jnp.zeros_like(q)
    qq = jnp.concatenate([jnp.where(lane < DA_HEAD_DIM, q, zero), jnp.where(lane >= DA_HEAD_DIM, q, zero)], axis=0)
    m = jnp.full((2 * tq, 1), -jnp.inf, F32)
    acc = jnp.zeros((2 * tq, 2 * DA_V_DIM), F32)
    def scores(chunk):
        ki, start, _, size = chunk
        return lax.dot_general(qq, k_refs[ki][start:start + size, :], NT_DIMS, preferred_element_type=F32)

    s_next = scores(chunks[0])
    for ci, (ki, start, off, size) in enumerate(chunks):
        s = s_next
        if ci + 1 < len(chunks):
            s_next = scores(chunks[ci + 1])
        m_new = jnp.maximum(m, jnp.max(s, axis=-1, keepdims=True))
        p = jnp.exp2(s - m_new).astype(BF16)
        acc = jnp.exp2(m - m_new) * acc + jnp.dot(p, v1_sc[off:off + size, :], preferred_element_type=F32)
        m = m_new
    o = acc[:, :DA_V_DIM] / acc[:, DA_V_DIM:]
    od = o[:tq] - lam_ref[...] * o[tq:]
    o_ref[...] = (_rms_rows(od) * w_ref[...] * (1.0 - lam_init)).astype(o_ref.dtype)


def diff_attention(q, ks, vs, kv_rows, q_rows, lam_row, subln_w, lam_init):
    t = q.shape[0]
    b = t // q_rows
    tq = min(ATTN_TQ, q_rows)
    nq = q_rows // tq
    chunks, off = [], 0
    for ki, rows in enumerate(kv_rows):
        kc = min(ATTN_KC, rows)
        chunks += [(ki, s, off + s, kc) for s in range(0, rows, kc)]
        off += rows
    qmap = lambda bi, h, qi: (bi * nq + qi, h)
    kvmap = lambda bi, h, qi: (bi, h)
    const = lambda bi, h, qi: (0, 0)
    return pl.pallas_call(
        functools.partial(_attn_kernel, n_kv=len(ks), kv_rows=tuple(kv_rows), chunks=tuple(chunks),
                          lam_init=lam_init),
        grid=(b, DA_HEADS, nq),
        in_specs=([pl.BlockSpec((tq, LANES), qmap)]
                  + [pl.BlockSpec((rows, LANES), kvmap) for rows in kv_rows] * 2
                  + [pl.BlockSpec((1, LANES), const)] * 2),
        out_specs=pl.BlockSpec((tq, LANES), qmap),
        out_shape=jax.ShapeDtypeStruct((t, DA_WIDTH), BF16),
        scratch_shapes=[pltpu.VMEM((off, 2 * DA_V_DIM), BF16)],
        compiler_params=_params("parallel", "parallel", "arbitrary"), name="diff_attention",
    )(q, *ks, *vs, lam_row, subln_w.reshape(1, LANES))


S5_LC = 64
S5_CW = S5_LC * S5_GROUP
S5_SW = 2 * S5_STATE


def _s5_discretize(lam_re, lam_im, log_step, b_re, b_im):
    lr, li = lam_re.astype(F32), lam_im.astype(F32)
    step = jnp.exp(log_step.astype(F32))[:, None]
    mag = jnp.exp(lr * step)
    ab_re, ab_im = mag * jnp.cos(li * step), mag * jnp.sin(li * step)
    den = lr * lr + li * li
    nr, ni = ab_re - 1.0, ab_im
    f_re = (nr * lr + ni * li) / den
    f_im = (ni * lr - nr * li) / den
    br, bi = b_re.astype(F32), b_im.astype(F32)
    bb_re = f_re[..., None] * br - f_im[..., None] * bi
    bb_im = f_re[..., None] * bi + f_im[..., None] * br
    return bb_re, bb_im


def _s5_tables(lam_re, lam_im, log_step, b_re, b_im, c_re, c_im):
    lc, g = S5_LC, S5_GROUPS
    bm, cm, mm, k1, k2 = [], [], [], [], []
    for d in range(2):
        lr, li = lam_re[d].astype(F32), lam_im[d].astype(F32)
        step = jnp.exp(log_step[d].astype(F32))[:, None]
        bb_re, bb_im = _s5_discretize(lam_re[d], lam_im[d], log_step[d], b_re[d], b_im[d])
        tau = jnp.arange(lc + 1, dtype=F32)[:, None, None]
        mag = jnp.exp(tau * (lr * step))
        ang = tau * (li * step)
        pr, pi = mag * jnp.cos(ang), mag * jnp.sin(ang)
        abr = pr[..., None] * bb_re - pi[..., None] * bb_im
        abi = pr[..., None] * bb_im + pi[..., None] * bb_re
        cr, ci = c_re[d].astype(F32), c_im[d].astype(F32)
        kern = (jnp.einsum('gkp,tgph->tgkh', cr, abr[:lc], precision=HP)
                - jnp.einsum('gkp,tgph->tgkh', ci, abi[:lc], precision=HP))
        rank = np.arange(lc) if d == 0 else lc - 1 - np.arange(lc)
        e_in = lc - 1 - rank
        bmat = jnp.concatenate([abr[e_in].transpose(1, 3, 0, 2), abi[e_in].transpose(1, 3, 0, 2)], axis=-1)
        bm.append(bmat.reshape(g, S5_CW, S5_SW))
        e_out = rank + 1
        pro, pio = pr[e_out][:, :, None, :], pi[e_out][:, :, None, :]
        car = cr[None] * pro - ci[None] * pio
        cai = cr[None] * pio + ci[None] * pro
        cmat = jnp.concatenate([car.transpose(1, 3, 2, 0), -cai.transpose(1, 3, 2, 0)], axis=1)
        cm.append(cmat.reshape(g, S5_SW, S5_CW))
        kt = kern.transpose(1, 3, 2, 0)
        zpad = jnp.zeros((g, S5_GROUP, S5_GROUP, lc - 1), F32)
        strip = jnp.concatenate([zpad, kt] if d == 0 else [kt[..., ::-1], zpad], axis=3)
        mm.append(jnp.pad(strip, ((0, 0), (0, 0), (0, 0), (0, 1))))
        alr, ali = pr[lc], pi[lc]
        k1.append(jnp.repeat(jnp.concatenate([alr, alr], axis=-1), SUBLANES, axis=0))
        k2.append(jnp.repeat(jnp.concatenate([-ali, ali], axis=-1), SUBLANES, axis=0))
    return (jnp.stack(bm).astype(BF16), jnp.stack(cm).astype(BF16), _s5_toeplitz(jnp.stack(mm)),
            jnp.stack(k1), jnp.stack(k2))


def _s5_toeplitz_kernel(strip_ref, m_ref):
    lc = S5_LC
    lane = lax.broadcasted_iota(jnp.int32, (lc, 2 * lc), 1)
    for h in range(S5_GROUP):
        for kp in range(S5_GROUP // 2):
            halves = []
            for half in range(2):
                row = jnp.broadcast_to(strip_ref[0, 0, h, 2 * kp + half:2 * kp + half + 1, :], (lc, 2 * lc))
                base = lc + 1 if half == 0 else 1
                halves.append(pltpu.roll(row, base, 1, stride=1, stride_axis=0))
            tile = jnp.where(lane < lc, halves[0], halves[1])
            m_ref[0, 0, h * lc:(h + 1) * lc, kp * 2 * lc:(kp + 1) * 2 * lc] = tile.astype(m_ref.dtype)


def _s5_toeplitz(strips):
    nd, g = strips.shape[:2]
    return pl.pallas_call(
        _s5_toeplitz_kernel,
        grid=(nd, g),
        in_specs=[pl.BlockSpec((1, 1) + strips.shape[2:], lambda d, gi: (d, gi, 0, 0, 0))],
        out_specs=pl.BlockSpec((1, 1, S5_CW, S5_CW), lambda d, gi: (d, gi, 0, 0)),
        out_shape=jax.ShapeDtypeStruct((nd, g, S5_CW, S5_CW), BF16),
        compiler_params=_params("parallel", "parallel"), name="s5_toeplitz",
    )(strips)


def _s5_local_kernel(uc_ref, ul_ref, bm_ref, s_ref):
    u = jnp.concatenate([uc_ref[0], ul_ref[0]], axis=0)
    s = jnp.dot(u, bm_ref[0, 0], preferred_element_type=F32)
    s_ref[0] = s.reshape(s_ref.shape[1:])


def _s5_carry_kernel(s_ref, k1_ref, k2_ref, xin_ref, *, n_ctx, n_chunks):
    d = pl.program_id(0)
    k1, k2 = k1_ref[0], k2_ref[0]

    def body(k, x):
        rev = jnp.where(k < n_ctx, n_ctx - 1 - k, n_chunks + n_ctx - 1 - k)
        c = jnp.where(d == 0, k, rev)
        xin_ref[0, c] = x
        return k1 * x + k2 * pltpu.roll(x, S5_STATE, 1) + s_ref[0, c]

    lax.fori_loop(0, n_chunks, body, jnp.zeros(k1.shape, F32))


def _s5_out_kernel(uc_ref, ul_ref, m_ref, xin_ref, cm_ref, yc_ref, yl_ref):
    u = jnp.concatenate([uc_ref[0], ul_ref[0]], axis=0)
    acc = jnp.dot(u, m_ref[0, 0], preferred_element_type=F32)
    acc += jnp.dot(u, m_ref[1, 0], preferred_element_type=F32)
    for d in range(2):
        xin = xin_ref[d].reshape(u.shape[0], S5_SW).astype(BF16)
        acc += jnp.dot(xin, cm_ref[d, 0], preferred_element_type=F32)
    rc = yc_ref.shape[1]
    yc_ref[0] = acc[:rc].astype(yc_ref.dtype)
    yl_ref[0] = acc[rc:].astype(yl_ref.dtype)


def s5_scan(u_ctx, u_lat, b, tables):
    bmat, cmat, mmat, k1, k2 = tables
    g, lc = S5_GROUPS, S5_LC
    n_ctx, n_lat = u_ctx.shape[0] // b, u_lat.shape[0] // b
    assert b == SUBLANES and n_lat % lc == 0 and n_ctx % lc == 0
    rc, rl = n_ctx // lc * b, n_lat // lc * b
    nch = (n_ctx + n_lat) // lc

    def to_groups(x, rows):
        x = lax.optimization_barrier(jnp.swapaxes(x.reshape(b, rows // lc, lc, S5_WIDTH), 2, 3))
        return x.reshape(b, rows // lc, g, S5_CW).transpose(2, 1, 0, 3).reshape(g, rows // lc * b, S5_CW)

    def from_groups(y, rows):
        y = lax.optimization_barrier(y.reshape(g, rows // lc, b, S5_CW).transpose(2, 1, 0, 3))
        return jnp.swapaxes(y.reshape(b, rows // lc, S5_WIDTH, lc), 2, 3).reshape(b * rows, S5_WIDTH)

    ugc, ugl = to_groups(u_ctx, n_ctx), to_groups(u_lat, n_lat)
    s = pl.pallas_call(
        _s5_local_kernel,
        grid=(2, g),
        in_specs=[pl.BlockSpec((1, rc, S5_CW), lambda d, gi: (gi, 0, 0)),
                  pl.BlockSpec((1, rl, S5_CW), lambda d, gi: (gi, 0, 0)),
                  pl.BlockSpec((1, 1, S5_CW, S5_SW), lambda d, gi: (d, gi, 0, 0))],
        out_specs=pl.BlockSpec((1, nch, b, S5_SW), lambda d, gi: (d, 0, gi, 0)),
        out_shape=jax.ShapeDtypeStruct((2, nch, g * b, S5_SW), F32),
        compiler_params=_params("arbitrary", "arbitrary"), name="s5_local",
    )(ugc, ugl, bmat)
    xin = pl.pallas_call(
        functools.partial(_s5_carry_kernel, n_ctx=n_ctx // lc, n_chunks=nch),
        grid=(2,),
        in_specs=[pl.BlockSpec((1, nch, g * b, S5_SW), lambda d: (d, 0, 0, 0)),
                  pl.BlockSpec((1, g * b, S5_SW), lambda d: (d, 0, 0)),
                  pl.BlockSpec((1, g * b, S5_SW), lambda d: (d, 0, 0))],
        out_specs=pl.BlockSpec((1, nch, g * b, S5_SW), lambda d: (d, 0, 0, 0)),
        out_shape=jax.ShapeDtypeStruct((2, nch, g * b, S5_SW), F32),
        compiler_params=_params("arbitrary"), name="s5_carry",
    )(s, k1, k2)
    yc, yl = pl.pallas_call(
        _s5_out_kernel,
        grid=(g,),
        in_specs=[pl.BlockSpec((1, rc, S5_CW), lambda gi: (gi, 0, 0)),
                  pl.BlockSpec((1, rl, S5_CW), lambda gi: (gi, 0, 0)),
                  pl.BlockSpec((2, 1, S5_CW, S5_CW), lambda gi: (0, gi, 0, 0)),
                  pl.BlockSpec((2, nch, b, S5_SW), lambda gi: (0, 0, gi, 0)),
                  pl.BlockSpec((2, 1, S5_SW, S5_CW), lambda gi: (0, gi, 0, 0))],
        out_specs=[pl.BlockSpec((1, rc, S5_CW), lambda gi: (gi, 0, 0)),
                   pl.BlockSpec((1, rl, S5_CW), lambda gi: (gi, 0, 0))],
        out_shape=[jax.ShapeDtypeStruct((g, rc, S5_CW), BF16), jax.ShapeDtypeStruct((g, rl, S5_CW), BF16)],
        compiler_params=_params("arbitrary"), name="s5_out",
    )(ugc, ugl, mmat, xin, cmat)
    return from_groups(yc, n_ctx), from_groups(yl, n_lat)


def _gdn_prep_kernel(x_ref, w_ref, o_ref):
    j = pl.program_id(1)
    x = x_ref[...]
    n = x.shape[0]
    row = lax.broadcasted_iota(jnp.int32, x.shape, 0)
    half = GDN_CONV // 2
    acc = x * w_ref[half:half + 1, :]
    for sh in range(1, half + 1):
        acc += jnp.where(row >= sh, pltpu.roll(x, sh, 0), 0.0) * w_ref[half - sh:half - sh + 1, :]
        acc += jnp.where(row < n - sh, pltpu.roll(x, n - sh, 0), 0.0) * w_ref[half + sh:half + sh + 1, :]
    a = acc * _sigmoid(acc)
    lane = lax.broadcasted_iota(jnp.int32, x.shape, 1)
    lo = lane < GDN_DK
    sq = a * a
    s_lo = jnp.sum(jnp.where(lo, sq, 0.0), axis=-1, keepdims=True)
    s_hi = jnp.sum(jnp.where(lo, 0.0, sq), axis=-1, keepdims=True)
    nrm = a * lax.rsqrt(jnp.where(lo, s_lo, s_hi) + EPS)
    q_blocks = GDN_HEADS * GDN_DK // LANES
    nrm = nrm * jnp.where(j < q_blocks, GDN_DK ** -0.5, 1.0)
    o_ref[...] = jnp.where(j < 2 * q_blocks, nrm, a)


def gdn_prep(qkv, conv_w, seg):
    t, w = qkv.shape
    return pl.pallas_call(
        _gdn_prep_kernel,
        grid=(t // seg, w // LANES),
        in_specs=[pl.BlockSpec((seg, LANES), lambda s, j: (s, j)),
                  pl.BlockSpec((GDN_CONV, LANES), lambda s, j: (0, j))],
        out_specs=pl.BlockSpec((seg, LANES), lambda s, j: (s, j)),
        out_shape=jax.ShapeDtypeStruct((t, w), F32),
        compiler_params=_params("parallel", "arbitrary"), name="gdn_prep",
    )(qkv, conv_w)


GDN_SHIFT = GDN_CHUNK.bit_length() - 1
assert 1 << GDN_SHIFT == GDN_CHUNK == GDN_DK == GDN_DV


def _bd(x):
    x2 = jnp.concatenate([x, x], axis=0)
    r = lax.broadcasted_iota(jnp.int32, x2.shape, 0)
    l = lax.broadcasted_iota(jnp.int32, x2.shape, 1)
    return jnp.where((r >> GDN_SHIFT) == (l >> GDN_SHIFT), x2, jnp.zeros_like(x2))


def _mm(a, b):
    return jnp.dot(a.astype(BF16), b.astype(BF16), preferred_element_type=F32)


def _dot01(m01, x):
    hi = x.astype(BF16)
    r1 = x - hi.astype(F32)
    mid = r1.astype(BF16)
    lo = (r1 - mid.astype(F32)).astype(BF16)
    dot = lambda p: jnp.dot(m01, p, preferred_element_type=F32)
    return (dot(lo) + dot(mid)) + dot(hi)


def _gdn_chunks(insts):
    c = GDN_CHUNK
    n = len(insts)
    every = range(n)
    q, k, v, beta, gcol, s_bd, rev = (list(t) for t in zip(*insts))
    i = lax.broadcasted_iota(jnp.int32, (c, LANES), 0)
    j = lax.broadcasted_iota(jnp.int32, (c, LANES), 1) & (c - 1)
    ti = lax.broadcasted_iota(jnp.int32, (c, c), 0)
    tj = lax.broadcasted_iota(jnp.int32, (c, c), 1)
    causal = [i <= j if r else i >= j for r in rev]
    strict = [i < j if r else i > j for r in rev]
    upto = [i >= j if r else i <= j for r in rev]
    lmat = [(tj >= ti if r else tj <= ti).astype(BF16) for r in rev]
    ones = jnp.ones((c, c), BF16)
    eye = jnp.where(i == j, 1.0, 0.0)
    lg = [_dot01(lmat[t], gcol[t]) for t in every]
    rg = [_dot01(ones, jnp.where(upto[t], gcol[t], 0.0)) for t in every]
    decay = [jnp.where(causal[t], jnp.exp(jnp.where(causal[t], lg[t] - rg[t], 0.0)), 0.0) for t in every]
    kb = [k[t] * beta[t] for t in every]
    k_bd = [_bd(k[t].astype(BF16)) for t in every]
    a = [jnp.where(strict[t], lax.dot_general(kb[t].astype(BF16), k_bd[t], NT_DIMS, preferred_element_type=F32)
                   * decay[t], 0.0) for t in every]
    qk = [lax.dot_general(q[t].astype(BF16), k_bd[t], NT_DIMS, preferred_element_type=F32) * decay[t] for t in every]
    eg = [jnp.exp(lg[t]) for t in every]
    base = 3
    same = [(i >> sh) == (j >> sh) for sh in range(base, GDN_SHIFT)]
    x = [jnp.where(same[0], a[t], 0.0) for t in every]
    p = [eye - x[t] for t in every]
    for _ in range(base - 1):
        x = [_mm(x[t], _bd(x[t])) for t in every]
        p = [p[t] + _mm(p[t], _bd(x[t])) for t in every]
    for lvl, inner in enumerate(same):
        outer_same = same[lvl + 1] if lvl + 1 < len(same) else True
        join = jnp.logical_and(outer_same, jnp.logical_not(inner))
        tl = [_mm(p[t], _bd(jnp.where(join, a[t], 0.0))) for t in every]
        p = [p[t] - _mm(tl[t], _bd(p[t])) for t in every]
    u = [_mm(p[t], _bd(v[t] * beta[t])) for t in every]
    w = [_mm(p[t], _bd(kb[t] * eg[t])) for t in every]
    v_new = [u[t] - _mm(w[t], s_bd[t]) for t in every]
    o_state = [_mm(q[t] * eg[t], s_bd[t]) for t in every]
    o = [o_state[t] + _mm(qk[t], _bd(v_new[t])) for t in every]
    g_last = [lg[t][0:1, :] if rev[t] else lg[t][c - 1:c, :] for t in every]
    k_dec = [k[t] * jnp.exp(g_last[t] - lg[t]) for t in every]
    upd = [jnp.dot(k_dec[t].T.astype(BF16), v_new[t].astype(BF16), preferred_element_type=F32) for t in every]
    r2 = lax.broadcasted_iota(jnp.int32, (LANES, LANES), 0)
    l2 = lax.broadcasted_iota(jnp.int32, (LANES, LANES), 1)
    diag = (r2 >> GDN_SHIFT) == (l2 >> GDN_SHIFT)
    s_new = [s_bd[t] * jnp.exp(g_last[t]) + jnp.where(diag, upd[t], 0.0) for t in every]
    return list(zip(o, s_new))


GDN_BATCH_UNROLL = 8


def _gdn_scan_kernel(qf, kf, vf, baf, qr, kr, vr, bar, alog_ref, dtb_ref, s0_ref,
                     of_ref, or_ref, sfin_ref, s_sc):
    c = pl.program_id(0)

    @pl.when(c == 0)
    def _():
        s_sc[...] = s0_ref[...]

    lane = lax.broadcasted_iota(jnp.int32, (GDN_CHUNK, LANES), 1)
    first = lane < GDN_DK
    ins = ((qf, kf, vf, baf, of_ref), (qr, kr, vr, bar, or_ref))

    def body(it, carry):
        work = []
        for bb in range(GDN_BATCH_UNROLL):
            b = it * GDN_BATCH_UNROLL + bb
            for d in range(2):
                q_ref, k_ref, v_ref, ba_ref, o_ref = ins[d]
                ba = ba_ref[b]
                bsig = _sigmoid(ba)
                sp = ba + dtb_ref[...]
                gall = -jnp.exp(alog_ref[...]) * (jnp.maximum(sp, 0.0) + jnp.log(1.0 + jnp.exp(-jnp.abs(sp))))
                for hp in range(GDN_HEADS // 2):
                    col = d * GDN_HEADS + 2 * hp
                    beta = jnp.where(first, bsig[:, col:col + 1], bsig[:, col + 1:col + 2])
                    ga = GDN_A_LANE + col
                    gcol = jnp.where(first, gall[:, ga:ga + 1], gall[:, ga + 1:ga + 2])
                    sl = slice(hp * LANES, (hp + 1) * LANES)
                    work.append((b, d, hp, sl, o_ref, (q_ref[b, :, sl], k_ref[b, :, sl], v_ref[b, :, sl],
                                                       beta, gcol, s_sc[b, d, hp], d == 1)))
        done = _gdn_chunks([args for (_, _, _, _, _, args) in work])
        for (b, d, hp, sl, o_ref, _), (o, s_new) in zip(work, done):
            o_ref[b, :, sl] = o
            s_sc[b, d, hp] = s_new
        return carry

    lax.fori_loop(0, s_sc.shape[0] // GDN_BATCH_UNROLL, body, 0)

    @pl.when(c == pl.num_programs(0) - 1)
    def _():
        sfin_ref[...] = s_sc[...]


def gdn_scan(qkvn, ba, s0, alog_row, dtb_row):
    b, l, _ = qkvn.shape
    nch = l // GDN_CHUNK
    blk = (b, GDN_CHUNK, GDN_WIDTH)
    fwd = lambda col: (lambda c: (0, c, col))
    bwd = lambda col: (lambda c: (0, nch - 1 - c, col))
    st = pl.BlockSpec(s0.shape, lambda c: (0, 0, 0, 0, 0))
    return pl.pallas_call(
        _gdn_scan_kernel,
        grid=(nch,),
        in_specs=[pl.BlockSpec(blk, fwd(0)), pl.BlockSpec(blk, fwd(1)), pl.BlockSpec(blk, fwd(2)),
                  pl.BlockSpec((b, GDN_CHUNK, LANES), fwd(0)),
                  pl.BlockSpec(blk, bwd(0)), pl.BlockSpec(blk, bwd(1)), pl.BlockSpec(blk, bwd(2)),
                  pl.BlockSpec((b, GDN_CHUNK, LANES), bwd(0)),
                  pl.BlockSpec((1, LANES), lambda c: (0, 0)), pl.BlockSpec((1, LANES), lambda c: (0, 0)), st],
        out_specs=[pl.BlockSpec(blk, fwd(0)), pl.BlockSpec(blk, bwd(0)), st],
        out_shape=[jax.ShapeDtypeStruct((b, l, GDN_WIDTH), F32)] * 2 + [jax.ShapeDtypeStruct(s0.shape, F32)],
        scratch_shapes=[pltpu.VMEM(s0.shape, F32)],
        compiler_params=_params("arbitrary"), name="gdn_scan",
    )(qkvn, qkvn, qkvn, ba, qkvn, qkvn, qkvn, ba, alog_row, dtb_row, s0)


OUTPROJ_TM = 512


def _outproj_kernel(x_ref, mod_ref, da_ref, ys_ref, u_ref, dsk_ref, wg_ref, bg_ref,
                    of_ref, or_ref, z_ref, gnw_ref, avg_ref, w_ref, o_ref):
    y = ys_ref[...] + u_ref[...] * dsk_ref[...]
    zz = 0.5 * y * (1.0 + jnp.tanh(math.sqrt(2.0 / math.pi) * (y + 0.044715 * (y * y * y))))
    glu = zz * _sigmoid(jnp.dot(zz.astype(BF16), wg_ref[...], preferred_element_type=F32) + bg_ref[...])
    o = of_ref[...] + or_ref[...]
    ms = jnp.dot(o * o, avg_ref[...], precision=HP, preferred_element_type=F32)
    z = z_ref[...]
    gd = o * lax.rsqrt(ms + EPS) * gnw_ref[...] * (z * _sigmoid(z))
    acc = jnp.dot(da_ref[...], w_ref[0:DA_WIDTH, :], preferred_element_type=F32)
    acc += jnp.dot(glu.astype(BF16), w_ref[DA_WIDTH:DA_WIDTH + S5_WIDTH, :], preferred_element_type=F32)
    acc += jnp.dot(gd.astype(BF16), w_ref[DA_WIDTH + S5_WIDTH:, :], preferred_element_type=F32)
    o_ref[...] = x_ref[...] + mod_ref[0, 2:3, :] * acc


def outproj(x, mod, rows_per_mod, da, ys, u, dsk, w_glu, b_glu, o_f, o_r, z, gnw, avg, w_out):
    t, d = x.shape
    tm = min(OUTPROJ_TM, t)
    tpm = rows_per_mod // tm
    row = lambda i: (i, 0)
    const = lambda i: (0, 0)
    sw = S5_WIDTH
    return pl.pallas_call(
        _outproj_kernel,
        grid=(t // tm,),
        in_specs=[pl.BlockSpec((tm, d), row),
                  pl.BlockSpec((1, 6, d), lambda i: (i // tpm, 0, 0)),
                  pl.BlockSpec((tm, DA_WIDTH), row),
                  pl.BlockSpec((tm, sw), row), pl.BlockSpec((tm, sw), row),
                  pl.BlockSpec((1, sw), const), pl.BlockSpec((sw, sw), const), pl.BlockSpec((1, sw), const),
                  pl.BlockSpec((tm, GDN_WIDTH), row), pl.BlockSpec((tm, GDN_WIDTH), row),
                  pl.BlockSpec((tm, GDN_WIDTH), row),
                  pl.BlockSpec((1, GDN_WIDTH), const), pl.BlockSpec((GDN_WIDTH, GDN_WIDTH), const),
                  pl.BlockSpec((D_MIX, d), const)],
        out_specs=pl.BlockSpec((tm, d), row),
        out_shape=jax.ShapeDtypeStruct((t, d), F32),
        compiler_params=_params("parallel"), name="outproj",
    )(x, mod, da, ys, u, dsk, w_glu, b_glu, o_f, o_r, z, gnw, avg, w_out)


MOE_TM = 1024
MOE_EP = 4


def _pick_lowest(cur, idx, sentinel, axis):
    m = jnp.max(cur, axis=axis, keepdims=True)
    first = jnp.min(jnp.where(cur == m, idx, sentinel), axis=axis, keepdims=True)
    return idx == first


def _route(logits_t, bias):
    tm = logits_t.shape[1]
    neg = jnp.float32(-jnp.inf)
    scores = jax.nn.sigmoid(logits_t)
    biased = scores + bias
    b3 = biased.reshape(N_GROUPS, GROUP_SIZE, tm)
    eidx = lax.broadcasted_iota(jnp.int32, b3.shape, 1)
    m1 = jnp.max(b3, axis=1, keepdims=True)
    p1 = _pick_lowest(b3, eidx, GROUP_SIZE, 1)
    m2 = jnp.max(jnp.where(p1, neg, b3), axis=1, keepdims=True)
    gs = (m1 + m2).reshape(N_GROUPS, tm)
    gidx = lax.broadcasted_iota(jnp.int32, gs.shape, 0)
    gsel = jnp.zeros(gs.shape, jnp.bool_)
    cur = gs
    for _ in range(TOPK_GROUPS):
        pick = _pick_lowest(cur, gidx, N_GROUPS, 0)
        gsel = jnp.logical_or(gsel, pick)
        cur = jnp.where(pick, neg, cur)
    emask = jnp.broadcast_to(gsel.reshape(N_GROUPS, 1, tm), b3.shape)
    cur = jnp.where(emask, b3, neg).reshape(N_EXPERTS, tm)
    ridx = lax.broadcasted_iota(jnp.int32, cur.shape, 0)
    sel = jnp.zeros(cur.shape, jnp.bool_)
    for _ in range(TOP_K):
        pick = _pick_lowest(cur, ridx, N_EXPERTS, 0)
        sel = jnp.logical_or(sel, pick)
        cur = jnp.where(pick, neg, cur)
    w = jnp.where(sel, scores, 0.0)
    return w / jnp.sum(w, axis=0, keepdims=True) * ROUTED_SCALE


def _moe_kernel(x_ref, mod_ref, nw_ref, wr_ref, rb_ref, w1_ref, w3_ref, w2_ref, ws1_ref, ws3_ref, ws2_ref,
                fnw_ref, o_ref, h_sc, gate_sc, acc_sc, hid_sc, *, final_norm):
    e = pl.program_id(1)
    n_e = pl.num_programs(1)
    ep = w1_ref.shape[0]

    @pl.when(e == 0)
    def _():
        h = _rms_rows(x_ref[...]) * nw_ref[...] * (1.0 + mod_ref[0, 4:5, :]) + mod_ref[0, 3:4, :]
        hb = h.astype(BF16)
        h_sc[...] = hb
        a = jnp.dot(hb, ws1_ref[...], preferred_element_type=F32)
        b = jnp.dot(hb, ws3_ref[...], preferred_element_type=F32)
        acc_sc[...] = jnp.dot((a * _sigmoid(a) * b).astype(BF16), ws2_ref[...], preferred_element_type=F32)
        logits_t = lax.dot_general(wr_ref[...], h, NT_DIMS, precision=HP, preferred_element_type=F32)
        w = _route(logits_t, rb_ref[...])
        pad = jnp.zeros((LANES - N_EXPERTS, w.shape[1]), F32)
        gate_sc[...] = jnp.concatenate([w, pad], axis=0).T

    h = h_sc[...]
    lane = lax.broadcasted_iota(jnp.int32, (1, LANES), 1)
    for j in range(ep):
        a = jnp.dot(h, w1_ref[j], preferred_element_type=F32)
        b = jnp.dot(h, w3_ref[j], preferred_element_type=F32)
        g = jnp.sum(jnp.where(lane == e * ep + j, gate_sc[...], 0.0), axis=-1, keepdims=True)
        hid_sc[:, j * MOE_FFN:(j + 1) * MOE_FFN] = (a * _sigmoid(a) * b * g).astype(BF16)
    w2 = w2_ref[...].reshape(ep * MOE_FFN, w2_ref.shape[2])
    acc_sc[...] += jnp.dot(hid_sc[...], w2, preferred_element_type=F32)

    @pl.when(e == n_e - 1)
    def _():
        y = x_ref[...] + mod_ref[0, 5:6, :] * acc_sc[...]
        if final_norm:
            y = _rms_rows(y) * fnw_ref[...]
        o_ref[...] = y


def moe_sublayer(x, mod, rows_per_mod, norm_w, w_router_t, router_bias, weights, layer, final_w, final_norm):
    t, d = x.shape
    tm = min(MOE_TM, rows_per_mod)
    assert t % tm == 0 and rows_per_mod % tm == 0
    w1, w3, w2, ws1, ws3, ws2 = weights
    ep = MOE_EP
    assert N_EXPERTS % ep == 0
    tiles_per_mod = rows_per_mod // tm
    f = MOE_FFN
    return pl.pallas_call(
        functools.partial(_moe_kernel, final_norm=final_norm),
        grid=(t // tm, N_EXPERTS // ep),
        in_specs=[
            pl.BlockSpec((tm, d), lambda i, e: (i, 0)),
            pl.BlockSpec((1, 6, d), lambda i, e: (i // tiles_per_mod, 0, 0)),
            pl.BlockSpec((1, d), lambda i, e: (0, 0)),
            pl.BlockSpec((N_EXPERTS, d), lambda i, e: (0, 0)),
            pl.BlockSpec((N_EXPERTS, 1), lambda i, e: (0, 0)),
            pl.BlockSpec((None, ep, d, f), lambda i, e: (layer, e, 0, 0)),
            pl.BlockSpec((None, ep, d, f), lambda i, e: (layer, e, 0, 0)),
            pl.BlockSpec((None, ep, f, d), lambda i, e: (layer, e, 0, 0)),
            pl.BlockSpec((None, d, f), lambda i, e: (layer, 0, 0)),
            pl.BlockSpec((None, d, f), lambda i, e: (layer, 0, 0)),
            pl.BlockSpec((None, f, d), lambda i, e: (layer, 0, 0)),
            pl.BlockSpec((1, d), lambda i, e: (0, 0)),
        ],
        out_specs=pl.BlockSpec((tm, d), lambda i, e: (i, 0)),
        out_shape=jax.ShapeDtypeStruct((t, d), F32),
        scratch_shapes=[
            pltpu.VMEM((tm, d), BF16),
            pltpu.VMEM((tm, LANES), F32),
            pltpu.VMEM((tm, d), F32),
            pltpu.VMEM((tm, ep * MOE_FFN), BF16),
        ],
        compiler_params=_params("parallel", "arbitrary"), name="moe_sublayer",
    )(x, mod, norm_w.reshape(1, d), w_router_t, router_bias.reshape(N_EXPERTS, 1), w1, w3, w2, ws1, ws3, ws2,
      final_w.reshape(1, d))


def kernel(x, c, ctx, c_ctx, norm1_w, norm2_w, w_mod, b_mod, w_in, w_out, da_lambda, da_subln_w,
           s5_lam_re, s5_lam_im, s5_log_step, s5_b_re, s5_b_im, s5_c_re, s5_c_im, s5_d, s5_w_glu, s5_b_glu,
           gdn_conv_w, gdn_a_log, gdn_dt_bias, gdn_norm_w,
           moe_w_router, moe_router_bias, moe_w1, moe_w3, moe_w2, moe_ws1, moe_ws3, moe_ws2,
           final_norm_w):
    b, n, d = x.shape
    nc = ctx.shape[1]
    cos_t, sin_t = _rope_tables(n)
    cond = jnp.zeros((2 * b, d), F32).at[:b].set(c).at[b].set(c_ctx)
    xl = x.reshape(b * n, d)
    xc = ctx.reshape(b * nc, d)
    head_avg = jnp.kron(jnp.eye(GDN_HEADS, dtype=F32), jnp.full((GDN_DV, GDN_DV), 1.0 / GDN_DV, F32))
    s_zero = jnp.zeros((b, 2, GDN_HEADS // 2, LANES, LANES), F32)
    moe_w = tuple(w.astype(BF16) for w in (moe_w1, moe_w3, moe_w2, moe_ws1, moe_ws3, moe_ws2))
    for i in range(DEPTH):
        ctx_out = i < DEPTH - 1
        last = i == DEPTH - 1
        lam_init = 0.8 - 0.6 * math.exp(-0.3 * i)
        mod_all = mod_proj(cond, w_mod, b_mod, i).reshape(2 * b, 6, d)
        mod, modc = mod_all[:b], mod_all[b:b + 1]

        w_in_p = jnp.pad(w_in[i], ((0, 0), (0, IN_PAD - IN_WIDTH))).astype(BF16)
        q, k, v, u, ub, gq, z, ba = inproj(xl, mod, n, norm1_w[i], w_in_p, cos_t, sin_t, rope=True)
        qc, kc, vc, uc, ubc, gqc, zc, bac = inproj(xc, modc, b * nc, norm1_w[i], w_in_p, cos_t, sin_t, rope=False)

        lq1, lk1, lq2, lk2 = da_lambda[i].astype(F32)
        lam = jnp.exp(jnp.sum(lq1 * lk1)) - jnp.exp(jnp.sum(lq2 * lk2)) + lam_init
        lam_row = jnp.full((1, LANES), lam, F32)
        da = diff_attention(q, [kc, k], [vc, v], (nc, n), n, lam_row, da_subln_w[i], lam_init)

        tables = _s5_tables(s5_lam_re[i], s5_lam_im[i], s5_log_step[i], s5_b_re[i], s5_b_im[i],
                            s5_c_re[i], s5_c_im[i])
        ysc, ysl = s5_scan(ubc, ub, b, tables)

        a_lanes = slice(GDN_A_LANE, 2 * GDN_A_LANE)
        alog_row = jnp.zeros((1, LANES), F32).at[0, a_lanes].set(gdn_a_log[i].astype(F32).reshape(-1))
        dtb_row = jnp.zeros((1, LANES), F32).at[0, a_lanes].set(gdn_dt_bias[i].astype(F32).reshape(-1))
        gn = gdn_prep(gq, gdn_conv_w[i], n)
        gnc = gdn_prep(gqc, gdn_conv_w[i], nc)
        ofc, orc, s_ctx = gdn_scan(gnc.reshape(b, nc, GDN_QKV_W), bac.reshape(b, nc, LANES), s_zero, alog_row, dtb_row)
        of, orv, _ = gdn_scan(gn.reshape(b, n, GDN_QKV_W), ba.reshape(b, n, LANES), s_ctx, alog_row, dtb_row)

        w_out_b = w_out[i].astype(BF16)
        dsk = s5_d[i].astype(F32).reshape(1, S5_WIDTH)
        wg = s5_w_glu[i].astype(BF16)
        bg = s5_b_glu[i].astype(F32).reshape(1, S5_WIDTH)
        gnw = jnp.tile(gdn_norm_w[i].astype(F32), GDN_HEADS).reshape(1, GDN_WIDTH)
        xl = outproj(xl, mod, n, da, ysl, u, dsk, wg, bg,
                     of.reshape(b * n, GDN_WIDTH), orv.reshape(b * n, GDN_WIDTH), z, gnw, head_avg, w_out_b)
        if ctx_out:
            dac = diff_attention(qc, [kc], [vc], (nc,), nc, lam_row, da_subln_w[i], lam_init)
            xc = outproj(xc, modc, b * nc, dac, ysc, uc, dsk, wg, bg,
                         ofc.reshape(b * nc, GDN_WIDTH), orc.reshape(b * nc, GDN_WIDTH), zc, gnw, head_avg, w_out_b)

        wr_t = moe_w_router[i].T
        xl = moe_sublayer(xl, mod, n, norm2_w[i], wr_t, moe_router_bias[i], moe_w, i, final_norm_w, last)
        if ctx_out:
            xc = moe_sublayer(xc, modc, b * nc, norm2_w[i], wr_t, moe_router_bias[i], moe_w, i, final_norm_w, False)
    return xl.reshape(b, n, d)
```

```python
import functools
import math

import jax
import jax.numpy as jnp
import numpy as np
from jax import lax
from jax.experimental import pallas as pl
from jax.experimental.pallas import tpu as pltpu

F32 = jnp.float32
BF16 = jnp.bfloat16

D_MODEL = 1024
DEPTH = 2
GRID_W = 64
EPS = 1e-6

DA_HEADS = 4
DA_HEAD_DIM = D_MODEL // 16
DA_V_DIM = 2 * DA_HEAD_DIM
DA_WIDTH = DA_HEADS * DA_V_DIM
ROPE_THETA = 10000.0

S5_WIDTH = D_MODEL // 4
S5_GROUP = 16
S5_GROUPS = S5_WIDTH // S5_GROUP
S5_STATE = 64

GDN_HEADS = 4
GDN_DK = D_MODEL // 16
GDN_DV = D_MODEL // 16
GDN_WIDTH = GDN_HEADS * GDN_DV
GDN_CONV = 5
GDN_CHUNK = 64

D_MIX = DA_WIDTH + S5_WIDTH + GDN_WIDTH
DA_QK_W = 2 * DA_HEADS * DA_HEAD_DIM
GDN_QKV_W = 2 * GDN_HEADS * GDN_DK + GDN_HEADS * GDN_DV
IN_SIZES = (DA_QK_W, DA_QK_W, DA_WIDTH, S5_WIDTH, GDN_QKV_W, GDN_WIDTH, 2 * GDN_HEADS, 2 * GDN_HEADS)
IN_WIDTH = sum(IN_SIZES)

N_EXPERTS = 64
TOP_K = 8
N_GROUPS = 8
GROUP_SIZE = N_EXPERTS // N_GROUPS
TOPK_GROUPS = 4
MOE_FFN = D_MODEL // 4
ROUTED_SCALE = 2.5

LANES = 128
SUBLANES = 8
V7X_VMEM_BYTES = 64 * 1024 * 1024
VMEM_LIMIT_BYTES = V7X_VMEM_BYTES * 7 // 8

GDN_A_LANE = 2 * GDN_HEADS

HP = lax.Precision.HIGHEST
NT_DIMS = (((1,), (1,)), ((), ()))


def _params(*sem):
    return pltpu.CompilerParams(dimension_semantics=sem, vmem_limit_bytes=VMEM_LIMIT_BYTES)


def _sigmoid(x):
    return 0.5 * (1.0 + jnp.tanh(0.5 * x))


def _rms_rows(x):
    return x * lax.rsqrt(jnp.mean(x * x, axis=-1, keepdims=True) + EPS)


MOD_TN = 768


def _mod_kernel(c_ref, w_ref, b_ref, o_ref):
    c = c_ref[...]
    act = c * _sigmoid(c)
    o_ref[...] = jnp.dot(act, w_ref[...], precision=HP, preferred_element_type=F32) + b_ref[...]


def mod_proj(cond, w_mod, b_mod, layer):
    r, d = cond.shape
    depth, _, n = w_mod.shape
    return pl.pallas_call(
        _mod_kernel,
        grid=(n // MOD_TN,),
        in_specs=[pl.BlockSpec((r, d), lambda j: (0, 0)),
                  pl.BlockSpec((None, d, MOD_TN), lambda j: (layer, 0, j)),
                  pl.BlockSpec((None, 1, MOD_TN), lambda j: (layer, 0, j))],
        out_specs=pl.BlockSpec((r, MOD_TN), lambda j: (0, j)),
        out_shape=jax.ShapeDtypeStruct((r, n), F32),
        compiler_params=_params("parallel"), name="mod_proj",
    )(cond, w_mod, b_mod.reshape(depth, 1, n))


IN_PAD = 2944
INPROJ_TM = 512
O_Q, O_K, O_V, O_U, O_G, O_Z, O_BA = 0, 512, 1024, 1536, 1792, 2560, 2816


def _rope_apply(x, cos, sin):
    lane = lax.broadcasted_iota(jnp.int32, x.shape, 1)
    up = pltpu.roll(x, LANES - 16, 1)
    dn = pltpu.roll(x, 16, 1)
    partner = jnp.where((lane & 31) < 16, up, dn)
    return x * cos + partner * sin


def _inproj_kernel(x_ref, mod_ref, nw_ref, w_ref, cos_ref, sin_ref,
                   q_ref, k_ref, v_ref, u_ref, ub_ref, g_ref, z_ref, ba_ref, *, rope):
    h = (_rms_rows(x_ref[...]) * nw_ref[...] * (1.0 + mod_ref[0, 1:2, :]) + mod_ref[0, 0:1, :]).astype(BF16)

    def proj(lo, hi):
        return jnp.dot(h, w_ref[:, lo:hi], preferred_element_type=F32)

    scale = DA_HEAD_DIM ** -0.5 * math.log2(math.e)
    q_all = proj(O_Q, O_K)
    k_all = proj(O_K, O_V)
    for hd in range(DA_HEADS):
        lo = hd * LANES
        qs = q_all[:, lo:lo + LANES]
        ks = k_all[:, lo:lo + LANES]
        if rope:
            qs = _rope_apply(qs, cos_ref[...], sin_ref[...])
            ks = _rope_apply(ks, cos_ref[...], sin_ref[...])
        q_ref[:, lo:lo + LANES] = (qs * scale).astype(BF16)
        k_ref[:, lo:lo + LANES] = ks.astype(BF16)
    v_ref[...] = proj(O_V, O_U).astype(BF16)
    u = proj(O_U, O_G)
    u_ref[...] = u
    ub_ref[...] = u.astype(BF16)
    g_ref[...] = proj(O_G, O_Z)
    z_ref[...] = proj(O_Z, O_BA)
    ba_ref[...] = proj(O_BA, IN_PAD)


def inproj(x, mod, rows_per_mod, norm_w, w_pad, cos_t, sin_t, rope):
    t, d = x.shape
    tm = min(INPROJ_TM, t)
    tpm = rows_per_mod // tm
    npos = cos_t.shape[0] // tm
    row = lambda i: (i, 0)
    widths = (DA_QK_W, DA_QK_W, DA_WIDTH, S5_WIDTH, S5_WIDTH, GDN_QKV_W, GDN_WIDTH, LANES)
    dtypes = (BF16, BF16, BF16, F32, BF16, F32, F32, F32)
    return pl.pallas_call(
        functools.partial(_inproj_kernel, rope=rope),
        grid=(t // tm,),
        in_specs=[pl.BlockSpec((tm, d), row),
                  pl.BlockSpec((1, 6, d), lambda i: (i // tpm, 0, 0)),
                  pl.BlockSpec((1, d), lambda i: (0, 0)),
                  pl.BlockSpec((d, IN_PAD), lambda i: (0, 0)),
                  pl.BlockSpec((tm, LANES), lambda i: (i % npos, 0)),
                  pl.BlockSpec((tm, LANES), lambda i: (i % npos, 0))],
        out_specs=[pl.BlockSpec((tm, w), row) for w in widths],
        out_shape=[jax.ShapeDtypeStruct((t, w), dt) for w, dt in zip(widths, dtypes)],
        compiler_params=_params("parallel"), name="inproj",
    )(x, mod, norm_w.reshape(1, d), w_pad, cos_t, sin_t)


def _rope_tables(n):
    nf = DA_HEAD_DIM // 4
    t = jnp.arange(n, dtype=jnp.int32)
    inv = ROPE_THETA ** (-jnp.arange(nf, dtype=F32) / nf)
    ang_r = (t // GRID_W).astype(F32)[:, None] * inv
    ang_c = (t % GRID_W).astype(F32)[:, None] * inv
    cos64 = jnp.concatenate([jnp.cos(ang_r), jnp.cos(ang_r), jnp.cos(ang_c), jnp.cos(ang_c)], axis=-1)
    sin64 = jnp.concatenate([-jnp.sin(ang_r), jnp.sin(ang_r), -jnp.sin(ang_c), jnp.sin(ang_c)], axis=-1)
    return jnp.tile(cos64, (1, 2)), jnp.tile(sin64, (1, 2))


ATTN_TQ = 512
ATTN_KC = 512


def _attn_kernel(*refs, n_kv, kv_rows, chunks, lam_init):
    q_ref = refs[0]
    k_refs = refs[1:1 + n_kv]
    v_refs = refs[1 + n_kv:1 + 2 * n_kv]
    lam_ref, w_ref, o_ref, v1_sc = refs[1 + 2 * n_kv:]

    @pl.when(pl.program_id(2) == 0)
    def _():
        off = 0
        for ki, rows in enumerate(kv_rows):
            v1_sc[off:off + rows, :DA_V_DIM] = v_refs[ki][...]
            v1_sc[off:off + rows, DA_V_DIM:] = jnp.ones((rows, DA_V_DIM), BF16)
            off += rows

    q = q_ref[...]
    tq = q.shape[0]
    lane = lax.broadcasted_iota(jnp.int32, q.shape, 1)
    zero = jnp.zeros_like(q)
    qq = jnp.concatenate([jnp.where(lane < DA_HEAD_DIM, q, zero), jnp.where(lane >= DA_HEAD_DIM, q, zero)], axis=0)
    m = jnp.full((2 * tq, 1), -jnp.inf, F32)
    acc = jnp.zeros((2 * tq, 2 * DA_V_DIM), F32)
    def scores(chunk):
        ki, start, _, size = chunk
        return lax.dot_general(qq, k_refs[ki][start:start + size, :], NT_DIMS, preferred_element_type=F32)

    s_next = scores(chunks[0])
    for ci, (ki, start, off, size) in enumerate(chunks):
        s = s_next
        if ci + 1 < len(chunks):
            s_next = scores(chunks[ci + 1])
        m_new = jnp.maximum(m, jnp.max(s, axis=-1, keepdims=True))
        p = jnp.exp2(s - m_new).astype(BF16)
        acc = jnp.exp2(m - m_new) * acc + jnp.dot(p, v1_sc[off:off + size, :], preferred_element_type=F32)
        m = m_new
    o = acc[:, :DA_V_DIM] / acc[:, DA_V_DIM:]
    od = o[:tq] - lam_ref[...] * o[tq:]
    o_ref[...] = (_rms_rows(od) * w_ref[...] * (1.0 - lam_init)).astype(o_ref.dtype)


def diff_attention(q, ks, vs, kv_rows, q_rows, lam_row, subln_w, lam_init):
    t = q.shape[0]
    b = t // q_rows
    tq = min(ATTN_TQ, q_rows)
    nq = q_rows // tq
    chunks, off = [], 0
    for ki, rows in enumerate(kv_rows):
        kc = min(ATTN_KC, rows)
        chunks += [(ki, s, off + s, kc) for s in range(0, rows, kc)]
        off += rows
    qmap = lambda bi, h, qi: (bi * nq + qi, h)
    kvmap = lambda bi, h, qi: (bi, h)
    const = lambda bi, h, qi: (0, 0)
    return pl.pallas_call(
        functools.partial(_attn_kernel, n_kv=len(ks), kv_rows=tuple(kv_rows), chunks=tuple(chunks),
                          lam_init=lam_init),
        grid=(b, DA_HEADS, nq),
        in_specs=([pl.BlockSpec((tq, LANES), qmap)]
                  + [pl.BlockSpec((rows, LANES), kvmap) for rows in kv_rows] * 2
                  + [pl.BlockSpec((1, LANES), const)] * 2),
        out_specs=pl.BlockSpec((tq, LANES), qmap),
        out_shape=jax.ShapeDtypeStruct((t, DA_WIDTH), BF16),
        scratch_shapes=[pltpu.VMEM((off, 2 * DA_V_DIM), BF16)],
        compiler_params=_params("parallel", "parallel", "arbitrary"), name="diff_attention",
    )(q, *ks, *vs, lam_row, subln_w.reshape(1, LANES))


S5_LC = 64
S5_CW = S5_LC * S5_GROUP
S5_SW = 2 * S5_STATE


def _s5_discretize(lam_re, lam_im, log_step, b_re, b_im):
    lr, li = lam_re.astype(F32), lam_im.astype(F32)
    step = jnp.exp(log_step.astype(F32))[..., None]
    mag = jnp.exp(lr * step)
    ab_re, ab_im = mag * jnp.cos(li * step), mag * jnp.sin(li * step)
    den = lr * lr + li * li
    nr, ni = ab_re - 1.0, ab_im
    f_re = (nr * lr + ni * li) / den
    f_im = (ni * lr - nr * li) / den
    br, bi = b_re.astype(F32), b_im.astype(F32)
    bb_re = f_re[..., None] * br - f_im[..., None] * bi
    bb_im = f_re[..., None] * bi + f_im[..., None] * br
    return bb_re, bb_im


def _s5_tables(lam_re, lam_im, log_step, b_re, b_im, c_re, c_im):
    lc, g, nl = S5_LC, S5_GROUPS, lam_re.shape[0] * 2
    merge = lambda x: x.astype(F32).reshape((nl,) + x.shape[2:])
    lam_re, lam_im, log_step, b_re, b_im, cr, ci = map(merge, (lam_re, lam_im, log_step, b_re, b_im, c_re, c_im))
    rev = jnp.arange(nl) % 2 == 1

    def flip_rev(x, axis):
        return jnp.where(rev.reshape((1, nl) + (1,) * (x.ndim - 2)), jnp.flip(x, axis), x)

    step = jnp.exp(log_step)[..., None]
    bb_re, bb_im = _s5_discretize(lam_re, lam_im, log_step, b_re, b_im)
    tau = jnp.arange(lc + 1, dtype=F32).reshape(lc + 1, 1, 1, 1)
    mag = jnp.exp(tau * (lam_re * step))
    ang = tau * (lam_im * step)
    pr, pi = mag * jnp.cos(ang), mag * jnp.sin(ang)
    abr = pr[..., None] * bb_re - pi[..., None] * bb_im
    abi = pr[..., None] * bb_im + pi[..., None] * bb_re
    kern = (jnp.einsum('lgkp,tlgph->tlgkh', cr, abr[:lc], precision=HP)
            - jnp.einsum('lgkp,tlgph->tlgkh', ci, abi[:lc], precision=HP))
    ab = jnp.concatenate([abr[:lc], abi[:lc]], axis=3)
    bmat = flip_rev(ab[::-1], 0).transpose(1, 2, 4, 0, 3).reshape(nl, g, S5_CW, S5_SW)
    pro, pio = pr[1:, :, :, None, :], pi[1:, :, :, None, :]
    ca = jnp.concatenate([cr * pro - ci * pio, -(cr * pio + ci * pro)], axis=4)
    cmat = flip_rev(ca, 0).transpose(1, 2, 4, 3, 0).reshape(nl, g, S5_SW, S5_CW)
    strip = jnp.concatenate([jnp.zeros((lc - 1,) + kern.shape[1:], F32), kern], axis=0)
    strip = jnp.pad(flip_rev(strip, 0), ((0, 1),) + ((0, 0),) * 4).transpose(1, 2, 4, 3, 0)
    alr, ali = pr[lc], pi[lc]
    k1 = jnp.repeat(jnp.concatenate([alr, alr], axis=-1), SUBLANES, axis=1)
    k2 = jnp.repeat(jnp.concatenate([-ali, ali], axis=-1), SUBLANES, axis=1)
    return bmat.astype(BF16), cmat.astype(BF16), _s5_toeplitz(strip), k1, k2


def _s5_toeplitz_kernel(strip_ref, m_ref):
    lc = S5_LC
    lane = lax.broadcasted_iota(jnp.int32, (lc, 2 * lc), 1)
    for h in range(S5_GROUP):
        for kp in range(S5_GROUP // 2):
            halves = []
            for half in range(2):
                row = jnp.broadcast_to(strip_ref[0, 0, h, 2 * kp + half:2 * kp + half + 1, :], (lc, 2 * lc))
                base = lc + 1 if half == 0 else 1
                halves.append(pltpu.roll(row, base, 1, stride=1, stride_axis=0))
            tile = jnp.where(lane < lc, halves[0], halves[1])
            m_ref[0, 0, h * lc:(h + 1) * lc, kp * 2 * lc:(kp + 1) * 2 * lc] = tile.astype(m_ref.dtype)


def _s5_toeplitz(strips):
    nd, g = strips.shape[:2]
    return pl.pallas_call(
        _s5_toeplitz_kernel,
        grid=(nd, g),
        in_specs=[pl.BlockSpec((1, 1) + strips.shape[2:], lambda d, gi: (d, gi, 0, 0, 0))],
        out_specs=pl.BlockSpec((1, 1, S5_CW, S5_CW), lambda d, gi: (d, gi, 0, 0)),
        out_shape=jax.ShapeDtypeStruct((nd, g, S5_CW, S5_CW), BF16),
        compiler_params=_params("parallel", "parallel"), name="s5_toeplitz",
    )(strips)


def _s5_local_kernel(uc_ref, ul_ref, bm_ref, s_ref):
    u = jnp.concatenate([uc_ref[0], ul_ref[0]], axis=0)
    s = jnp.dot(u, bm_ref[0, 0], preferred_element_type=F32)
    s_ref[0] = s.reshape(s_ref.shape[1:])


def _s5_carry_kernel(s_ref, k1_ref, k2_ref, xin_ref, *, n_ctx, n_chunks):
    d = pl.program_id(0)
    k1, k2 = k1_ref[0], k2_ref[0]

    def body(k, x):
        rev = jnp.where(k < n_ctx, n_ctx - 1 - k, n_chunks + n_ctx - 1 - k)
        c = jnp.where(d == 0, k, rev)
        xin_ref[0, c] = x
        return k1 * x + k2 * pltpu.roll(x, S5_STATE, 1) + s_ref[0, c]

    lax.fori_loop(0, n_chunks, body, jnp.zeros(k1.shape, F32))


def _s5_out_kernel(uc_ref, ul_ref, m_ref, xin_ref, cm_ref, yc_ref, yl_ref):
    u = jnp.concatenate([uc_ref[0], ul_ref[0]], axis=0)
    acc = jnp.dot(u, m_ref[0, 0], preferred_element_type=F32)
    acc += jnp.dot(u, m_ref[1, 0], preferred_element_type=F32)
    for d in range(2):
        xin = xin_ref[d].reshape(u.shape[0], S5_SW).astype(BF16)
        acc += jnp.dot(xin, cm_ref[d, 0], preferred_element_type=F32)
    rc = yc_ref.shape[1]
    yc_ref[0] = acc[:rc].astype(yc_ref.dtype)
    yl_ref[0] = acc[rc:].astype(yl_ref.dtype)


def s5_scan(u_ctx, u_lat, b, tables, layer):
    bmat, cmat, mmat, k1, k2 = tables
    g, lc = S5_GROUPS, S5_LC
    n_ctx, n_lat = u_ctx.shape[0] // b, u_lat.shape[0] // b
    assert b == SUBLANES and n_lat % lc == 0 and n_ctx % lc == 0
    rc, rl = n_ctx // lc * b, n_lat // lc * b
    nch = (n_ctx + n_lat) // lc

    def to_groups(x, rows):
        x = lax.optimization_barrier(jnp.swapaxes(x.reshape(b, rows // lc, lc, S5_WIDTH), 2, 3))
        return x.reshape(b, rows // lc, g, S5_CW).transpose(2, 1, 0, 3).reshape(g, rows // lc * b, S5_CW)

    def from_groups(y, rows):
        y = lax.optimization_barrier(y.reshape(g, rows // lc, b, S5_CW).transpose(2, 1, 0, 3))
        return jnp.swapaxes(y.reshape(b, rows // lc, S5_WIDTH, lc), 2, 3).reshape(b * rows, S5_WIDTH)

    ugc, ugl = to_groups(u_ctx, n_ctx), to_groups(u_lat, n_lat)
    s = pl.pallas_call(
        _s5_local_kernel,
        grid=(2, g),
        in_specs=[pl.BlockSpec((1, rc, S5_CW), lambda d, gi: (gi, 0, 0)),
                  pl.BlockSpec((1, rl, S5_CW), lambda d, gi: (gi, 0, 0)),
                  pl.BlockSpec((1, 1, S5_CW, S5_SW), lambda d, gi: (2 * layer + d, gi, 0, 0))],
        out_specs=pl.BlockSpec((1, nch, b, S5_SW), lambda d, gi: (d, 0, gi, 0)),
        out_shape=jax.ShapeDtypeStruct((2, nch, g * b, S5_SW), F32),
        compiler_params=_params("arbitrary", "arbitrary"), name="s5_local",
    )(ugc, ugl, bmat)
    xin = pl.pallas_call(
        functools.partial(_s5_carry_kernel, n_ctx=n_ctx // lc, n_chunks=nch),
        grid=(2,),
        in_specs=[pl.BlockSpec((1, nch, g * b, S5_SW), lambda d: (d, 0, 0, 0)),
                  pl.BlockSpec((1, g * b, S5_SW), lambda d: (2 * layer + d, 0, 0)),
                  pl.BlockSpec((1, g * b, S5_SW), lambda d: (2 * layer + d, 0, 0))],
        out_specs=pl.BlockSpec((1, nch, g * b, S5_SW), lambda d: (d, 0, 0, 0)),
        out_shape=jax.ShapeDtypeStruct((2, nch, g * b, S5_SW), F32),
        compiler_params=_params("arbitrary"), name="s5_carry",
    )(s, k1, k2)
    yc, yl = pl.pallas_call(
        _s5_out_kernel,
        grid=(g,),
        in_specs=[pl.BlockSpec((1, rc, S5_CW), lambda gi: (gi, 0, 0)),
                  pl.BlockSpec((1, rl, S5_CW), lambda gi: (gi, 0, 0)),
                  pl.BlockSpec((2, 1, S5_CW, S5_CW), lambda gi: (layer, gi, 0, 0)),
                  pl.BlockSpec((2, nch, b, S5_SW), lambda gi: (0, 0, gi, 0)),
                  pl.BlockSpec((2, 1, S5_SW, S5_CW), lambda gi: (layer, gi, 0, 0))],
        out_specs=[pl.BlockSpec((1, rc, S5_CW), lambda gi: (gi, 0, 0)),
                   pl.BlockSpec((1, rl, S5_CW), lambda gi: (gi, 0, 0))],
        out_shape=[jax.ShapeDtypeStruct((g, rc, S5_CW), BF16), jax.ShapeDtypeStruct((g, rl, S5_CW), BF16)],
        compiler_params=_params("arbitrary"), name="s5_out",
    )(ugc, ugl, mmat, xin, cmat)
    return from_groups(yc, n_ctx), from_groups(yl, n_lat)


def _gdn_prep_kernel(x_ref, w_ref, o_ref):
    j = pl.program_id(1)
    x = x_ref[...]
    n = x.shape[0]
    row = lax.broadcasted_iota(jnp.int32, x.shape, 0)
    half = GDN_CONV // 2
    acc = x * w_ref[half:half + 1, :]
    for sh in range(1, half + 1):
        acc += jnp.where(row >= sh, pltpu.roll(x, sh, 0), 0.0) * w_ref[half - sh:half - sh + 1, :]
        acc += jnp.where(row < n - sh, pltpu.roll(x, n - sh, 0), 0.0) * w_ref[half + sh:half + sh + 1, :]
    a = acc * _sigmoid(acc)
    lane = lax.broadcasted_iota(jnp.int32, x.shape, 1)
    lo = lane < GDN_DK
    sq = a * a
    s_lo = jnp.sum(jnp.where(lo, sq, 0.0), axis=-1, keepdims=True)
    s_hi = jnp.sum(jnp.where(lo, 0.0, sq), axis=-1, keepdims=True)
    nrm = a * lax.rsqrt(jnp.where(lo, s_lo, s_hi) + EPS)
    q_blocks = GDN_HEADS * GDN_DK // LANES
    nrm = nrm * jnp.where(j < q_blocks, GDN_DK ** -0.5, 1.0)
    o_ref[...] = jnp.where(j < 2 * q_blocks, nrm, a)


def gdn_prep(qkv, conv_w, seg):
    t, w = qkv.shape
    return pl.pallas_call(
        _gdn_prep_kernel,
        grid=(t // seg, w // LANES),
        in_specs=[pl.BlockSpec((seg, LANES), lambda s, j: (s, j)),
                  pl.BlockSpec((GDN_CONV, LANES), lambda s, j: (0, j))],
        out_specs=pl.BlockSpec((seg, LANES), lambda s, j: (s, j)),
        out_shape=jax.ShapeDtypeStruct((t, w), F32),
        compiler_params=_params("parallel", "arbitrary"), name="gdn_prep",
    )(qkv, conv_w)


GDN_SHIFT = GDN_CHUNK.bit_length() - 1
assert 1 << GDN_SHIFT == GDN_CHUNK == GDN_DK == GDN_DV


def _bd(x):
    x2 = jnp.concatenate([x, x], axis=0)
    r = lax.broadcasted_iota(jnp.int32, x2.shape, 0)
    l = lax.broadcasted_iota(jnp.int32, x2.shape, 1)
    return jnp.where((r >> GDN_SHIFT) == (l >> GDN_SHIFT), x2, jnp.zeros_like(x2))


def _mm(a, b):
    return jnp.dot(a.astype(BF16), b.astype(BF16), preferred_element_type=F32)


def _dot01(m01, x):
    hi = x.astype(BF16)
    r1 = x - hi.astype(F32)
    mid = r1.astype(BF16)
    lo = (r1 - mid.astype(F32)).astype(BF16)
    dot = lambda p: jnp.dot(m01, p, preferred_element_type=F32)
    return (dot(lo) + dot(mid)) + dot(hi)


def _gdn_chunks(insts):
    c = GDN_CHUNK
    n = len(insts)
    every = range(n)
    q, k, v, beta, gcol, s_bd, rev = (list(t) for t in zip(*insts))
    i = lax.broadcasted_iota(jnp.int32, (c, LANES), 0)
    j = lax.broadcasted_iota(jnp.int32, (c, LANES), 1) & (c - 1)
    ti = lax.broadcasted_iota(jnp.int32, (c, c), 0)
    tj = lax.broadcasted_iota(jnp.int32, (c, c), 1)
    causal = [i <= j if r else i >= j for r in rev]
    strict = [i < j if r else i > j for r in rev]
    upto = [i >= j if r else i <= j for r in rev]
    lmat = [(tj >= ti if r else tj <= ti).astype(BF16) for r in rev]
    ones = jnp.ones((c, c), BF16)
    eye = jnp.where(i == j, 1.0, 0.0)
    lg = [_dot01(lmat[t], gcol[t]) for t in every]
    rg = [_dot01(ones, jnp.where(upto[t], gcol[t], 0.0)) for t in every]
    decay = [jnp.where(causal[t], jnp.exp(jnp.where(causal[t], lg[t] - rg[t], 0.0)), 0.0) for t in every]
    kb = [k[t] * beta[t] for t in every]
    k_bd = [_bd(k[t].astype(BF16)) for t in every]
    a = [jnp.where(strict[t], lax.dot_general(kb[t].astype(BF16), k_bd[t], NT_DIMS, preferred_element_type=F32)
                   * decay[t], 0.0) for t in every]
    qk = [lax.dot_general(q[t].astype(BF16), k_bd[t], NT_DIMS, preferred_element_type=F32) * decay[t] for t in every]
    eg = [jnp.exp(lg[t]) for t in every]
    base = 3
    same = [(i >> sh) == (j >> sh) for sh in range(base, GDN_SHIFT)]
    x = [jnp.where(same[0], a[t], 0.0) for t in every]
    p = [eye - x[t] for t in every]
    for _ in range(base - 1):
        x = [_mm(x[t], _bd(x[t])) for t in every]
        p = [p[t] + _mm(p[t], _bd(x[t])) for t in every]
    for lvl, inner in enumerate(same):
        outer_same = same[lvl + 1] if lvl + 1 < len(same) else True
        join = jnp.logical_and(outer_same, jnp.logical_not(inner))
        tl = [_mm(p[t], _bd(jnp.where(join, a[t], 0.0))) for t in every]
        p = [p[t] - _mm(tl[t], _bd(p[t])) for t in every]
    u = [_mm(p[t], _bd(v[t] * beta[t])) for t in every]
    w = [_mm(p[t], _bd(kb[t] * eg[t])) for t in every]
    v_new = [u[t] - _mm(w[t], s_bd[t]) for t in every]
    o_state = [_mm(q[t] * eg[t], s_bd[t]) for t in every]
    o = [o_state[t] + _mm(qk[t], _bd(v_new[t])) for t in every]
    g_last = [lg[t][0:1, :] if rev[t] else lg[t][c - 1:c, :] for t in every]
    k_dec = [k[t] * jnp.exp(g_last[t] - lg[t]) for t in every]
    upd = [jnp.dot(k_dec[t].T.astype(BF16), v_new[t].astype(BF16), preferred_element_type=F32) for t in every]
    r2 = lax.broadcasted_iota(jnp.int32, (LANES, LANES), 0)
    l2 = lax.broadcasted_iota(jnp.int32, (LANES, LANES), 1)
    diag = (r2 >> GDN_SHIFT) == (l2 >> GDN_SHIFT)
    s_new = [s_bd[t] * jnp.exp(g_last[t]) + jnp.where(diag, upd[t], 0.0) for t in every]
    return list(zip(o, s_new))


GDN_BATCH_UNROLL = 8


def _gdn_scan_kernel(qf, kf, vf, baf, qr, kr, vr, bar, alog_ref, dtb_ref, s0_ref,
                     of_ref, or_ref, sfin_ref, s_sc):
    c = pl.program_id(0)

    @pl.when(c == 0)
    def _():
        s_sc[...] = s0_ref[...]

    lane = lax.broadcasted_iota(jnp.int32, (GDN_CHUNK, LANES), 1)
    first = lane < GDN_DK
    ins = ((qf, kf, vf, baf, of_ref), (qr, kr, vr, bar, or_ref))

    def body(it, carry):
        work = []
        for bb in range(GDN_BATCH_UNROLL):
            b = it * GDN_BATCH_UNROLL + bb
            for d in range(2):
                q_ref, k_ref, v_ref, ba_ref, o_ref = ins[d]
                ba = ba_ref[b]
                bsig = _sigmoid(ba)
                sp = ba + dtb_ref[...]
                gall = -jnp.exp(alog_ref[...]) * (jnp.maximum(sp, 0.0) + jnp.log(1.0 + jnp.exp(-jnp.abs(sp))))
                for hp in range(GDN_HEADS // 2):
                    col = d * GDN_HEADS + 2 * hp
                    beta = jnp.where(first, bsig[:, col:col + 1], bsig[:, col + 1:col + 2])
                    ga = GDN_A_LANE + col
                    gcol = jnp.where(first, gall[:, ga:ga + 1], gall[:, ga + 1:ga + 2])
                    sl = slice(hp * LANES, (hp + 1) * LANES)
                    work.append((b, d, hp, sl, o_ref, (q_ref[b, :, sl], k_ref[b, :, sl], v_ref[b, :, sl],
                                                       beta, gcol, s_sc[b, d, hp], d == 1)))
        done = _gdn_chunks([args for (_, _, _, _, _, args) in work])
        for (b, d, hp, sl, o_ref, _), (o, s_new) in zip(work, done):
            o_ref[b, :, sl] = o
            s_sc[b, d, hp] = s_new
        return carry

    lax.fori_loop(0, s_sc.shape[0] // GDN_BATCH_UNROLL, body, 0)

    @pl.when(c == pl.num_programs(0) - 1)
    def _():
        sfin_ref[...] = s_sc[...]


def gdn_scan(qkvn, ba, s0, alog_row, dtb_row):
    b, l, _ = qkvn.shape
    nch = l // GDN_CHUNK
    blk = (b, GDN_CHUNK, GDN_WIDTH)
    fwd = lambda col: (lambda c: (0, c, col))
    bwd = lambda col: (lambda c: (0, nch - 1 - c, col))
    st = pl.BlockSpec(s0.shape, lambda c: (0, 0, 0, 0, 0))
    return pl.pallas_call(
        _gdn_scan_kernel,
        grid=(nch,),
        in_specs=[pl.BlockSpec(blk, fwd(0)), pl.BlockSpec(blk, fwd(1)), pl.BlockSpec(blk, fwd(2)),
                  pl.BlockSpec((b, GDN_CHUNK, LANES), fwd(0)),
                  pl.BlockSpec(blk, bwd(0)), pl.BlockSpec(blk, bwd(1)), pl.BlockSpec(blk, bwd(2)),
                  pl.BlockSpec((b, GDN_CHUNK, LANES), bwd(0)),
                  pl.BlockSpec((1, LANES), lambda c: (0, 0)), pl.BlockSpec((1, LANES), lambda c: (0, 0)), st],
        out_specs=[pl.BlockSpec(blk, fwd(0)), pl.BlockSpec(blk, bwd(0)), st],
        out_shape=[jax.ShapeDtypeStruct((b, l, GDN_WIDTH), F32)] * 2 + [jax.ShapeDtypeStruct(s0.shape, F32)],
        scratch_shapes=[pltpu.VMEM(s0.shape, F32)],
        compiler_params=_params("arbitrary"), name="gdn_scan",
    )(qkvn, qkvn, qkvn, ba, qkvn, qkvn, qkvn, ba, alog_row, dtb_row, s0)


OUTPROJ_TM = 512


def _outproj_kernel(x_ref, mod_ref, da_ref, ys_ref, u_ref, dsk_ref, wg_ref, bg_ref,
                    of_ref, or_ref, z_ref, gnw_ref, avg_ref, w_ref, o_ref):
    y = ys_ref[...] + u_ref[...] * dsk_ref[...]
    zz = 0.5 * y * (1.0 + jnp.tanh(math.sqrt(2.0 / math.pi) * (y + 0.044715 * (y * y * y))))
    glu = zz * _sigmoid(jnp.dot(zz.astype(BF16), wg_ref[...], preferred_element_type=F32) + bg_ref[...])
    o = of_ref[...] + or_ref[...]
    ms = jnp.dot(o * o, avg_ref[...], precision=HP, preferred_element_type=F32)
    z = z_ref[...]
    gd = o * lax.rsqrt(ms + EPS) * gnw_ref[...] * (z * _sigmoid(z))
    acc = jnp.dot(da_ref[...], w_ref[0:DA_WIDTH, :], preferred_element_type=F32)
    acc += jnp.dot(glu.astype(BF16), w_ref[DA_WIDTH:DA_WIDTH + S5_WIDTH, :], preferred_element_type=F32)
    acc += jnp.dot(gd.astype(BF16), w_ref[DA_WIDTH + S5_WIDTH:, :], preferred_element_type=F32)
    o_ref[...] = x_ref[...] + mod_ref[0, 2:3, :] * acc


def outproj(x, mod, rows_per_mod, da, ys, u, dsk, w_glu, b_glu, o_f, o_r, z, gnw, avg, w_out):
    t, d = x.shape
    tm = min(OUTPROJ_TM, t)
    tpm = rows_per_mod // tm
    row = lambda i: (i, 0)
    const = lambda i: (0, 0)
    sw = S5_WIDTH
    return pl.pallas_call(
        _outproj_kernel,
        grid=(t // tm,),
        in_specs=[pl.BlockSpec((tm, d), row),
                  pl.BlockSpec((1, 6, d), lambda i: (i // tpm, 0, 0)),
                  pl.BlockSpec((tm, DA_WIDTH), row),
                  pl.BlockSpec((tm, sw), row), pl.BlockSpec((tm, sw), row),
                  pl.BlockSpec((1, sw), const), pl.BlockSpec((sw, sw), const), pl.BlockSpec((1, sw), const),
                  pl.BlockSpec((tm, GDN_WIDTH), row), pl.BlockSpec((tm, GDN_WIDTH), row),
                  pl.BlockSpec((tm, GDN_WIDTH), row),
                  pl.BlockSpec((1, GDN_WIDTH), const), pl.BlockSpec((GDN_WIDTH, GDN_WIDTH), const),
                  pl.BlockSpec((D_MIX, d), const)],
        out_specs=pl.BlockSpec((tm, d), row),
        out_shape=jax.ShapeDtypeStruct((t, d), F32),
        compiler_params=_params("parallel"), name="outproj",
    )(x, mod, da, ys, u, dsk, w_glu, b_glu, o_f, o_r, z, gnw, avg, w_out)


MOE_TM = 1024
MOE_EP = 4


def _pick_lowest(cur, idx, sentinel, axis):
    m = jnp.max(cur, axis=axis, keepdims=True)
    first = jnp.min(jnp.where(cur == m, idx, sentinel), axis=axis, keepdims=True)
    return idx == first


def _route(logits_t, bias):
    tm = logits_t.shape[1]
    neg = jnp.float32(-jnp.inf)
    scores = jax.nn.sigmoid(logits_t)
    biased = scores + bias
    b3 = biased.reshape(N_GROUPS, GROUP_SIZE, tm)
    eidx = lax.broadcasted_iota(jnp.int32, b3.shape, 1)
    m1 = jnp.max(b3, axis=1, keepdims=True)
    p1 = _pick_lowest(b3, eidx, GROUP_SIZE, 1)
    m2 = jnp.max(jnp.where(p1, neg, b3), axis=1, keepdims=True)
    gs = (m1 + m2).reshape(N_GROUPS, tm)
    gidx = lax.broadcasted_iota(jnp.int32, gs.shape, 0)
    gsel = jnp.zeros(gs.shape, jnp.bool_)
    cur = gs
    for _ in range(TOPK_GROUPS):
        pick = _pick_lowest(cur, gidx, N_GROUPS, 0)
        gsel = jnp.logical_or(gsel, pick)
        cur = jnp.where(pick, neg, cur)
    emask = jnp.broadcast_to(gsel.reshape(N_GROUPS, 1, tm), b3.shape)
    cur = jnp.where(emask, b3, neg).reshape(N_EXPERTS, tm)
    ridx = lax.broadcasted_iota(jnp.int32, cur.shape, 0)
    sel = jnp.zeros(cur.shape, jnp.bool_)
    for _ in range(TOP_K):
        pick = _pick_lowest(cur, ridx, N_EXPERTS, 0)
        sel = jnp.logical_or(sel, pick)
        cur = jnp.where(pick, neg, cur)
    w = jnp.where(sel, scores, 0.0)
    return w / jnp.sum(w, axis=0, keepdims=True) * ROUTED_SCALE


def _moe_kernel(x_ref, mod_ref, nw_ref, wr_ref, rb_ref, w1_ref, w3_ref, w2_ref, ws1_ref, ws3_ref, ws2_ref,
                fnw_ref, o_ref, h_sc, gate_sc, acc_sc, hid_sc, *, final_norm):
    e = pl.program_id(1)
    n_e = pl.num_programs(1)
    ep = w1_ref.shape[0]

    @pl.when(e == 0)
    def _():
        h = _rms_rows(x_ref[...]) * nw_ref[...] * (1.0 + mod_ref[0, 4:5, :]) + mod_ref[0, 3:4, :]
        hb = h.astype(BF16)
        h_sc[...] = hb
        a = jnp.dot(hb, ws1_ref[...], preferred_element_type=F32)
        b = jnp.dot(hb, ws3_ref[...], preferred_element_type=F32)
        acc_sc[...] = jnp.dot((a * _sigmoid(a) * b).astype(BF16), ws2_ref[...], preferred_element_type=F32)
        logits_t = lax.dot_general(wr_ref[...], h, NT_DIMS, precision=HP, preferred_element_type=F32)
        w = _route(logits_t, rb_ref[...])
        pad = jnp.zeros((LANES - N_EXPERTS, w.shape[1]), F32)
        gate_sc[...] = jnp.concatenate([w, pad], axis=0).T

    h = h_sc[...]
    lane = lax.broadcasted_iota(jnp.int32, (1, LANES), 1)
    for j in range(ep):
        a = jnp.dot(h, w1_ref[j], preferred_element_type=F32)
        b = jnp.dot(h, w3_ref[j], preferred_element_type=F32)
        g = jnp.sum(jnp.where(lane == e * ep + j, gate_sc[...], 0.0), axis=-1, keepdims=True)
        hid_sc[:, j * MOE_FFN:(j + 1) * MOE_FFN] = (a * _sigmoid(a) * b * g).astype(BF16)
    w2 = w2_ref[...].reshape(ep * MOE_FFN, w2_ref.shape[2])
    acc_sc[...] += jnp.dot(hid_sc[...], w2, preferred_element_type=F32)

    @pl.when(e == n_e - 1)
    def _():
        y = x_ref[...] + mod_ref[0, 5:6, :] * acc_sc[...]
        if final_norm:
            y = _rms_rows(y) * fnw_ref[...]
        o_ref[...] = y


def moe_sublayer(x, mod, rows_per_mod, norm_w, w_router_t, router_bias, weights, layer, final_w, final_norm):
    t, d = x.shape
    tm = min(MOE_TM, rows_per_mod)
    assert t % tm == 0 and rows_per_mod % tm == 0
    w1, w3, w2, ws1, ws3, ws2 = weights
    ep = MOE_EP
    assert N_EXPERTS % ep == 0
    tiles_per_mod = rows_per_mod // tm
    f = MOE_FFN
    return pl.pallas_call(
        functools.partial(_moe_kernel, final_norm=final_norm),
        grid=(t // tm, N_EXPERTS // ep),
        in_specs=[
            pl.BlockSpec((tm, d), lambda i, e: (i, 0)),
            pl.BlockSpec((1, 6, d), lambda i, e: (i // tiles_per_mod, 0, 0)),
            pl.BlockSpec((1, d), lambda i, e: (0, 0)),
            pl.BlockSpec((N_EXPERTS, d), lambda i, e: (0, 0)),
            pl.BlockSpec((N_EXPERTS, 1), lambda i, e: (0, 0)),
            pl.BlockSpec((None, ep, d, f), lambda i, e: (layer, e, 0, 0)),
            pl.BlockSpec((None, ep, d, f), lambda i, e: (layer, e, 0, 0)),
            pl.BlockSpec((None, ep, f, d), lambda i, e: (layer, e, 0, 0)),
            pl.BlockSpec((None, d, f), lambda i, e: (layer, 0, 0)),
            pl.BlockSpec((None, d, f), lambda i, e: (layer, 0, 0)),
            pl.BlockSpec((None, f, d), lambda i, e: (layer, 0, 0)),
            pl.BlockSpec((1, d), lambda i, e: (0, 0)),
        ],
        out_specs=pl.BlockSpec((tm, d), lambda i, e: (i, 0)),
        out_shape=jax.ShapeDtypeStruct((t, d), F32),
        scratch_shapes=[
            pltpu.VMEM((tm, d), BF16),
            pltpu.VMEM((tm, LANES), F32),
            pltpu.VMEM((tm, d), F32),
            pltpu.VMEM((tm, ep * MOE_FFN), BF16),
        ],
        compiler_params=_params("parallel", "arbitrary"), name="moe_sublayer",
    )(x, mod, norm_w.reshape(1, d), w_router_t, router_bias.reshape(N_EXPERTS, 1), w1, w3, w2, ws1, ws3, ws2,
      final_w.reshape(1, d))


def kernel(x, c, ctx, c_ctx, norm1_w, norm2_w, w_mod, b_mod, w_in, w_out, da_lambda, da_subln_w,
           s5_lam_re, s5_lam_im, s5_log_step, s5_b_re, s5_b_im, s5_c_re, s5_c_im, s5_d, s5_w_glu, s5_b_glu,
           gdn_conv_w, gdn_a_log, gdn_dt_bias, gdn_norm_w,
           moe_w_router, moe_router_bias, moe_w1, moe_w3, moe_w2, moe_ws1, moe_ws3, moe_ws2,
           final_norm_w):
    b, n, d = x.shape
    nc = ctx.shape[1]
    cos_t, sin_t = _rope_tables(n)
    cond = jnp.zeros((2 * b, d), F32).at[:b].set(c).at[b].set(c_ctx)
    xl = x.reshape(b * n, d)
    xc = ctx.reshape(b * nc, d)
    head_avg = jnp.kron(jnp.eye(GDN_HEADS, dtype=F32), jnp.full((GDN_DV, GDN_DV), 1.0 / GDN_DV, F32))
    s_zero = jnp.zeros((b, 2, GDN_HEADS // 2, LANES, LANES), F32)
    moe_w = tuple(w.astype(BF16) for w in (moe_w1, moe_w3, moe_w2, moe_ws1, moe_ws3, moe_ws2))
    s5_tables = _s5_tables(s5_lam_re, s5_lam_im, s5_log_step, s5_b_re, s5_b_im, s5_c_re, s5_c_im)
    for i in range(DEPTH):
        ctx_out = i < DEPTH - 1
        last = i == DEPTH - 1
        lam_init = 0.8 - 0.6 * math.exp(-0.3 * i)
        mod_all = mod_proj(cond, w_mod, b_mod, i).reshape(2 * b, 6, d)
        mod, modc = mod_all[:b], mod_all[b:b + 1]

        w_in_p = jnp.pad(w_in[i], ((0, 0), (0, IN_PAD - IN_WIDTH))).astype(BF16)
        q, k, v, u, ub, gq, z, ba = inproj(xl, mod, n, norm1_w[i], w_in_p, cos_t, sin_t, rope=True)
        qc, kc, vc, uc, ubc, gqc, zc, bac = inproj(xc, modc, b * nc, norm1_w[i], w_in_p, cos_t, sin_t, rope=False)

        lq1, lk1, lq2, lk2 = da_lambda[i].astype(F32)
        lam = jnp.exp(jnp.sum(lq1 * lk1)) - jnp.exp(jnp.sum(lq2 * lk2)) + lam_init
        lam_row = jnp.full((1, LANES), lam, F32)
        da = diff_attention(q, [kc, k], [vc, v], (nc, n), n, lam_row, da_subln_w[i], lam_init)

        ysc, ysl = s5_scan(ubc, ub, b, s5_tables, i)

        a_lanes = slice(GDN_A_LANE, 2 * GDN_A_LANE)
        alog_row = jnp.zeros((1, LANES), F32).at[0, a_lanes].set(gdn_a_log[i].astype(F32).reshape(-1))
        dtb_row = jnp.zeros((1, LANES), F32).at[0, a_lanes].set(gdn_dt_bias[i].astype(F32).reshape(-1))
        gn = gdn_prep(gq, gdn_conv_w[i], n)
        gnc = gdn_prep(gqc, gdn_conv_w[i], nc)
        ofc, orc, s_ctx = gdn_scan(gnc.reshape(b, nc, GDN_QKV_W), bac.reshape(b, nc, LANES), s_zero, alog_row, dtb_row)
        of, orv, _ = gdn_scan(gn.reshape(b, n, GDN_QKV_W), ba.reshape(b, n, LANES), s_ctx, alog_row, dtb_row)

        w_out_b = w_out[i].astype(BF16)
        dsk = s5_d[i].astype(F32).reshape(1, S5_WIDTH)
        wg = s5_w_glu[i].astype(BF16)
        bg = s5_b_glu[i].astype(F32).reshape(1, S5_WIDTH)
        gnw = jnp.tile(gdn_norm_w[i].astype(F32), GDN_HEADS).reshape(1, GDN_WIDTH)
        xl = outproj(xl, mod, n, da, ysl, u, dsk, wg, bg,
                     of.reshape(b * n, GDN_WIDTH), orv.reshape(b * n, GDN_WIDTH), z, gnw, head_avg, w_out_b)
        if ctx_out:
            dac = diff_attention(qc, [kc], [vc], (nc,), nc, lam_row, da_subln_w[i], lam_init)
            xc = outproj(xc, modc, b * nc, dac, ysc, uc, dsk, wg, bg,
                         ofc.reshape(b * nc, GDN_WIDTH), orc.reshape(b * nc, GDN_WIDTH), zc, gnw, head_avg, w_out_b)

        wr_t = moe_w_router[i].T
        xl = moe_sublayer(xl, mod, n, norm2_w[i], wr_t, moe_router_bias[i], moe_w, i, final_norm_w, last)
        if ctx_out:
            xc = moe_sublayer(xc, modc, b * nc, norm2_w[i], wr_t, moe_router_bias[i], moe_w, i, final_norm_w, False)
    return xl.reshape(b, n, d)
```

```python
import functools
import math

import jax
import jax.numpy as jnp
from jax import lax
from jax.experimental import pallas as pl
from jax.experimental.pallas import tpu as pltpu

F32 = jnp.float32
BF16 = jnp.bfloat16

D_MODEL = 1024
DEPTH = 2
GRID_W = 64
EPS = 1e-6

DA_HEADS = 4
DA_HEAD_DIM = D_MODEL // 16
DA_V_DIM = 2 * DA_HEAD_DIM
DA_WIDTH = DA_HEADS * DA_V_DIM
ROPE_THETA = 10000.0

S5_WIDTH = D_MODEL // 4
S5_GROUP = 16
S5_GROUPS = S5_WIDTH // S5_GROUP
S5_STATE = 64

GDN_HEADS = 4
GDN_DK = D_MODEL // 16
GDN_DV = D_MODEL // 16
GDN_WIDTH = GDN_HEADS * GDN_DV
GDN_CONV = 5
GDN_CHUNK = 64

D_MIX = DA_WIDTH + S5_WIDTH + GDN_WIDTH
DA_QK_W = 2 * DA_HEADS * DA_HEAD_DIM
GDN_QKV_W = 2 * GDN_HEADS * GDN_DK + GDN_HEADS * GDN_DV
IN_SIZES = (DA_QK_W, DA_QK_W, DA_WIDTH, S5_WIDTH, GDN_QKV_W, GDN_WIDTH, 2 * GDN_HEADS, 2 * GDN_HEADS)
IN_WIDTH = sum(IN_SIZES)

N_EXPERTS = 64
TOP_K = 8
N_GROUPS = 8
GROUP_SIZE = N_EXPERTS // N_GROUPS
TOPK_GROUPS = 4
MOE_FFN = D_MODEL // 4
ROUTED_SCALE = 2.5

LANES = 128
SUBLANES = 8
V7X_VMEM_BYTES = 64 * 1024 * 1024
VMEM_LIMIT_BYTES = V7X_VMEM_BYTES * 7 // 8

GDN_A_LANE = 2 * GDN_HEADS

HP = lax.Precision.HIGHEST
NT_DIMS = (((1,), (1,)), ((), ()))


def _params(*sem):
    return pltpu.CompilerParams(dimension_semantics=sem, vmem_limit_bytes=VMEM_LIMIT_BYTES)


def _sigmoid(x):
    return 0.5 * (1.0 + jnp.tanh(0.5 * x))


def _rms_rows(x):
    return x * lax.rsqrt(jnp.mean(x * x, axis=-1, keepdims=True) + EPS)


MOD_TN = 768


def _mod_kernel(c_ref, w_ref, b_ref, o_ref):
    c = c_ref[...]
    act = c * _sigmoid(c)
    o_ref[...] = jnp.dot(act, w_ref[...], precision=HP, preferred_element_type=F32) + b_ref[...]


def mod_proj(cond, w_mod, b_mod, layer):
    r, d = cond.shape
    depth, _, n = w_mod.shape
    return pl.pallas_call(
        _mod_kernel,
        grid=(n // MOD_TN,),
        in_specs=[pl.BlockSpec((r, d), lambda j: (0, 0)),
                  pl.BlockSpec((None, d, MOD_TN), lambda j: (layer, 0, j)),
                  pl.BlockSpec((None, 1, MOD_TN), lambda j: (layer, 0, j))],
        out_specs=pl.BlockSpec((r, MOD_TN), lambda j: (0, j)),
        out_shape=jax.ShapeDtypeStruct((r, n), F32),
        compiler_params=_params("parallel"), name="mod_proj",
    )(cond, w_mod, b_mod.reshape(depth, 1, n))


IN_PAD = 2944
INPROJ_TM = 512
O_Q, O_K, O_V, O_U, O_G, O_Z, O_BA = 0, 512, 1024, 1536, 1792, 2560, 2816


def _rope_apply(x, cos, sin):
    lane = lax.broadcasted_iota(jnp.int32, x.shape, 1)
    up = pltpu.roll(x, LANES - 16, 1)
    dn = pltpu.roll(x, 16, 1)
    partner = jnp.where((lane & 31) < 16, up, dn)
    return x * cos + partner * sin


def _inproj_kernel(x_ref, mod_ref, nw_ref, w_ref, cos_ref, sin_ref,
                   q_ref, k_ref, v_ref, u_ref, ub_ref, g_ref, z_ref, ba_ref, *, rope):
    h = (_rms_rows(x_ref[...]) * nw_ref[...] * (1.0 + mod_ref[0, 1:2, :]) + mod_ref[0, 0:1, :]).astype(BF16)

    def proj(lo, hi):
        return jnp.dot(h, w_ref[:, lo:hi], preferred_element_type=F32)

    scale = DA_HEAD_DIM ** -0.5 * math.log2(math.e)
    q_all = proj(O_Q, O_K)
    k_all = proj(O_K, O_V)
    for hd in range(DA_HEADS):
        lo = hd * LANES
        qs = q_all[:, lo:lo + LANES]
        ks = k_all[:, lo:lo + LANES]
        if rope:
            qs = _rope_apply(qs, cos_ref[...], sin_ref[...])
            ks = _rope_apply(ks, cos_ref[...], sin_ref[...])
        q_ref[:, lo:lo + LANES] = (qs * scale).astype(BF16)
        k_ref[:, lo:lo + LANES] = ks.astype(BF16)
    v_ref[...] = proj(O_V, O_U).astype(BF16)
    u = proj(O_U, O_G)
    u_ref[...] = u
    ub_ref[...] = u.astype(BF16)
    g_ref[...] = proj(O_G, O_Z)
    z_ref[...] = proj(O_Z, O_BA)
    ba_ref[...] = proj(O_BA, IN_PAD)


def inproj(x, mod, rows_per_mod, norm_w, w_pad, cos_t, sin_t, rope):
    t, d = x.shape
    tm = min(INPROJ_TM, t)
    tpm = rows_per_mod // tm
    npos = cos_t.shape[0] // tm
    row = lambda i: (i, 0)
    widths = (DA_QK_W, DA_QK_W, DA_WIDTH, S5_WIDTH, S5_WIDTH, GDN_QKV_W, GDN_WIDTH, LANES)
    dtypes = (BF16, BF16, BF16, F32, BF16, F32, F32, F32)
    return pl.pallas_call(
        functools.partial(_inproj_kernel, rope=rope),
        grid=(t // tm,),
        in_specs=[pl.BlockSpec((tm, d), row),
                  pl.BlockSpec((1, 6, d), lambda i: (i // tpm, 0, 0)),
                  pl.BlockSpec((1, d), lambda i: (0, 0)),
                  pl.BlockSpec((d, IN_PAD), lambda i: (0, 0)),
                  pl.BlockSpec((tm, LANES), lambda i: (i % npos, 0)),
                  pl.BlockSpec((tm, LANES), lambda i: (i % npos, 0))],
        out_specs=[pl.BlockSpec((tm, w), row) for w in widths],
        out_shape=[jax.ShapeDtypeStruct((t, w), dt) for w, dt in zip(widths, dtypes)],
        compiler_params=_params("parallel"), name="inproj",
    )(x, mod, norm_w.reshape(1, d), w_pad, cos_t, sin_t)


def _rope_tables(n):
    nf = DA_HEAD_DIM // 4
    t = jnp.arange(n, dtype=jnp.int32)
    inv = ROPE_THETA ** (-jnp.arange(nf, dtype=F32) / nf)
    ang_r = (t // GRID_W).astype(F32)[:, None] * inv
    ang_c = (t % GRID_W).astype(F32)[:, None] * inv
    cos64 = jnp.concatenate([jnp.cos(ang_r), jnp.cos(ang_r), jnp.cos(ang_c), jnp.cos(ang_c)], axis=-1)
    sin64 = jnp.concatenate([-jnp.sin(ang_r), jnp.sin(ang_r), -jnp.sin(ang_c), jnp.sin(ang_c)], axis=-1)
    return jnp.tile(cos64, (1, 2)), jnp.tile(sin64, (1, 2))


ATTN_TQ = 512
ATTN_KC = 512


def _attn_kernel(*refs, n_kv, kv_rows, chunks, lam_init):
    q_ref = refs[0]
    k_refs = refs[1:1 + n_kv]
    v_refs = refs[1 + n_kv:1 + 2 * n_kv]
    lam_ref, w_ref, o_ref, v1_sc = refs[1 + 2 * n_kv:]

    @pl.when(pl.program_id(2) == 0)
    def _():
        off = 0
        for ki, rows in enumerate(kv_rows):
            v1_sc[off:off + rows, :DA_V_DIM] = v_refs[ki][...]
            v1_sc[off:off + rows, DA_V_DIM:] = jnp.ones((rows, DA_V_DIM), BF16)
            off += rows

    q = q_ref[...]
    tq = q.shape[0]
    lane = lax.broadcasted_iota(jnp.int32, q.shape, 1)
    zero = jnp.zeros_like(q)
    qq = jnp.concatenate([jnp.where(lane < DA_HEAD_DIM, q, zero), jnp.where(lane >= DA_HEAD_DIM, q, zero)], axis=0)
    m = jnp.full((2 * tq, 1), -jnp.inf, F32)
    acc = jnp.zeros((2 * tq, 2 * DA_V_DIM), F32)
    def scores(chunk):
        ki, start, _, size = chunk
        return lax.dot_general(qq, k_refs[ki][start:start + size, :], NT_DIMS, preferred_element_type=F32)

    s_next = scores(chunks[0])
    for ci, (ki, start, off, size) in enumerate(chunks):
        s = s_next
        if ci + 1 < len(chunks):
            s_next = scores(chunks[ci + 1])
        m_new = jnp.maximum(m, jnp.max(s, axis=-1, keepdims=True))
        p = jnp.exp2(s - m_new).astype(BF16)
        acc = jnp.exp2(m - m_new) * acc + jnp.dot(p, v1_sc[off:off + size, :], preferred_element_type=F32)
        m = m_new
    o = acc[:, :DA_V_DIM] / acc[:, DA_V_DIM:]
    od = o[:tq] - lam_ref[...] * o[tq:]
    o_ref[...] = (_rms_rows(od) * w_ref[...] * (1.0 - lam_init)).astype(o_ref.dtype)


def diff_attention(q, ks, vs, kv_rows, q_rows, lam_row, subln_w, lam_init):
    t = q.shape[0]
    b = t // q_rows
    tq = min(ATTN_TQ, q_rows)
    nq = q_rows // tq
    chunks, off = [], 0
    for ki, rows in enumerate(kv_rows):
        kc = min(ATTN_KC, rows)
        chunks += [(ki, s, off + s, kc) for s in range(0, rows, kc)]
        off += rows
    qmap = lambda bi, h, qi: (bi * nq + qi, h)
    kvmap = lambda bi, h, qi: (bi, h)
    const = lambda bi, h, qi: (0, 0)
    return pl.pallas_call(
        functools.partial(_attn_kernel, n_kv=len(ks), kv_rows=tuple(kv_rows), chunks=tuple(chunks),
                          lam_init=lam_init),
        grid=(b, DA_HEADS, nq),
        in_specs=([pl.BlockSpec((tq, LANES), qmap)]
                  + [pl.BlockSpec((rows, LANES), kvmap) for rows in kv_rows] * 2
                  + [pl.BlockSpec((1, LANES), const)] * 2),
        out_specs=pl.BlockSpec((tq, LANES), qmap),
        out_shape=jax.ShapeDtypeStruct((t, DA_WIDTH), BF16),
        scratch_shapes=[pltpu.VMEM((off, 2 * DA_V_DIM), BF16)],
        compiler_params=_params("parallel", "parallel", "arbitrary"), name="diff_attention",
    )(q, *ks, *vs, lam_row, subln_w.reshape(1, LANES))


S5_LC = 64
S5_CW = S5_LC * S5_GROUP
S5_SW = 2 * S5_STATE


def _s5_discretize(lam_re, lam_im, log_step, b_re, b_im):
    lr, li = lam_re.astype(F32), lam_im.astype(F32)
    step = jnp.exp(log_step.astype(F32))[..., None]
    mag = jnp.exp(lr * step)
    ab_re, ab_im = mag * jnp.cos(li * step), mag * jnp.sin(li * step)
    den = lr * lr + li * li
    nr, ni = ab_re - 1.0, ab_im
    f_re = (nr * lr + ni * li) / den
    f_im = (ni * lr - nr * li) / den
    br, bi = b_re.astype(F32), b_im.astype(F32)
    bb_re = f_re[..., None] * br - f_im[..., None] * bi
    bb_im = f_re[..., None] * bi + f_im[..., None] * br
    return bb_re, bb_im


def _s5_tables(lam_re, lam_im, log_step, b_re, b_im, c_re, c_im):
    lc, g, nl = S5_LC, S5_GROUPS, lam_re.shape[0] * 2
    merge = lambda x: x.astype(F32).reshape((nl,) + x.shape[2:])
    lam_re, lam_im, log_step, b_re, b_im, cr, ci = map(merge, (lam_re, lam_im, log_step, b_re, b_im, c_re, c_im))
    rev = jnp.arange(nl) % 2 == 1

    def flip_rev(x, axis):
        return jnp.where(rev.reshape((1, nl) + (1,) * (x.ndim - 2)), jnp.flip(x, axis), x)

    step = jnp.exp(log_step)[..., None]
    bb_re, bb_im = _s5_discretize(lam_re, lam_im, log_step, b_re, b_im)
    tau = jnp.arange(lc + 1, dtype=F32).reshape(lc + 1, 1, 1, 1)
    mag = jnp.exp(tau * (lam_re * step))
    ang = tau * (lam_im * step)
    pr, pi = mag * jnp.cos(ang), mag * jnp.sin(ang)
    abr = pr[..., None] * bb_re - pi[..., None] * bb_im
    abi = pr[..., None] * bb_im + pi[..., None] * bb_re
    kern = (jnp.einsum('lgkp,tlgph->tlgkh', cr, abr[:lc], precision=HP)
            - jnp.einsum('lgkp,tlgph->tlgkh', ci, abi[:lc], precision=HP))
    ab = jnp.concatenate([abr[:lc], abi[:lc]], axis=3)
    bmat = flip_rev(ab[::-1], 0).transpose(1, 2, 4, 0, 3).reshape(nl, g, S5_CW, S5_SW)
    pro, pio = pr[1:, :, :, None, :], pi[1:, :, :, None, :]
    ca = jnp.concatenate([cr * pro - ci * pio, -(cr * pio + ci * pro)], axis=4)
    cmat = flip_rev(ca, 0).transpose(1, 2, 4, 3, 0).reshape(nl, g, S5_SW, S5_CW)
    strip = jnp.concatenate([jnp.zeros((lc - 1,) + kern.shape[1:], F32), kern], axis=0)
    strip = jnp.pad(flip_rev(strip, 0), ((0, 1),) + ((0, 0),) * 4).transpose(1, 2, 4, 3, 0)
    alr, ali = pr[lc], pi[lc]
    k1 = jnp.repeat(jnp.concatenate([alr, alr], axis=-1), SUBLANES, axis=1)
    k2 = jnp.repeat(jnp.concatenate([-ali, ali], axis=-1), SUBLANES, axis=1)
    return bmat.astype(BF16), cmat.astype(BF16), _s5_toeplitz(strip), k1, k2


def _s5_toeplitz_kernel(strip_ref, m_ref):
    lc = S5_LC
    lane = lax.broadcasted_iota(jnp.int32, (lc, 2 * lc), 1)
    for h in range(S5_GROUP):
        for kp in range(S5_GROUP // 2):
            halves = []
            for half in range(2):
                row = jnp.broadcast_to(strip_ref[0, 0, h, 2 * kp + half:2 * kp + half + 1, :], (lc, 2 * lc))
                base = lc + 1 if half == 0 else 1
                halves.append(pltpu.roll(row, base, 1, stride=1, stride_axis=0))
            tile = jnp.where(lane < lc, halves[0], halves[1])
            m_ref[0, 0, h * lc:(h + 1) * lc, kp * 2 * lc:(kp + 1) * 2 * lc] = tile.astype(m_ref.dtype)


def _s5_toeplitz(strips):
    nd, g = strips.shape[:2]
    return pl.pallas_call(
        _s5_toeplitz_kernel,
        grid=(nd, g),
        in_specs=[pl.BlockSpec((1, 1) + strips.shape[2:], lambda d, gi: (d, gi, 0, 0, 0))],
        out_specs=pl.BlockSpec((1, 1, S5_CW, S5_CW), lambda d, gi: (d, gi, 0, 0)),
        out_shape=jax.ShapeDtypeStruct((nd, g, S5_CW, S5_CW), BF16),
        compiler_params=_params("parallel", "parallel"), name="s5_toeplitz",
    )(strips)


def _s5_local_kernel(uc_ref, ul_ref, bm_ref, s_ref):
    u = jnp.concatenate([uc_ref[0], ul_ref[0]], axis=0)
    s = jnp.dot(u, bm_ref[0, 0], preferred_element_type=F32)
    s_ref[0] = s.reshape(s_ref.shape[1:])


def _s5_carry_kernel(s_ref, k1_ref, k2_ref, xin_ref, *, n_ctx, n_chunks):
    d = pl.program_id(0)
    k1, k2 = k1_ref[0], k2_ref[0]

    def body(k, x):
        rev = jnp.where(k < n_ctx, n_ctx - 1 - k, n_chunks + n_ctx - 1 - k)
        c = jnp.where(d == 0, k, rev)
        xin_ref[0, c] = x
        return k1 * x + k2 * pltpu.roll(x, S5_STATE, 1) + s_ref[0, c]

    lax.fori_loop(0, n_chunks, body, jnp.zeros(k1.shape, F32))


def _s5_out_kernel(uc_ref, ul_ref, m_ref, xin_ref, cm_ref, yc_ref, yl_ref):
    u = jnp.concatenate([uc_ref[0], ul_ref[0]], axis=0)
    acc = jnp.dot(u, m_ref[0, 0], preferred_element_type=F32)
    acc += jnp.dot(u, m_ref[1, 0], preferred_element_type=F32)
    for d in range(2):
        xin = xin_ref[d].reshape(u.shape[0], S5_SW).astype(BF16)
        acc += jnp.dot(xin, cm_ref[d, 0], preferred_element_type=F32)
    rc = yc_ref.shape[1]
    yc_ref[0] = acc[:rc].astype(yc_ref.dtype)
    yl_ref[0] = acc[rc:].astype(yl_ref.dtype)


def s5_scan(u_ctx, u_lat, b, tables, layer):
    bmat, cmat, mmat, k1, k2 = tables
    g, lc = S5_GROUPS, S5_LC
    n_ctx, n_lat = u_ctx.shape[0] // b, u_lat.shape[0] // b
    assert b == SUBLANES and n_lat % lc == 0 and n_ctx % lc == 0
    rc, rl = n_ctx // lc * b, n_lat // lc * b
    nch = (n_ctx + n_lat) // lc

    def to_groups(x, rows):
        x = lax.optimization_barrier(jnp.swapaxes(x.reshape(b, rows // lc, lc, S5_WIDTH), 2, 3))
        return x.reshape(b, rows // lc, g, S5_CW).transpose(2, 1, 0, 3).reshape(g, rows // lc * b, S5_CW)

    def from_groups(y, rows):
        y = lax.optimization_barrier(y.reshape(g, rows // lc, b, S5_CW).transpose(2, 1, 0, 3))
        return jnp.swapaxes(y.reshape(b, rows // lc, S5_WIDTH, lc), 2, 3).reshape(b * rows, S5_WIDTH)

    ugc, ugl = to_groups(u_ctx, n_ctx), to_groups(u_lat, n_lat)
    s = pl.pallas_call(
        _s5_local_kernel,
        grid=(2, g),
        in_specs=[pl.BlockSpec((1, rc, S5_CW), lambda d, gi: (gi, 0, 0)),
                  pl.BlockSpec((1, rl, S5_CW), lambda d, gi: (gi, 0, 0)),
                  pl.BlockSpec((1, 1, S5_CW, S5_SW), lambda d, gi: (2 * layer + d, gi, 0, 0))],
        out_specs=pl.BlockSpec((1, nch, b, S5_SW), lambda d, gi: (d, 0, gi, 0)),
        out_shape=jax.ShapeDtypeStruct((2, nch, g * b, S5_SW), F32),
        compiler_params=_params("arbitrary", "arbitrary"), name="s5_local",
    )(ugc, ugl, bmat)
    xin = pl.pallas_call(
        functools.partial(_s5_carry_kernel, n_ctx=n_ctx // lc, n_chunks=nch),
        grid=(2,),
        in_specs=[pl.BlockSpec((1, nch, g * b, S5_SW), lambda d: (d, 0, 0, 0)),
                  pl.BlockSpec((1, g * b, S5_SW), lambda d: (2 * layer + d, 0, 0)),
                  pl.BlockSpec((1, g * b, S5_SW), lambda d: (2 * layer + d, 0, 0))],
        out_specs=pl.BlockSpec((1, nch, g * b, S5_SW), lambda d: (d, 0, 0, 0)),
        out_shape=jax.ShapeDtypeStruct((2, nch, g * b, S5_SW), F32),
        compiler_params=_params("arbitrary"), name="s5_carry",
    )(s, k1, k2)
    yc, yl = pl.pallas_call(
        _s5_out_kernel,
        grid=(g,),
        in_specs=[pl.BlockSpec((1, rc, S5_CW), lambda gi: (gi, 0, 0)),
                  pl.BlockSpec((1, rl, S5_CW), lambda gi: (gi, 0, 0)),
                  pl.BlockSpec((2, 1, S5_CW, S5_CW), lambda gi: (layer, gi, 0, 0)),
                  pl.BlockSpec((2, nch, b, S5_SW), lambda gi: (0, 0, gi, 0)),
                  pl.BlockSpec((2, 1, S5_SW, S5_CW), lambda gi: (layer, gi, 0, 0))],
        out_specs=[pl.BlockSpec((1, rc, S5_CW), lambda gi: (gi, 0, 0)),
                   pl.BlockSpec((1, rl, S5_CW), lambda gi: (gi, 0, 0))],
        out_shape=[jax.ShapeDtypeStruct((g, rc, S5_CW), BF16), jax.ShapeDtypeStruct((g, rl, S5_CW), BF16)],
        compiler_params=_params("arbitrary"), name="s5_out",
    )(ugc, ugl, mmat, xin, cmat)
    return from_groups(yc, n_ctx), from_groups(yl, n_lat)


def _gdn_prep_kernel(x_ref, w_ref, o_ref):
    j = pl.program_id(1)
    x = x_ref[...]
    n = x.shape[0]
    row = lax.broadcasted_iota(jnp.int32, x.shape, 0)
    half = GDN_CONV // 2
    acc = x * w_ref[half:half + 1, :]
    for sh in range(1, half + 1):
        acc += jnp.where(row >= sh, pltpu.roll(x, sh, 0), 0.0) * w_ref[half - sh:half - sh + 1, :]
        acc += jnp.where(row < n - sh, pltpu.roll(x, n - sh, 0), 0.0) * w_ref[half + sh:half + sh + 1, :]
    a = acc * _sigmoid(acc)
    lane = lax.broadcasted_iota(jnp.int32, x.shape, 1)
    lo = lane < GDN_DK
    sq = a * a
    s_lo = jnp.sum(jnp.where(lo, sq, 0.0), axis=-1, keepdims=True)
    s_hi = jnp.sum(jnp.where(lo, 0.0, sq), axis=-1, keepdims=True)
    nrm = a * lax.rsqrt(jnp.where(lo, s_lo, s_hi) + EPS)
    q_blocks = GDN_HEADS * GDN_DK // LANES
    nrm = nrm * jnp.where(j < q_blocks, GDN_DK ** -0.5, 1.0)
    o_ref[...] = jnp.where(j < 2 * q_blocks, nrm, a)


def gdn_prep(qkv, conv_w, seg):
    t, w = qkv.shape
    return pl.pallas_call(
        _gdn_prep_kernel,
        grid=(t // seg, w // LANES),
        in_specs=[pl.BlockSpec((seg, LANES), lambda s, j: (s, j)),
                  pl.BlockSpec((GDN_CONV, LANES), lambda s, j: (0, j))],
        out_specs=pl.BlockSpec((seg, LANES), lambda s, j: (s, j)),
        out_shape=jax.ShapeDtypeStruct((t, w), F32),
        compiler_params=_params("parallel", "arbitrary"), name="gdn_prep",
    )(qkv, conv_w)


GDN_SHIFT = GDN_CHUNK.bit_length() - 1
assert 1 << GDN_SHIFT == GDN_CHUNK == GDN_DK == GDN_DV


def _bd(x):
    x2 = jnp.concatenate([x, x], axis=0)
    r = lax.broadcasted_iota(jnp.int32, x2.shape, 0)
    l = lax.broadcasted_iota(jnp.int32, x2.shape, 1)
    return jnp.where((r >> GDN_SHIFT) == (l >> GDN_SHIFT), x2, jnp.zeros_like(x2))


def _mm(a, b):
    return jnp.dot(a.astype(BF16), b.astype(BF16), preferred_element_type=F32)


def _split3(x):
    hi = x.astype(BF16)
    r1 = x - hi.astype(F32)
    mid = r1.astype(BF16)
    lo = (r1 - mid.astype(F32)).astype(BF16)
    return hi, mid, lo


def _dot01(m01, x):
    hi, mid, lo = _split3(x)
    dot = lambda p: jnp.dot(m01, p, preferred_element_type=F32)
    return (dot(lo) + dot(mid)) + dot(hi)


def _gdn_chunks(insts):
    c = GDN_CHUNK
    n = len(insts)
    every = range(n)
    q, k, v, beta, gcol, s_bd, rev = (list(t) for t in zip(*insts))
    i = lax.broadcasted_iota(jnp.int32, (c, LANES), 0)
    j = lax.broadcasted_iota(jnp.int32, (c, LANES), 1) & (c - 1)
    ti = lax.broadcasted_iota(jnp.int32, (c, c), 0)
    tj = lax.broadcasted_iota(jnp.int32, (c, c), 1)
    causal = [i <= j if r else i >= j for r in rev]
    strict = [i < j if r else i > j for r in rev]
    upto = [i >= j if r else i <= j for r in rev]
    lmat = [(tj >= ti if r else tj <= ti).astype(BF16) for r in rev]
    ones = jnp.ones((c, c), BF16)
    eye = jnp.where(i == j, 1.0, 0.0)
    lg = [_dot01(lmat[t], gcol[t]) for t in every]
    rg = [_dot01(ones, jnp.where(upto[t], gcol[t], 0.0)) for t in every]
    decay = [jnp.where(causal[t], jnp.exp(jnp.where(causal[t], lg[t] - rg[t], 0.0)), 0.0) for t in every]
    kb = [k[t] * beta[t] for t in every]
    k_bd = [_bd(k[t].astype(BF16)) for t in every]
    a = [jnp.where(strict[t], lax.dot_general(kb[t].astype(BF16), k_bd[t], NT_DIMS, preferred_element_type=F32)
                   * decay[t], 0.0) for t in every]
    qk = [lax.dot_general(q[t].astype(BF16), k_bd[t], NT_DIMS, preferred_element_type=F32) * decay[t] for t in every]
    eg = [jnp.exp(lg[t]) for t in every]
    base = 3
    same = [(i >> sh) == (j >> sh) for sh in range(base, GDN_SHIFT)]
    x = [jnp.where(same[0], a[t], 0.0) for t in every]
    p = [eye - x[t] for t in every]
    for _ in range(base - 1):
        x = [_mm(x[t], _bd(x[t])) for t in every]
        p = [p[t] + _mm(p[t], _bd(x[t])) for t in every]
    for lvl, inner in enumerate(same):
        outer_same = same[lvl + 1] if lvl + 1 < len(same) else True
        join = jnp.logical_and(outer_same, jnp.logical_not(inner))
        tl = [_mm(p[t], _bd(jnp.where(join, a[t], 0.0))) for t in every]
        p = [p[t] - _mm(tl[t], _bd(p[t])) for t in every]
    u = [_mm(p[t], _bd(v[t] * beta[t])) for t in every]
    w = [_mm(p[t], _bd(kb[t] * eg[t])) for t in every]
    v_new = [u[t] - _mm(w[t], s_bd[t]) for t in every]
    o_state = [_mm(q[t] * eg[t], s_bd[t]) for t in every]
    o = [o_state[t] + _mm(qk[t], _bd(v_new[t])) for t in every]
    g_last = [lg[t][0:1, :] if rev[t] else lg[t][c - 1:c, :] for t in every]
    k_dec = [k[t] * jnp.exp(g_last[t] - lg[t]) for t in every]
    upd = [jnp.dot(k_dec[t].T.astype(BF16), v_new[t].astype(BF16), preferred_element_type=F32) for t in every]
    r2 = lax.broadcasted_iota(jnp.int32, (LANES, LANES), 0)
    l2 = lax.broadcasted_iota(jnp.int32, (LANES, LANES), 1)
    diag = (r2 >> GDN_SHIFT) == (l2 >> GDN_SHIFT)
    s_new = [s_bd[t] * jnp.exp(g_last[t]) + jnp.where(diag, upd[t], 0.0) for t in every]
    return list(zip(o, s_new))


GDN_BATCH_UNROLL = 8


def _gdn_scan_kernel(qf, kf, vf, baf, qr, kr, vr, bar, alog_ref, dtb_ref, s0_ref,
                     of_ref, or_ref, sfin_ref, s_sc):
    c = pl.program_id(0)

    @pl.when(c == 0)
    def _():
        s_sc[...] = s0_ref[...]

    lane = lax.broadcasted_iota(jnp.int32, (GDN_CHUNK, LANES), 1)
    first = lane < GDN_DK
    ins = ((qf, kf, vf, baf, of_ref), (qr, kr, vr, bar, or_ref))

    def body(it, carry):
        work = []
        for bb in range(GDN_BATCH_UNROLL):
            b = it * GDN_BATCH_UNROLL + bb
            for d in range(2):
                q_ref, k_ref, v_ref, ba_ref, o_ref = ins[d]
                ba = ba_ref[b]
                bsig = _sigmoid(ba)
                sp = ba + dtb_ref[...]
                gall = -jnp.exp(alog_ref[...]) * (jnp.maximum(sp, 0.0) + jnp.log(1.0 + jnp.exp(-jnp.abs(sp))))
                for hp in range(GDN_HEADS // 2):
                    col = d * GDN_HEADS + 2 * hp
                    beta = jnp.where(first, bsig[:, col:col + 1], bsig[:, col + 1:col + 2])
                    ga = GDN_A_LANE + col
                    gcol = jnp.where(first, gall[:, ga:ga + 1], gall[:, ga + 1:ga + 2])
                    sl = slice(hp * LANES, (hp + 1) * LANES)
                    work.append((b, d, hp, sl, o_ref, (q_ref[b, :, sl], k_ref[b, :, sl], v_ref[b, :, sl],
                                                       beta, gcol, s_sc[b, d, hp], d == 1)))
        done = _gdn_chunks([args for (_, _, _, _, _, args) in work])
        for (b, d, hp, sl, o_ref, _), (o, s_new) in zip(work, done):
            o_ref[b, :, sl] = o
            s_sc[b, d, hp] = s_new
        return carry

    lax.fori_loop(0, s_sc.shape[0] // GDN_BATCH_UNROLL, body, 0)

    @pl.when(c == pl.num_programs(0) - 1)
    def _():
        sfin_ref[...] = s_sc[...]


def gdn_scan(qkvn, ba, s0, alog_row, dtb_row):
    b, l, _ = qkvn.shape
    nch = l // GDN_CHUNK
    blk = (b, GDN_CHUNK, GDN_WIDTH)
    fwd = lambda col: (lambda c: (0, c, col))
    bwd = lambda col: (lambda c: (0, nch - 1 - c, col))
    st = pl.BlockSpec(s0.shape, lambda c: (0, 0, 0, 0, 0))
    return pl.pallas_call(
        _gdn_scan_kernel,
        grid=(nch,),
        in_specs=[pl.BlockSpec(blk, fwd(0)), pl.BlockSpec(blk, fwd(1)), pl.BlockSpec(blk, fwd(2)),
                  pl.BlockSpec((b, GDN_CHUNK, LANES), fwd(0)),
                  pl.BlockSpec(blk, bwd(0)), pl.BlockSpec(blk, bwd(1)), pl.BlockSpec(blk, bwd(2)),
                  pl.BlockSpec((b, GDN_CHUNK, LANES), bwd(0)),
                  pl.BlockSpec((1, LANES), lambda c: (0, 0)), pl.BlockSpec((1, LANES), lambda c: (0, 0)), st],
        out_specs=[pl.BlockSpec(blk, fwd(0)), pl.BlockSpec(blk, bwd(0)), st],
        out_shape=[jax.ShapeDtypeStruct((b, l, GDN_WIDTH), F32)] * 2 + [jax.ShapeDtypeStruct(s0.shape, F32)],
        scratch_shapes=[pltpu.VMEM(s0.shape, F32)],
        compiler_params=_params("arbitrary"), name="gdn_scan",
    )(qkvn, qkvn, qkvn, ba, qkvn, qkvn, qkvn, ba, alog_row, dtb_row, s0)


OUTPROJ_TM = 512


def _outproj_kernel(x_ref, mod_ref, da_ref, ys_ref, u_ref, dsk_ref, wg_ref, bg_ref,
                    of_ref, or_ref, z_ref, gnw_ref, avg_ref, w_ref, o_ref):
    y = ys_ref[...] + u_ref[...] * dsk_ref[...]
    zz = 0.5 * y * (1.0 + jnp.tanh(math.sqrt(2.0 / math.pi) * (y + 0.044715 * (y * y * y))))
    glu = zz * _sigmoid(jnp.dot(zz.astype(BF16), wg_ref[...], preferred_element_type=F32) + bg_ref[...])
    o = of_ref[...] + or_ref[...]
    avg = avg_ref[...].astype(BF16)
    hi, mid, lo = _split3(o * o)
    mean_of = lambda p: jnp.dot(p, avg, preferred_element_type=F32)
    ms = (mean_of(lo) + mean_of(mid)) + mean_of(hi)
    z = z_ref[...]
    gd = o * lax.rsqrt(ms + EPS) * gnw_ref[...] * (z * _sigmoid(z))
    acc = jnp.dot(da_ref[...], w_ref[0:DA_WIDTH, :], preferred_element_type=F32)
    acc += jnp.dot(glu.astype(BF16), w_ref[DA_WIDTH:DA_WIDTH + S5_WIDTH, :], preferred_element_type=F32)
    acc += jnp.dot(gd.astype(BF16), w_ref[DA_WIDTH + S5_WIDTH:, :], preferred_element_type=F32)
    o_ref[...] = x_ref[...] + mod_ref[0, 2:3, :] * acc


def outproj(x, mod, rows_per_mod, da, ys, u, dsk, w_glu, b_glu, o_f, o_r, z, gnw, avg, w_out):
    t, d = x.shape
    tm = min(OUTPROJ_TM, t)
    tpm = rows_per_mod // tm
    row = lambda i: (i, 0)
    const = lambda i: (0, 0)
    sw = S5_WIDTH
    return pl.pallas_call(
        _outproj_kernel,
        grid=(t // tm,),
        in_specs=[pl.BlockSpec((tm, d), row),
                  pl.BlockSpec((1, 6, d), lambda i: (i // tpm, 0, 0)),
                  pl.BlockSpec((tm, DA_WIDTH), row),
                  pl.BlockSpec((tm, sw), row), pl.BlockSpec((tm, sw), row),
                  pl.BlockSpec((1, sw), const), pl.BlockSpec((sw, sw), const), pl.BlockSpec((1, sw), const),
                  pl.BlockSpec((tm, GDN_WIDTH), row), pl.BlockSpec((tm, GDN_WIDTH), row),
                  pl.BlockSpec((tm, GDN_WIDTH), row),
                  pl.BlockSpec((1, GDN_WIDTH), const), pl.BlockSpec((GDN_WIDTH, GDN_WIDTH), const),
                  pl.BlockSpec((D_MIX, d), const)],
        out_specs=pl.BlockSpec((tm, d), row),
        out_shape=jax.ShapeDtypeStruct((t, d), F32),
        compiler_params=_params("parallel"), name="outproj",
    )(x, mod, da, ys, u, dsk, w_glu, b_glu, o_f, o_r, z, gnw, avg, w_out)


MOE_TM = 1024
MOE_EP = 4


def _pick_lowest(cur, idx, sentinel, axis):
    m = jnp.max(cur, axis=axis, keepdims=True)
    first = jnp.min(jnp.where(cur == m, idx, sentinel), axis=axis, keepdims=True)
    return idx == first


def _route(logits_t, bias):
    tm = logits_t.shape[1]
    neg = jnp.float32(-jnp.inf)
    scores = jax.nn.sigmoid(logits_t)
    biased = scores + bias
    b3 = biased.reshape(N_GROUPS, GROUP_SIZE, tm)
    eidx = lax.broadcasted_iota(jnp.int32, b3.shape, 1)
    m1 = jnp.max(b3, axis=1, keepdims=True)
    p1 = _pick_lowest(b3, eidx, GROUP_SIZE, 1)
    m2 = jnp.max(jnp.where(p1, neg, b3), axis=1, keepdims=True)
    gs = (m1 + m2).reshape(N_GROUPS, tm)
    gidx = lax.broadcasted_iota(jnp.int32, gs.shape, 0)
    gsel = jnp.zeros(gs.shape, jnp.bool_)
    cur = gs
    for _ in range(TOPK_GROUPS):
        pick = _pick_lowest(cur, gidx, N_GROUPS, 0)
        gsel = jnp.logical_or(gsel, pick)
        cur = jnp.where(pick, neg, cur)
    emask = jnp.broadcast_to(gsel.reshape(N_GROUPS, 1, tm), b3.shape)
    cur = jnp.where(emask, b3, neg).reshape(N_EXPERTS, tm)
    ridx = lax.broadcasted_iota(jnp.int32, cur.shape, 0)
    sel = jnp.zeros(cur.shape, jnp.bool_)
    for _ in range(TOP_K):
        pick = _pick_lowest(cur, ridx, N_EXPERTS, 0)
        sel = jnp.logical_or(sel, pick)
        cur = jnp.where(pick, neg, cur)
    w = jnp.where(sel, scores, 0.0)
    return w / jnp.sum(w, axis=0, keepdims=True) * ROUTED_SCALE


def _moe_kernel(x_ref, mod_ref, nw_ref, wr_ref, rb_ref, w1_ref, w3_ref, w2_ref, ws1_ref, ws3_ref, ws2_ref,
                fnw_ref, o_ref, h_sc, gate_sc, acc_sc, hid_sc, *, final_norm):
    e = pl.program_id(1)
    n_e = pl.num_programs(1)
    ep = w1_ref.shape[0]

    @pl.when(e == 0)
    def _():
        h = _rms_rows(x_ref[...]) * nw_ref[...] * (1.0 + mod_ref[0, 4:5, :]) + mod_ref[0, 3:4, :]
        hb = h.astype(BF16)
        h_sc[...] = hb
        a = jnp.dot(hb, ws1_ref[...], preferred_element_type=F32)
        b = jnp.dot(hb, ws3_ref[...], preferred_element_type=F32)
        acc_sc[...] = jnp.dot((a * _sigmoid(a) * b).astype(BF16), ws2_ref[...], preferred_element_type=F32)
        wr = wr_ref[...]
        wr_hi = wr.astype(BF16)
        wr_lo = (wr - wr_hi.astype(F32)).astype(BF16)
        h_lo = (h - hb.astype(F32)).astype(BF16)
        nt = lambda p, q: lax.dot_general(p, q, NT_DIMS, preferred_element_type=F32)
        logits_t = (nt(wr_hi, h_lo) + nt(wr_lo, hb)) + nt(wr_hi, hb)
        w = _route(logits_t, rb_ref[...])
        pad = jnp.zeros((LANES - N_EXPERTS, w.shape[1]), F32)
        gate_sc[...] = jnp.concatenate([w, pad], axis=0).T

    h = h_sc[...]
    lane = lax.broadcasted_iota(jnp.int32, (1, LANES), 1)
    for j in range(ep):
        a = jnp.dot(h, w1_ref[j], preferred_element_type=F32)
        b = jnp.dot(h, w3_ref[j], preferred_element_type=F32)
        g = jnp.sum(jnp.where(lane == e * ep + j, gate_sc[...], 0.0), axis=-1, keepdims=True)
        hid_sc[:, j * MOE_FFN:(j + 1) * MOE_FFN] = (a * _sigmoid(a) * b * g).astype(BF16)
    w2 = w2_ref[...].reshape(ep * MOE_FFN, w2_ref.shape[2])
    acc_sc[...] += jnp.dot(hid_sc[...], w2, preferred_element_type=F32)

    @pl.when(e == n_e - 1)
    def _():
        y = x_ref[...] + mod_ref[0, 5:6, :] * acc_sc[...]
        if final_norm:
            y = _rms_rows(y) * fnw_ref[...]
        o_ref[...] = y


def moe_sublayer(x, mod, rows_per_mod, norm_w, w_router_t, router_bias, weights, layer, final_w, final_norm):
    t, d = x.shape
    tm = min(MOE_TM, rows_per_mod)
    assert t % tm == 0 and rows_per_mod % tm == 0
    w1, w3, w2, ws1, ws3, ws2 = weights
    ep = MOE_EP
    assert N_EXPERTS % ep == 0
    tiles_per_mod = rows_per_mod // tm
    f = MOE_FFN
    return pl.pallas_call(
        functools.partial(_moe_kernel, final_norm=final_norm),
        grid=(t // tm, N_EXPERTS // ep),
        in_specs=[
            pl.BlockSpec((tm, d), lambda i, e: (i, 0)),
            pl.BlockSpec((1, 6, d), lambda i, e: (i // tiles_per_mod, 0, 0)),
            pl.BlockSpec((1, d), lambda i, e: (0, 0)),
            pl.BlockSpec((N_EXPERTS, d), lambda i, e: (0, 0)),
            pl.BlockSpec((N_EXPERTS, 1), lambda i, e: (0, 0)),
            pl.BlockSpec((None, ep, d, f), lambda i, e: (layer, e, 0, 0)),
            pl.BlockSpec((None, ep, d, f), lambda i, e: (layer, e, 0, 0)),
            pl.BlockSpec((None, ep, f, d), lambda i, e: (layer, e, 0, 0)),
            pl.BlockSpec((None, d, f), lambda i, e: (layer, 0, 0)),
            pl.BlockSpec((None, d, f), lambda i, e: (layer, 0, 0)),
            pl.BlockSpec((None, f, d), lambda i, e: (layer, 0, 0)),
            pl.BlockSpec((1, d), lambda i, e: (0, 0)),
        ],
        out_specs=pl.BlockSpec((tm, d), lambda i, e: (i, 0)),
        out_shape=jax.ShapeDtypeStruct((t, d), F32),
        scratch_shapes=[
            pltpu.VMEM((tm, d), BF16),
            pltpu.VMEM((tm, LANES), F32),
            pltpu.VMEM((tm, d), F32),
            pltpu.VMEM((tm, ep * MOE_FFN), BF16),
        ],
        compiler_params=_params("parallel", "arbitrary"), name="moe_sublayer",
    )(x, mod, norm_w.reshape(1, d), w_router_t, router_bias.reshape(N_EXPERTS, 1), w1, w3, w2, ws1, ws3, ws2,
      final_w.reshape(1, d))


def kernel(x, c, ctx, c_ctx, norm1_w, norm2_w, w_mod, b_mod, w_in, w_out, da_lambda, da_subln_w,
           s5_lam_re, s5_lam_im, s5_log_step, s5_b_re, s5_b_im, s5_c_re, s5_c_im, s5_d, s5_w_glu, s5_b_glu,
           gdn_conv_w, gdn_a_log, gdn_dt_bias, gdn_norm_w,
           moe_w_router, moe_router_bias, moe_w1, moe_w3, moe_w2, moe_ws1, moe_ws3, moe_ws2,
           final_norm_w):
    b, n, d = x.shape
    nc = ctx.shape[1]
    cos_t, sin_t = _rope_tables(n)
    cond = jnp.zeros((2 * b, d), F32).at[:b].set(c).at[b].set(c_ctx)
    xl = x.reshape(b * n, d)
    xc = ctx.reshape(b * nc, d)
    head_avg = jnp.kron(jnp.eye(GDN_HEADS, dtype=F32), jnp.full((GDN_DV, GDN_DV), 1.0 / GDN_DV, F32))
    s_zero = jnp.zeros((b, 2, GDN_HEADS // 2, LANES, LANES), F32)
    moe_w = tuple(w.astype(BF16) for w in (moe_w1, moe_w3, moe_w2, moe_ws1, moe_ws3, moe_ws2))
    s5_tables = _s5_tables(s5_lam_re, s5_lam_im, s5_log_step, s5_b_re, s5_b_im, s5_c_re, s5_c_im)
    for i in range(DEPTH):
        ctx_out = i < DEPTH - 1
        last = i == DEPTH - 1
        lam_init = 0.8 - 0.6 * math.exp(-0.3 * i)
        mod_all = mod_proj(cond, w_mod, b_mod, i).reshape(2 * b, 6, d)
        mod, modc = mod_all[:b], mod_all[b:b + 1]

        w_in_p = jnp.pad(w_in[i], ((0, 0), (0, IN_PAD - IN_WIDTH))).astype(BF16)
        q, k, v, u, ub, gq, z, ba = inproj(xl, mod, n, norm1_w[i], w_in_p, cos_t, sin_t, rope=True)
        qc, kc, vc, uc, ubc, gqc, zc, bac = inproj(xc, modc, b * nc, norm1_w[i], w_in_p, cos_t, sin_t, rope=False)

        lq1, lk1, lq2, lk2 = da_lambda[i].astype(F32)
        lam = jnp.exp(jnp.sum(lq1 * lk1)) - jnp.exp(jnp.sum(lq2 * lk2)) + lam_init
        lam_row = jnp.full((1, LANES), lam, F32)
        da = diff_attention(q, [kc, k], [vc, v], (nc, n), n, lam_row, da_subln_w[i], lam_init)

        ysc, ysl = s5_scan(ubc, ub, b, s5_tables, i)

        a_lanes = slice(GDN_A_LANE, 2 * GDN_A_LANE)
        alog_row = jnp.zeros((1, LANES), F32).at[0, a_lanes].set(gdn_a_log[i].astype(F32).reshape(-1))
        dtb_row = jnp.zeros((1, LANES), F32).at[0, a_lanes].set(gdn_dt_bias[i].astype(F32).reshape(-1))
        gn = gdn_prep(gq, gdn_conv_w[i], n)
        gnc = gdn_prep(gqc, gdn_conv_w[i], nc)
        ofc, orc, s_ctx = gdn_scan(gnc.reshape(b, nc, GDN_QKV_W), bac.reshape(b, nc, LANES), s_zero, alog_row, dtb_row)
        of, orv, _ = gdn_scan(gn.reshape(b, n, GDN_QKV_W), ba.reshape(b, n, LANES), s_ctx, alog_row, dtb_row)

        w_out_b = w_out[i].astype(BF16)
        dsk = s5_d[i].astype(F32).reshape(1, S5_WIDTH)
        wg = s5_w_glu[i].astype(BF16)
        bg = s5_b_glu[i].astype(F32).reshape(1, S5_WIDTH)
        gnw = jnp.tile(gdn_norm_w[i].astype(F32), GDN_HEADS).reshape(1, GDN_WIDTH)
        xl = outproj(xl, mod, n, da, ysl, u, dsk, wg, bg,
                     of.reshape(b * n, GDN_WIDTH), orv.reshape(b * n, GDN_WIDTH), z, gnw, head_avg, w_out_b)
        if ctx_out:
            dac = diff_attention(qc, [kc], [vc], (nc,), nc, lam_row, da_subln_w[i], lam_init)
            xc = outproj(xc, modc, b * nc, dac, ysc, uc, dsk, wg, bg,
                         ofc.reshape(b * nc, GDN_WIDTH), orc.reshape(b * nc, GDN_WIDTH), zc, gnw, head_avg, w_out_b)

        wr_t = moe_w_router[i].T
        xl = moe_sublayer(xl, mod, n, norm2_w[i], wr_t, moe_router_bias[i], moe_w, i, final_norm_w, last)
        if ctx_out:
            xc = moe_sublayer(xc, modc, b * nc, norm2_w[i], wr_t, moe_router_bias[i], moe_w, i, final_norm_w, False)
    return xl.reshape(b, n, d)
```

```python
import functools
import math

import jax
import jax.numpy as jnp
from jax import lax
from jax.experimental import pallas as pl
from jax.experimental.pallas import tpu as pltpu

F32 = jnp.float32
BF16 = jnp.bfloat16

D_MODEL = 1024
DEPTH = 2
GRID_W = 64
EPS = 1e-6

DA_HEADS = 4
DA_HEAD_DIM = D_MODEL // 16
DA_V_DIM = 2 * DA_HEAD_DIM
DA_WIDTH = DA_HEADS * DA_V_DIM
ROPE_THETA = 10000.0

S5_WIDTH = D_MODEL // 4
S5_GROUP = 16
S5_GROUPS = S5_WIDTH // S5_GROUP
S5_STATE = 64

GDN_HEADS = 4
GDN_DK = D_MODEL // 16
GDN_DV = D_MODEL // 16
GDN_WIDTH = GDN_HEADS * GDN_DV
GDN_CONV = 5
GDN_CHUNK = 64

D_MIX = DA_WIDTH + S5_WIDTH + GDN_WIDTH
DA_QK_W = 2 * DA_HEADS * DA_HEAD_DIM
GDN_QKV_W = 2 * GDN_HEADS * GDN_DK + GDN_HEADS * GDN_DV
IN_SIZES = (DA_QK_W, DA_QK_W, DA_WIDTH, S5_WIDTH, GDN_QKV_W, GDN_WIDTH, 2 * GDN_HEADS, 2 * GDN_HEADS)
IN_WIDTH = sum(IN_SIZES)

N_EXPERTS = 64
TOP_K = 8
N_GROUPS = 8
GROUP_SIZE = N_EXPERTS // N_GROUPS
TOPK_GROUPS = 4
MOE_FFN = D_MODEL // 4
ROUTED_SCALE = 2.5

LANES = 128
SUBLANES = 8
V7X_VMEM_BYTES = 64 * 1024 * 1024
VMEM_LIMIT_BYTES = V7X_VMEM_BYTES * 7 // 8

GDN_A_LANE = 2 * GDN_HEADS

HP = lax.Precision.HIGHEST
NT_DIMS = (((1,), (1,)), ((), ()))


def _params(*sem):
    return pltpu.CompilerParams(dimension_semantics=sem, vmem_limit_bytes=VMEM_LIMIT_BYTES)


def _sigmoid(x):
    return 0.5 * (1.0 + jnp.tanh(0.5 * x))


def _rms_rows(x):
    return x * lax.rsqrt(jnp.mean(x * x, axis=-1, keepdims=True) + EPS)


MOD_TN = 768


def _mod_kernel(c_ref, w_ref, b_ref, o_ref):
    c = c_ref[...]
    act = c * _sigmoid(c)
    o_ref[...] = jnp.dot(act, w_ref[...], precision=HP, preferred_element_type=F32) + b_ref[...]


def mod_proj(cond, w_mod, b_mod, layer):
    r, d = cond.shape
    depth, _, n = w_mod.shape
    return pl.pallas_call(
        _mod_kernel,
        grid=(n // MOD_TN,),
        in_specs=[pl.BlockSpec((r, d), lambda j: (0, 0)),
                  pl.BlockSpec((None, d, MOD_TN), lambda j: (layer, 0, j)),
                  pl.BlockSpec((None, 1, MOD_TN), lambda j: (layer, 0, j))],
        out_specs=pl.BlockSpec((r, MOD_TN), lambda j: (0, j)),
        out_shape=jax.ShapeDtypeStruct((r, n), F32),
        compiler_params=_params("parallel"), name="mod_proj",
    )(cond, w_mod, b_mod.reshape(depth, 1, n))


IN_PAD = 2944
INPROJ_TM = 512
O_Q, O_K, O_V, O_U, O_G, O_Z, O_BA = 0, 512, 1024, 1536, 1792, 2560, 2816


def _rope_apply(x, cos, sin):
    lane = lax.broadcasted_iota(jnp.int32, x.shape, 1)
    up = pltpu.roll(x, LANES - 16, 1)
    dn = pltpu.roll(x, 16, 1)
    partner = jnp.where((lane & 31) < 16, up, dn)
    return x * cos + partner * sin


def _inproj_kernel(x_ref, mod_ref, nw_ref, w_ref, cos_ref, sin_ref,
                   q_ref, k_ref, v_ref, u_ref, ub_ref, g_ref, z_ref, ba_ref, *, rope):
    h = (_rms_rows(x_ref[...]) * nw_ref[...] * (1.0 + mod_ref[0, 1:2, :]) + mod_ref[0, 0:1, :]).astype(BF16)

    def proj(lo, hi):
        return jnp.dot(h, w_ref[:, lo:hi], preferred_element_type=F32)

    scale = DA_HEAD_DIM ** -0.5 * math.log2(math.e)
    q_all = proj(O_Q, O_K)
    k_all = proj(O_K, O_V)
    for hd in range(DA_HEADS):
        lo = hd * LANES
        qs = q_all[:, lo:lo + LANES]
        ks = k_all[:, lo:lo + LANES]
        if rope:
            qs = _rope_apply(qs, cos_ref[...], sin_ref[...])
            ks = _rope_apply(ks, cos_ref[...], sin_ref[...])
        q_ref[:, lo:lo + LANES] = (qs * scale).astype(BF16)
        k_ref[:, lo:lo + LANES] = ks.astype(BF16)
    v_ref[...] = proj(O_V, O_U).astype(BF16)
    u = proj(O_U, O_G)
    u_ref[...] = u
    ub_ref[...] = u.astype(BF16)
    g_ref[...] = proj(O_G, O_Z)
    z_ref[...] = proj(O_Z, O_BA)
    ba_ref[...] = proj(O_BA, IN_PAD)


def inproj(x, mod, rows_per_mod, norm_w, w_pad, cos_t, sin_t, rope):
    t, d = x.shape
    tm = min(INPROJ_TM, t)
    tpm = rows_per_mod // tm
    npos = cos_t.shape[0] // tm
    row = lambda i: (i, 0)
    widths = (DA_QK_W, DA_QK_W, DA_WIDTH, S5_WIDTH, S5_WIDTH, GDN_QKV_W, GDN_WIDTH, LANES)
    dtypes = (BF16, BF16, BF16, F32, BF16, F32, F32, F32)
    return pl.pallas_call(
        functools.partial(_inproj_kernel, rope=rope),
        grid=(t // tm,),
        in_specs=[pl.BlockSpec((tm, d), row),
                  pl.BlockSpec((1, 6, d), lambda i: (i // tpm, 0, 0)),
                  pl.BlockSpec((1, d), lambda i: (0, 0)),
                  pl.BlockSpec((d, IN_PAD), lambda i: (0, 0)),
                  pl.BlockSpec((tm, LANES), lambda i: (i % npos, 0)),
                  pl.BlockSpec((tm, LANES), lambda i: (i % npos, 0))],
        out_specs=[pl.BlockSpec((tm, w), row) for w in widths],
        out_shape=[jax.ShapeDtypeStruct((t, w), dt) for w, dt in zip(widths, dtypes)],
        compiler_params=_params("parallel"), name="inproj",
    )(x, mod, norm_w.reshape(1, d), w_pad, cos_t, sin_t)


def _rope_tables(n):
    nf = DA_HEAD_DIM // 4
    t = jnp.arange(n, dtype=jnp.int32)
    inv = ROPE_THETA ** (-jnp.arange(nf, dtype=F32) / nf)
    ang_r = (t // GRID_W).astype(F32)[:, None] * inv
    ang_c = (t % GRID_W).astype(F32)[:, None] * inv
    cos64 = jnp.concatenate([jnp.cos(ang_r), jnp.cos(ang_r), jnp.cos(ang_c), jnp.cos(ang_c)], axis=-1)
    sin64 = jnp.concatenate([-jnp.sin(ang_r), jnp.sin(ang_r), -jnp.sin(ang_c), jnp.sin(ang_c)], axis=-1)
    return jnp.tile(cos64, (1, 2)), jnp.tile(sin64, (1, 2))


ATTN_TQ = 512
ATTN_KC = 512


def _attn_kernel(*refs, n_kv, kv_rows, chunks, lam_init):
    q_ref = refs[0]
    k_refs = refs[1:1 + n_kv]
    v_refs = refs[1 + n_kv:1 + 2 * n_kv]
    lam_ref, w_ref, o_ref, v1_sc = refs[1 + 2 * n_kv:]

    @pl.when(pl.program_id(2) == 0)
    def _():
        off = 0
        for ki, rows in enumerate(kv_rows):
            v1_sc[off:off + rows, :DA_V_DIM] = v_refs[ki][...]
            v1_sc[off:off + rows, DA_V_DIM:] = jnp.ones((rows, DA_V_DIM), BF16)
            off += rows

    q = q_ref[...]
    tq = q.shape[0]
    lane = lax.broadcasted_iota(jnp.int32, q.shape, 1)
    zero = jnp.zeros_like(q)
    qq = jnp.concatenate([jnp.where(lane < DA_HEAD_DIM, q, zero), jnp.where(lane >= DA_HEAD_DIM, q, zero)], axis=0)
    m = jnp.full((2 * tq, 1), -jnp.inf, F32)
    acc = jnp.zeros((2 * tq, 2 * DA_V_DIM), F32)
    def scores(chunk):
        ki, start, _, size = chunk
        return lax.dot_general(qq, k_refs[ki][start:start + size, :], NT_DIMS, preferred_element_type=F32)

    s_next = scores(chunks[0])
    for ci, (ki, start, off, size) in enumerate(chunks):
        s = s_next
        if ci + 1 < len(chunks):
            s_next = scores(chunks[ci + 1])
        m_new = jnp.maximum(m, jnp.max(s, axis=-1, keepdims=True))
        p = jnp.exp2(s - m_new).astype(BF16)
        acc = jnp.exp2(m - m_new) * acc + jnp.dot(p, v1_sc[off:off + size, :], preferred_element_type=F32)
        m = m_new
    o = acc[:, :DA_V_DIM] / acc[:, DA_V_DIM:]
    od = o[:tq] - lam_ref[...] * o[tq:]
    o_ref[...] = (_rms_rows(od) * w_ref[...] * (1.0 - lam_init)).astype(o_ref.dtype)


def diff_attention(q, ks, vs, kv_rows, q_rows, lam_row, subln_w, lam_init):
    t = q.shape[0]
    b = t // q_rows
    tq = min(ATTN_TQ, q_rows)
    nq = q_rows // tq
    chunks, off = [], 0
    for ki, rows in enumerate(kv_rows):
        kc = min(ATTN_KC, rows)
        chunks += [(ki, s, off + s, kc) for s in range(0, rows, kc)]
        off += rows
    qmap = lambda bi, h, qi: (bi * nq + qi, h)
    kvmap = lambda bi, h, qi: (bi, h)
    const = lambda bi, h, qi: (0, 0)
    return pl.pallas_call(
        functools.partial(_attn_kernel, n_kv=len(ks), kv_rows=tuple(kv_rows), chunks=tuple(chunks),
                          lam_init=lam_init),
        grid=(b, DA_HEADS, nq),
        in_specs=([pl.BlockSpec((tq, LANES), qmap)]
                  + [pl.BlockSpec((rows, LANES), kvmap) for rows in kv_rows] * 2
                  + [pl.BlockSpec((1, LANES), const)] * 2),
        out_specs=pl.BlockSpec((tq, LANES), qmap),
        out_shape=jax.ShapeDtypeStruct((t, DA_WIDTH), BF16),
        scratch_shapes=[pltpu.VMEM((off, 2 * DA_V_DIM), BF16)],
        compiler_params=_params("parallel", "parallel", "arbitrary"), name="diff_attention",
    )(q, *ks, *vs, lam_row, subln_w.reshape(1, LANES))


S5_LC = 64
S5_CW = S5_LC * S5_GROUP
S5_SW = 2 * S5_STATE


def _s5_discretize(lam_re, lam_im, log_step, b_re, b_im):
    lr, li = lam_re.astype(F32), lam_im.astype(F32)
    step = jnp.exp(log_step.astype(F32))[..., None]
    mag = jnp.exp(lr * step)
    ab_re, ab_im = mag * jnp.cos(li * step), mag * jnp.sin(li * step)
    den = lr * lr + li * li
    nr, ni = ab_re - 1.0, ab_im
    f_re = (nr * lr + ni * li) / den
    f_im = (ni * lr - nr * li) / den
    br, bi = b_re.astype(F32), b_im.astype(F32)
    bb_re = f_re[..., None] * br - f_im[..., None] * bi
    bb_im = f_re[..., None] * bi + f_im[..., None] * br
    return bb_re, bb_im


def _s5_tables(lam_re, lam_im, log_step, b_re, b_im, c_re, c_im):
    lc, g, nl = S5_LC, S5_GROUPS, lam_re.shape[0] * 2
    merge = lambda x: x.astype(F32).reshape((nl,) + x.shape[2:])
    lam_re, lam_im, log_step, b_re, b_im, cr, ci = map(merge, (lam_re, lam_im, log_step, b_re, b_im, c_re, c_im))
    rev = jnp.arange(nl) % 2 == 1

    def flip_rev(x, axis):
        return jnp.where(rev.reshape((1, nl) + (1,) * (x.ndim - 2)), jnp.flip(x, axis), x)

    step = jnp.exp(log_step)[..., None]
    bb_re, bb_im = _s5_discretize(lam_re, lam_im, log_step, b_re, b_im)
    tau = jnp.arange(lc + 1, dtype=F32).reshape(lc + 1, 1, 1, 1)
    mag = jnp.exp(tau * (lam_re * step))
    ang = tau * (lam_im * step)
    pr, pi = mag * jnp.cos(ang), mag * jnp.sin(ang)
    abr = pr[..., None] * bb_re - pi[..., None] * bb_im
    abi = pr[..., None] * bb_im + pi[..., None] * bb_re
    kern = (jnp.einsum('lgkp,tlgph->tlgkh', cr, abr[:lc], precision=HP)
            - jnp.einsum('lgkp,tlgph->tlgkh', ci, abi[:lc], precision=HP))
    ab = jnp.concatenate([abr[:lc], abi[:lc]], axis=3)
    bmat = flip_rev(ab[::-1], 0).transpose(1, 2, 4, 0, 3).reshape(nl, g, S5_CW, S5_SW)
    pro, pio = pr[1:, :, :, None, :], pi[1:, :, :, None, :]
    ca = jnp.concatenate([cr * pro - ci * pio, -(cr * pio + ci * pro)], axis=4)
    cmat = flip_rev(ca, 0).transpose(1, 2, 4, 3, 0).reshape(nl, g, S5_SW, S5_CW)
    strip = jnp.concatenate([jnp.zeros((lc - 1,) + kern.shape[1:], F32), kern], axis=0)
    strip = jnp.pad(flip_rev(strip, 0), ((0, 1),) + ((0, 0),) * 4).transpose(1, 2, 4, 3, 0)
    alr, ali = pr[lc], pi[lc]
    k1 = jnp.repeat(jnp.concatenate([alr, alr], axis=-1), SUBLANES, axis=1)
    k2 = jnp.repeat(jnp.concatenate([-ali, ali], axis=-1), SUBLANES, axis=1)
    return bmat.astype(BF16), cmat.astype(BF16), _s5_toeplitz(strip), k1, k2


def _s5_toeplitz_kernel(strip_ref, m_ref):
    lc = S5_LC
    lane = lax.broadcasted_iota(jnp.int32, (lc, 2 * lc), 1)
    for h in range(S5_GROUP):
        for kp in range(S5_GROUP // 2):
            halves = []
            for half in range(2):
                row = jnp.broadcast_to(strip_ref[0, 0, h, 2 * kp + half:2 * kp + half + 1, :], (lc, 2 * lc))
                base = lc + 1 if half == 0 else 1
                halves.append(pltpu.roll(row, base, 1, stride=1, stride_axis=0))
            tile = jnp.where(lane < lc, halves[0], halves[1])
            m_ref[0, 0, h * lc:(h + 1) * lc, kp * 2 * lc:(kp + 1) * 2 * lc] = tile.astype(m_ref.dtype)


def _s5_toeplitz(strips):
    nd, g = strips.shape[:2]
    return pl.pallas_call(
        _s5_toeplitz_kernel,
        grid=(nd, g),
        in_specs=[pl.BlockSpec((1, 1) + strips.shape[2:], lambda d, gi: (d, gi, 0, 0, 0))],
        out_specs=pl.BlockSpec((1, 1, S5_CW, S5_CW), lambda d, gi: (d, gi, 0, 0)),
        out_shape=jax.ShapeDtypeStruct((nd, g, S5_CW, S5_CW), BF16),
        compiler_params=_params("parallel", "parallel"), name="s5_toeplitz",
    )(strips)


def _s5_local_kernel(uc_ref, ul_ref, bm_ref, s_ref):
    u = jnp.concatenate([uc_ref[0], ul_ref[0]], axis=0)
    s = jnp.dot(u, bm_ref[0, 0], preferred_element_type=F32)
    s_ref[0] = s.reshape(s_ref.shape[1:])


def _s5_carry_kernel(s_ref, k1_ref, k2_ref, xin_ref, *, n_ctx, n_chunks):
    d = pl.program_id(0)
    k1, k2 = k1_ref[0], k2_ref[0]

    def body(k, x):
        rev = jnp.where(k < n_ctx, n_ctx - 1 - k, n_chunks + n_ctx - 1 - k)
        c = jnp.where(d == 0, k, rev)
        xin_ref[0, c] = x
        return k1 * x + k2 * pltpu.roll(x, S5_STATE, 1) + s_ref[0, c]

    lax.fori_loop(0, n_chunks, body, jnp.zeros(k1.shape, F32))


def _s5_out_kernel(uc_ref, ul_ref, m_ref, xin_ref, cm_ref, yc_ref, yl_ref):
    u = jnp.concatenate([uc_ref[0], ul_ref[0]], axis=0)
    acc = jnp.dot(u, m_ref[0, 0], preferred_element_type=F32)
    acc += jnp.dot(u, m_ref[1, 0], preferred_element_type=F32)
    for d in range(2):
        xin = xin_ref[d].reshape(u.shape[0], S5_SW).astype(BF16)
        acc += jnp.dot(xin, cm_ref[d, 0], preferred_element_type=F32)
    rc = yc_ref.shape[1]
    yc_ref[0] = acc[:rc].astype(yc_ref.dtype)
    yl_ref[0] = acc[rc:].astype(yl_ref.dtype)


def s5_scan(u_ctx, u_lat, b, tables, layer):
    bmat, cmat, mmat, k1, k2 = tables
    g, lc = S5_GROUPS, S5_LC
    n_ctx, n_lat = u_ctx.shape[0] // b, u_lat.shape[0] // b
    assert b == SUBLANES and n_lat % lc == 0 and n_ctx % lc == 0
    rc, rl = n_ctx // lc * b, n_lat // lc * b
    nch = (n_ctx + n_lat) // lc

    def to_groups(x, rows):
        x = lax.optimization_barrier(jnp.swapaxes(x.reshape(b, rows // lc, lc, S5_WIDTH), 2, 3))
        return x.reshape(b, rows // lc, g, S5_CW).transpose(2, 1, 0, 3).reshape(g, rows // lc * b, S5_CW)

    def from_groups(y, rows):
        y = lax.optimization_barrier(y.reshape(g, rows // lc, b, S5_CW).transpose(2, 1, 0, 3))
        return jnp.swapaxes(y.reshape(b, rows // lc, S5_WIDTH, lc), 2, 3).reshape(b * rows, S5_WIDTH)

    ugc, ugl = to_groups(u_ctx, n_ctx), to_groups(u_lat, n_lat)
    s = pl.pallas_call(
        _s5_local_kernel,
        grid=(2, g),
        in_specs=[pl.BlockSpec((1, rc, S5_CW), lambda d, gi: (gi, 0, 0)),
                  pl.BlockSpec((1, rl, S5_CW), lambda d, gi: (gi, 0, 0)),
                  pl.BlockSpec((1, 1, S5_CW, S5_SW), lambda d, gi: (2 * layer + d, gi, 0, 0))],
        out_specs=pl.BlockSpec((1, nch, b, S5_SW), lambda d, gi: (d, 0, gi, 0)),
        out_shape=jax.ShapeDtypeStruct((2, nch, g * b, S5_SW), F32),
        compiler_params=_params("arbitrary", "arbitrary"), name="s5_local",
    )(ugc, ugl, bmat)
    xin = pl.pallas_call(
        functools.partial(_s5_carry_kernel, n_ctx=n_ctx // lc, n_chunks=nch),
        grid=(2,),
        in_specs=[pl.BlockSpec((1, nch, g * b, S5_SW), lambda d: (d, 0, 0, 0)),
                  pl.BlockSpec((1, g * b, S5_SW), lambda d: (2 * layer + d, 0, 0)),
                  pl.BlockSpec((1, g * b, S5_SW), lambda d: (2 * layer + d, 0, 0))],
        out_specs=pl.BlockSpec((1, nch, g * b, S5_SW), lambda d: (d, 0, 0, 0)),
        out_shape=jax.ShapeDtypeStruct((2, nch, g * b, S5_SW), F32),
        compiler_params=_params("arbitrary"), name="s5_carry",
    )(s, k1, k2)
    yc, yl = pl.pallas_call(
        _s5_out_kernel,
        grid=(g,),
        in_specs=[pl.BlockSpec((1, rc, S5_CW), lambda gi: (gi, 0, 0)),
                  pl.BlockSpec((1, rl, S5_CW), lambda gi: (gi, 0, 0)),
                  pl.BlockSpec((2, 1, S5_CW, S5_CW), lambda gi: (layer, gi, 0, 0)),
                  pl.BlockSpec((2, nch, b, S5_SW), lambda gi: (0, 0, gi, 0)),
                  pl.BlockSpec((2, 1, S5_SW, S5_CW), lambda gi: (layer, gi, 0, 0))],
        out_specs=[pl.BlockSpec((1, rc, S5_CW), lambda gi: (gi, 0, 0)),
                   pl.BlockSpec((1, rl, S5_CW), lambda gi: (gi, 0, 0))],
        out_shape=[jax.ShapeDtypeStruct((g, rc, S5_CW), BF16), jax.ShapeDtypeStruct((g, rl, S5_CW), BF16)],
        compiler_params=_params("arbitrary"), name="s5_out",
    )(ugc, ugl, mmat, xin, cmat)
    return from_groups(yc, n_ctx), from_groups(yl, n_lat)


def _gdn_prep_kernel(x_ref, w_ref, o_ref):
    j = pl.program_id(1)
    x = x_ref[...]
    n = x.shape[0]
    row = lax.broadcasted_iota(jnp.int32, x.shape, 0)
    half = GDN_CONV // 2
    acc = x * w_ref[half:half + 1, :]
    for sh in range(1, half + 1):
        acc += jnp.where(row >= sh, pltpu.roll(x, sh, 0), 0.0) * w_ref[half - sh:half - sh + 1, :]
        acc += jnp.where(row < n - sh, pltpu.roll(x, n - sh, 0), 0.0) * w_ref[half + sh:half + sh + 1, :]
    a = acc * _sigmoid(acc)
    lane = lax.broadcasted_iota(jnp.int32, x.shape, 1)
    lo = lane < GDN_DK
    sq = a * a
    s_lo = jnp.sum(jnp.where(lo, sq, 0.0), axis=-1, keepdims=True)
    s_hi = jnp.sum(jnp.where(lo, 0.0, sq), axis=-1, keepdims=True)
    nrm = a * lax.rsqrt(jnp.where(lo, s_lo, s_hi) + EPS)
    q_blocks = GDN_HEADS * GDN_DK // LANES
    nrm = nrm * jnp.where(j < q_blocks, GDN_DK ** -0.5, 1.0)
    o_ref[...] = jnp.where(j < 2 * q_blocks, nrm, a)


def gdn_prep(qkv, conv_w, seg):
    t, w = qkv.shape
    return pl.pallas_call(
        _gdn_prep_kernel,
        grid=(t // seg, w // LANES),
        in_specs=[pl.BlockSpec((seg, LANES), lambda s, j: (s, j)),
                  pl.BlockSpec((GDN_CONV, LANES), lambda s, j: (0, j))],
        out_specs=pl.BlockSpec((seg, LANES), lambda s, j: (s, j)),
        out_shape=jax.ShapeDtypeStruct((t, w), F32),
        compiler_params=_params("parallel", "arbitrary"), name="gdn_prep",
    )(qkv, conv_w)


GDN_SHIFT = GDN_CHUNK.bit_length() - 1
assert 1 << GDN_SHIFT == GDN_CHUNK == GDN_DK == GDN_DV


def _bd(x):
    x2 = jnp.concatenate([x, x], axis=0)
    r = lax.broadcasted_iota(jnp.int32, x2.shape, 0)
    l = lax.broadcasted_iota(jnp.int32, x2.shape, 1)
    return jnp.where((r >> GDN_SHIFT) == (l >> GDN_SHIFT), x2, jnp.zeros_like(x2))


def _mm(a, b):
    return jnp.dot(a.astype(BF16), b.astype(BF16), preferred_element_type=F32)


def _split3(x):
    hi = x.astype(BF16)
    r1 = x - hi.astype(F32)
    mid = r1.astype(BF16)
    lo = (r1 - mid.astype(F32)).astype(BF16)
    return hi, mid, lo


def _dot01(m01, x):
    hi, mid, lo = _split3(x)
    dot = lambda p: jnp.dot(m01, p, preferred_element_type=F32)
    return (dot(lo) + dot(mid)) + dot(hi)


def _gdn_chunks(insts):
    c = GDN_CHUNK
    n = len(insts)
    every = range(n)
    q, k, v, beta, gcol, s_bd, rev = (list(t) for t in zip(*insts))
    i = lax.broadcasted_iota(jnp.int32, (c, LANES), 0)
    j = lax.broadcasted_iota(jnp.int32, (c, LANES), 1) & (c - 1)
    ti = lax.broadcasted_iota(jnp.int32, (c, c), 0)
    tj = lax.broadcasted_iota(jnp.int32, (c, c), 1)
    causal = [i <= j if r else i >= j for r in rev]
    strict = [i < j if r else i > j for r in rev]
    upto = [i >= j if r else i <= j for r in rev]
    lmat = [(tj >= ti if r else tj <= ti).astype(BF16) for r in rev]
    ones = jnp.ones((c, c), BF16)
    eye = jnp.where(i == j, 1.0, 0.0)
    lg = [_dot01(lmat[t], gcol[t]) for t in every]
    rg = [_dot01(ones, jnp.where(upto[t], gcol[t], 0.0)) for t in every]
    decay = [jnp.where(causal[t], jnp.exp(jnp.where(causal[t], lg[t] - rg[t], 0.0)), 0.0) for t in every]
    kb = [k[t] * beta[t] for t in every]
    k_bd = [_bd(k[t].astype(BF16)) for t in every]
    a = [jnp.where(strict[t], lax.dot_general(kb[t].astype(BF16), k_bd[t], NT_DIMS, preferred_element_type=F32)
                   * decay[t], 0.0) for t in every]
    qk = [lax.dot_general(q[t].astype(BF16), k_bd[t], NT_DIMS, preferred_element_type=F32) * decay[t] for t in every]
    eg = [jnp.exp(lg[t]) for t in every]
    base = 3
    same = [(i >> sh) == (j >> sh) for sh in range(base, GDN_SHIFT)]
    x = [jnp.where(same[0], a[t], 0.0) for t in every]
    p = [eye - x[t] for t in every]
    for _ in range(base - 1):
        x = [_mm(x[t], _bd(x[t])) for t in every]
        p = [p[t] + _mm(p[t], _bd(x[t])) for t in every]
    for lvl, inner in enumerate(same):
        outer_same = same[lvl + 1] if lvl + 1 < len(same) else True
        join = jnp.logical_and(outer_same, jnp.logical_not(inner))
        tl = [_mm(p[t], _bd(jnp.where(join, a[t], 0.0))) for t in every]
        p = [p[t] - _mm(tl[t], _bd(p[t])) for t in every]
    u = [_mm(p[t], _bd(v[t] * beta[t])) for t in every]
    w = [_mm(p[t], _bd(kb[t] * eg[t])) for t in every]
    v_new = [u[t] - _mm(w[t], s_bd[t]) for t in every]
    o_state = [_mm(q[t] * eg[t], s_bd[t]) for t in every]
    o = [o_state[t] + _mm(qk[t], _bd(v_new[t])) for t in every]
    g_last = [lg[t][0:1, :] if rev[t] else lg[t][c - 1:c, :] for t in every]
    k_dec = [k[t] * jnp.exp(g_last[t] - lg[t]) for t in every]
    upd = [jnp.dot(k_dec[t].T.astype(BF16), v_new[t].astype(BF16), preferred_element_type=F32) for t in every]
    r2 = lax.broadcasted_iota(jnp.int32, (LANES, LANES), 0)
    l2 = lax.broadcasted_iota(jnp.int32, (LANES, LANES), 1)
    diag = (r2 >> GDN_SHIFT) == (l2 >> GDN_SHIFT)
    s_new = [s_bd[t] * jnp.exp(g_last[t]) + jnp.where(diag, upd[t], 0.0) for t in every]
    return list(zip(o, s_new))


GDN_BATCH_UNROLL = 8


def _gdn_scan_kernel(qf, kf, vf, baf, qr, kr, vr, bar, alog_ref, dtb_ref, s0_ref,
                     of_ref, or_ref, sfin_ref, s_sc):
    c = pl.program_id(0)

    @pl.when(c == 0)
    def _():
        s_sc[...] = s0_ref[...]

    lane = lax.broadcasted_iota(jnp.int32, (GDN_CHUNK, LANES), 1)
    first = lane < GDN_DK
    ins = ((qf, kf, vf, baf, of_ref), (qr, kr, vr, bar, or_ref))

    def body(it, carry):
        work = []
        for bb in range(GDN_BATCH_UNROLL):
            b = it * GDN_BATCH_UNROLL + bb
            for d in range(2):
                q_ref, k_ref, v_ref, ba_ref, o_ref = ins[d]
                ba = ba_ref[b]
                bsig = _sigmoid(ba)
                sp = ba + dtb_ref[...]
                gall = -jnp.exp(alog_ref[...]) * (jnp.maximum(sp, 0.0) + jnp.log(1.0 + jnp.exp(-jnp.abs(sp))))
                for hp in range(GDN_HEADS // 2):
                    col = d * GDN_HEADS + 2 * hp
                    beta = jnp.where(first, bsig[:, col:col + 1], bsig[:, col + 1:col + 2])
                    ga = GDN_A_LANE + col
                    gcol = jnp.where(first, gall[:, ga:ga + 1], gall[:, ga + 1:ga + 2])
                    sl = slice(hp * LANES, (hp + 1) * LANES)
                    work.append((b, d, hp, sl, o_ref, (q_ref[b, :, sl], k_ref[b, :, sl], v_ref[b, :, sl],
                                                       beta, gcol, s_sc[b, d, hp], d == 1)))
        done = _gdn_chunks([args for (_, _, _, _, _, args) in work])
        for (b, d, hp, sl, o_ref, _), (o, s_new) in zip(work, done):
            o_ref[b, :, sl] = o
            s_sc[b, d, hp] = s_new
        return carry

    lax.fori_loop(0, s_sc.shape[0] // GDN_BATCH_UNROLL, body, 0)

    @pl.when(c == pl.num_programs(0) - 1)
    def _():
        sfin_ref[...] = s_sc[...]


def gdn_scan(qkvn, ba, s0, alog_row, dtb_row):
    b, l, _ = qkvn.shape
    nch = l // GDN_CHUNK
    blk = (b, GDN_CHUNK, GDN_WIDTH)
    fwd = lambda col: (lambda c: (0, c, col))
    bwd = lambda col: (lambda c: (0, nch - 1 - c, col))
    st = pl.BlockSpec(s0.shape, lambda c: (0, 0, 0, 0, 0))
    return pl.pallas_call(
        _gdn_scan_kernel,
        grid=(nch,),
        in_specs=[pl.BlockSpec(blk, fwd(0)), pl.BlockSpec(blk, fwd(1)), pl.BlockSpec(blk, fwd(2)),
                  pl.BlockSpec((b, GDN_CHUNK, LANES), fwd(0)),
                  pl.BlockSpec(blk, bwd(0)), pl.BlockSpec(blk, bwd(1)), pl.BlockSpec(blk, bwd(2)),
                  pl.BlockSpec((b, GDN_CHUNK, LANES), bwd(0)),
                  pl.BlockSpec((1, LANES), lambda c: (0, 0)), pl.BlockSpec((1, LANES), lambda c: (0, 0)), st],
        out_specs=[pl.BlockSpec(blk, fwd(0)), pl.BlockSpec(blk, bwd(0)), st],
        out_shape=[jax.ShapeDtypeStruct((b, l, GDN_WIDTH), F32)] * 2 + [jax.ShapeDtypeStruct(s0.shape, F32)],
        scratch_shapes=[pltpu.VMEM(s0.shape, F32)],
        compiler_params=_params("arbitrary"), name="gdn_scan",
    )(qkvn, qkvn, qkvn, ba, qkvn, qkvn, qkvn, ba, alog_row, dtb_row, s0)


OUTPROJ_TM = 512


def _outproj_kernel(x_ref, mod_ref, da_ref, ys_ref, u_ref, dsk_ref, wg_ref, bg_ref,
                    of_ref, or_ref, z_ref, gnw_ref, avg_ref, w_ref, o_ref):
    y = ys_ref[...] + u_ref[...] * dsk_ref[...]
    zz = 0.5 * y * (1.0 + jnp.tanh(math.sqrt(2.0 / math.pi) * (y + 0.044715 * (y * y * y))))
    glu = zz * _sigmoid(jnp.dot(zz.astype(BF16), wg_ref[...], preferred_element_type=F32) + bg_ref[...])
    o = of_ref[...] + or_ref[...]
    avg = avg_ref[...].astype(BF16)
    hi, mid, lo = _split3(o * o)
    mean_of = lambda p: jnp.dot(p, avg, preferred_element_type=F32)
    ms = (mean_of(lo) + mean_of(mid)) + mean_of(hi)
    z = z_ref[...]
    gd = o * lax.rsqrt(ms + EPS) * gnw_ref[...] * (z * _sigmoid(z))
    acc = jnp.dot(da_ref[...], w_ref[0:DA_WIDTH, :], preferred_element_type=F32)
    acc += jnp.dot(glu.astype(BF16), w_ref[DA_WIDTH:DA_WIDTH + S5_WIDTH, :], preferred_element_type=F32)
    acc += jnp.dot(gd.astype(BF16), w_ref[DA_WIDTH + S5_WIDTH:, :], preferred_element_type=F32)
    o_ref[...] = x_ref[...] + mod_ref[0, 2:3, :] * acc


def outproj(x, mod, rows_per_mod, da, ys, u, dsk, w_glu, b_glu, o_f, o_r, z, gnw, avg, w_out):
    t, d = x.shape
    tm = min(OUTPROJ_TM, t)
    tpm = rows_per_mod // tm
    row = lambda i: (i, 0)
    const = lambda i: (0, 0)
    sw = S5_WIDTH
    return pl.pallas_call(
        _outproj_kernel,
        grid=(t // tm,),
        in_specs=[pl.BlockSpec((tm, d), row),
                  pl.BlockSpec((1, 6, d), lambda i: (i // tpm, 0, 0)),
                  pl.BlockSpec((tm, DA_WIDTH), row),
                  pl.BlockSpec((tm, sw), row), pl.BlockSpec((tm, sw), row),
                  pl.BlockSpec((1, sw), const), pl.BlockSpec((sw, sw), const), pl.BlockSpec((1, sw), const),
                  pl.BlockSpec((tm, GDN_WIDTH), row), pl.BlockSpec((tm, GDN_WIDTH), row),
                  pl.BlockSpec((tm, GDN_WIDTH), row),
                  pl.BlockSpec((1, GDN_WIDTH), const), pl.BlockSpec((GDN_WIDTH, GDN_WIDTH), const),
                  pl.BlockSpec((D_MIX, d), const)],
        out_specs=pl.BlockSpec((tm, d), row),
        out_shape=jax.ShapeDtypeStruct((t, d), F32),
        compiler_params=_params("parallel"), name="outproj",
    )(x, mod, da, ys, u, dsk, w_glu, b_glu, o_f, o_r, z, gnw, avg, w_out)


MOE_TM = 1024
MOE_EP = 8


def _pick_lowest(cur, idx, sentinel, axis):
    m = jnp.max(cur, axis=axis, keepdims=True)
    first = jnp.min(jnp.where(cur == m, idx, sentinel), axis=axis, keepdims=True)
    return idx == first


def _route(logits_t, bias):
    tm = logits_t.shape[1]
    neg = jnp.float32(-jnp.inf)
    scores = jax.nn.sigmoid(logits_t)
    biased = scores + bias
    b3 = biased.reshape(N_GROUPS, GROUP_SIZE, tm)
    eidx = lax.broadcasted_iota(jnp.int32, b3.shape, 1)
    m1 = jnp.max(b3, axis=1, keepdims=True)
    p1 = _pick_lowest(b3, eidx, GROUP_SIZE, 1)
    m2 = jnp.max(jnp.where(p1, neg, b3), axis=1, keepdims=True)
    gs = (m1 + m2).reshape(N_GROUPS, tm)
    gidx = lax.broadcasted_iota(jnp.int32, gs.shape, 0)
    gsel = jnp.zeros(gs.shape, jnp.bool_)
    cur = gs
    for _ in range(TOPK_GROUPS):
        pick = _pick_lowest(cur, gidx, N_GROUPS, 0)
        gsel = jnp.logical_or(gsel, pick)
        cur = jnp.where(pick, neg, cur)
    emask = jnp.broadcast_to(gsel.reshape(N_GROUPS, 1, tm), b3.shape)
    cur = jnp.where(emask, b3, neg).reshape(N_EXPERTS, tm)
    ridx = lax.broadcasted_iota(jnp.int32, cur.shape, 0)
    sel = jnp.zeros(cur.shape, jnp.bool_)
    for _ in range(TOP_K):
        pick = _pick_lowest(cur, ridx, N_EXPERTS, 0)
        sel = jnp.logical_or(sel, pick)
        cur = jnp.where(pick, neg, cur)
    w = jnp.where(sel, scores, 0.0)
    return w / jnp.sum(w, axis=0, keepdims=True) * ROUTED_SCALE


def _moe_kernel(x_ref, mod_ref, nw_ref, wr_ref, rb_ref, w1_ref, w3_ref, w2_ref, ws1_ref, ws3_ref, ws2_ref,
                fnw_ref, o_ref, h_sc, gate_sc, acc_sc, hid_sc, *, final_norm):
    e = pl.program_id(1)
    n_e = pl.num_programs(1)
    ep = w1_ref.shape[0]

    @pl.when(e == 0)
    def _():
        h = _rms_rows(x_ref[...]) * nw_ref[...] * (1.0 + mod_ref[0, 4:5, :]) + mod_ref[0, 3:4, :]
        hb = h.astype(BF16)
        h_sc[...] = hb
        a = jnp.dot(hb, ws1_ref[...], preferred_element_type=F32)
        b = jnp.dot(hb, ws3_ref[...], preferred_element_type=F32)
        acc_sc[...] = jnp.dot((a * _sigmoid(a) * b).astype(BF16), ws2_ref[...], preferred_element_type=F32)
        wr = wr_ref[...]
        wr_hi = wr.astype(BF16)
        wr_lo = (wr - wr_hi.astype(F32)).astype(BF16)
        h_lo = (h - hb.astype(F32)).astype(BF16)
        nt = lambda p, q: lax.dot_general(p, q, NT_DIMS, preferred_element_type=F32)
        logits_t = (nt(wr_hi, h_lo) + nt(wr_lo, hb)) + nt(wr_hi, hb)
        w = _route(logits_t, rb_ref[...])
        pad = jnp.zeros((LANES - N_EXPERTS, w.shape[1]), F32)
        gate_sc[...] = jnp.concatenate([w, pad], axis=0).T

    h = h_sc[...]
    lane = lax.broadcasted_iota(jnp.int32, (1, LANES), 1)
    for j in range(ep):
        a = jnp.dot(h, w1_ref[j], preferred_element_type=F32)
        b = jnp.dot(h, w3_ref[j], preferred_element_type=F32)
        g = jnp.sum(jnp.where(lane == e * ep + j, gate_sc[...], 0.0), axis=-1, keepdims=True)
        hid_sc[:, j * MOE_FFN:(j + 1) * MOE_FFN] = (a * _sigmoid(a) * b * g).astype(BF16)
    w2 = w2_ref[...].reshape(ep * MOE_FFN, w2_ref.shape[2])
    acc_sc[...] += jnp.dot(hid_sc[...], w2, preferred_element_type=F32)

    @pl.when(e == n_e - 1)
    def _():
        y = x_ref[...] + mod_ref[0, 5:6, :] * acc_sc[...]
        if final_norm:
            y = _rms_rows(y) * fnw_ref[...]
        o_ref[...] = y


def moe_sublayer(x, mod, rows_per_mod, norm_w, w_router_t, router_bias, weights, layer, final_w, final_norm):
    t, d = x.shape
    tm = min(MOE_TM, rows_per_mod)
    assert t % tm == 0 and rows_per_mod % tm == 0
    w1, w3, w2, ws1, ws3, ws2 = weights
    ep = MOE_EP
    assert N_EXPERTS % ep == 0
    tiles_per_mod = rows_per_mod // tm
    f = MOE_FFN
    return pl.pallas_call(
        functools.partial(_moe_kernel, final_norm=final_norm),
        grid=(t // tm, N_EXPERTS // ep),
        in_specs=[
            pl.BlockSpec((tm, d), lambda i, e: (i, 0), pipeline_mode=pl.Buffered(1)),
            pl.BlockSpec((1, 6, d), lambda i, e: (i // tiles_per_mod, 0, 0)),
            pl.BlockSpec((1, d), lambda i, e: (0, 0)),
            pl.BlockSpec((N_EXPERTS, d), lambda i, e: (0, 0)),
            pl.BlockSpec((N_EXPERTS, 1), lambda i, e: (0, 0)),
            pl.BlockSpec((None, ep, d, f), lambda i, e: (layer, e, 0, 0)),
            pl.BlockSpec((None, ep, d, f), lambda i, e: (layer, e, 0, 0)),
            pl.BlockSpec((None, ep, f, d), lambda i, e: (layer, e, 0, 0)),
            pl.BlockSpec((None, d, f), lambda i, e: (layer, 0, 0)),
            pl.BlockSpec((None, d, f), lambda i, e: (layer, 0, 0)),
            pl.BlockSpec((None, f, d), lambda i, e: (layer, 0, 0)),
            pl.BlockSpec((1, d), lambda i, e: (0, 0)),
        ],
        out_specs=pl.BlockSpec((tm, d), lambda i, e: (i, 0), pipeline_mode=pl.Buffered(1)),
        out_shape=jax.ShapeDtypeStruct((t, d), F32),
        scratch_shapes=[
            pltpu.VMEM((tm, d), BF16),
            pltpu.VMEM((tm, LANES), F32),
            pltpu.VMEM((tm, d), F32),
            pltpu.VMEM((tm, ep * MOE_FFN), BF16),
        ],
        compiler_params=_params("parallel", "arbitrary"), name="moe_sublayer",
    )(x, mod, norm_w.reshape(1, d), w_router_t, router_bias.reshape(N_EXPERTS, 1), w1, w3, w2, ws1, ws3, ws2,
      final_w.reshape(1, d))


def kernel(x, c, ctx, c_ctx, norm1_w, norm2_w, w_mod, b_mod, w_in, w_out, da_lambda, da_subln_w,
           s5_lam_re, s5_lam_im, s5_log_step, s5_b_re, s5_b_im, s5_c_re, s5_c_im, s5_d, s5_w_glu, s5_b_glu,
           gdn_conv_w, gdn_a_log, gdn_dt_bias, gdn_norm_w,
           moe_w_router, moe_router_bias, moe_w1, moe_w3, moe_w2, moe_ws1, moe_ws3, moe_ws2,
           final_norm_w):
    b, n, d = x.shape
    nc = ctx.shape[1]
    cos_t, sin_t = _rope_tables(n)
    cond = jnp.zeros((2 * b, d), F32).at[:b].set(c).at[b].set(c_ctx)
    xl = x.reshape(b * n, d)
    xc = ctx.reshape(b * nc, d)
    head_avg = jnp.kron(jnp.eye(GDN_HEADS, dtype=F32), jnp.full((GDN_DV, GDN_DV), 1.0 / GDN_DV, F32))
    s_zero = jnp.zeros((b, 2, GDN_HEADS // 2, LANES, LANES), F32)
    moe_w = tuple(w.astype(BF16) for w in (moe_w1, moe_w3, moe_w2, moe_ws1, moe_ws3, moe_ws2))
    s5_tables = _s5_tables(s5_lam_re, s5_lam_im, s5_log_step, s5_b_re, s5_b_im, s5_c_re, s5_c_im)
    for i in range(DEPTH):
        ctx_out = i < DEPTH - 1
        last = i == DEPTH - 1
        lam_init = 0.8 - 0.6 * math.exp(-0.3 * i)
        mod_all = mod_proj(cond, w_mod, b_mod, i).reshape(2 * b, 6, d)
        mod, modc = mod_all[:b], mod_all[b:b + 1]

        w_in_p = jnp.pad(w_in[i], ((0, 0), (0, IN_PAD - IN_WIDTH))).astype(BF16)
        q, k, v, u, ub, gq, z, ba = inproj(xl, mod, n, norm1_w[i], w_in_p, cos_t, sin_t, rope=True)
        qc, kc, vc, uc, ubc, gqc, zc, bac = inproj(xc, modc, b * nc, norm1_w[i], w_in_p, cos_t, sin_t, rope=False)

        lq1, lk1, lq2, lk2 = da_lambda[i].astype(F32)
        lam = jnp.exp(jnp.sum(lq1 * lk1)) - jnp.exp(jnp.sum(lq2 * lk2)) + lam_init
        lam_row = jnp.full((1, LANES), lam, F32)
        da = diff_attention(q, [kc, k], [vc, v], (nc, n), n, lam_row, da_subln_w[i], lam_init)

        ysc, ysl = s5_scan(ubc, ub, b, s5_tables, i)

        a_lanes = slice(GDN_A_LANE, 2 * GDN_A_LANE)
        alog_row = jnp.zeros((1, LANES), F32).at[0, a_lanes].set(gdn_a_log[i].astype(F32).reshape(-1))
        dtb_row = jnp.zeros((1, LANES), F32).at[0, a_lanes].set(gdn_dt_bias[i].astype(F32).reshape(-1))
        gn = gdn_prep(gq, gdn_conv_w[i], n)
        gnc = gdn_prep(gqc, gdn_conv_w[i], nc)
        ofc, orc, s_ctx = gdn_scan(gnc.reshape(b, nc, GDN_QKV_W), bac.reshape(b, nc, LANES), s_zero, alog_row, dtb_row)
        of, orv, _ = gdn_scan(gn.reshape(b, n, GDN_QKV_W), ba.reshape(b, n, LANES), s_ctx, alog_row, dtb_row)

        w_out_b = w_out[i].astype(BF16)
        dsk = s5_d[i].astype(F32).reshape(1, S5_WIDTH)
        wg = s5_w_glu[i].astype(BF16)
        bg = s5_b_glu[i].astype(F32).reshape(1, S5_WIDTH)
        gnw = jnp.tile(gdn_norm_w[i].astype(F32), GDN_HEADS).reshape(1, GDN_WIDTH)
        xl = outproj(xl, mod, n, da, ysl, u, dsk, wg, bg,
                     of.reshape(b * n, GDN_WIDTH), orv.reshape(b * n, GDN_WIDTH), z, gnw, head_avg, w_out_b)
        if ctx_out:
            dac = diff_attention(qc, [kc], [vc], (nc,), nc, lam_row, da_subln_w[i], lam_init)
            xc = outproj(xc, modc, b * nc, dac, ysc, uc, dsk, wg, bg,
                         ofc.reshape(b * nc, GDN_WIDTH), orc.reshape(b * nc, GDN_WIDTH), zc, gnw, head_avg, w_out_b)

        wr_t = moe_w_router[i].T
        xl = moe_sublayer(xl, mod, n, norm2_w[i], wr_t, moe_router_bias[i], moe_w, i, final_norm_w, last)
        if ctx_out:
            xc = moe_sublayer(xc, modc, b * nc, norm2_w[i], wr_t, moe_router_bias[i], moe_w, i, final_norm_w, False)
    return xl.reshape(b, n, d)
```
